```python
import jax, jax.numpy as jnp
from jax import lax
import numpy as np

D_MODEL = 1024
BATCH = 2
SEQ = 8192
DEPTH = 4

D_MIX = D_MODEL
GLA_HEADS = 4
GLA_DK = 64
GLA_DV = 128
GLA_QK = GLA_HEADS * GLA_DK
GLA_WIDTH = GLA_HEADS * GLA_DV
GLA_LOWRANK = 16
GLA_TAU = 16.0
GLA_CHUNK = 64
SSD_HEADS = 8
SSD_HEADDIM = 64
SSD_WIDTH = SSD_HEADS * SSD_HEADDIM
SSD_GROUPS = 2
SSD_STATE = 64
SSD_BC = SSD_GROUPS * SSD_STATE
SSD_CONV = 4
SSD_CONV_DIM = SSD_WIDTH + 2 * SSD_BC
SSD_CHUNK = 128
D_FF = 3584
N_EXPERTS = 8
TOP_K = 2
EPS = 1e-6

IN_SIZES = (GLA_QK, GLA_QK, GLA_WIDTH, GLA_WIDTH, GLA_LOWRANK,
            SSD_WIDTH, SSD_WIDTH, SSD_BC, SSD_BC, SSD_HEADS)
IN_SPLITS = [int(s) for s in np.cumsum(IN_SIZES)[:-1]]
D_IN_PROJ = int(sum(IN_SIZES))

kernel_name = "hybrid_gla_ssd_adaln_moe_trunk"


def rmsnorm(x, w):
    x32 = x.astype(jnp.float32)
    r = x32 * lax.rsqrt(jnp.mean(x32 * x32, axis=-1, keepdims=True) + EPS)
    return (r * w.astype(jnp.float32)).astype(x.dtype)


def inter_chunk_states(decay, contrib):
    d = jnp.moveaxis(decay, 1, 0)
    u = jnp.moveaxis(contrib, 1, 0)

    def step(s, inp):
        dn, un = inp
        return dn * s + un, s

    _, prev = lax.scan(step, jnp.zeros_like(u[0]), (d, u))
    return jnp.moveaxis(prev, 0, 1)


def gla_chunked(q, k, v, log_a):
    bsz, seq, nh, dk = q.shape
    dv = v.shape[-1]
    nc = seq // GLA_CHUNK
    shp = lambda t: t.astype(jnp.float32).reshape(bsz, nc, GLA_CHUNK, nh, t.shape[-1])
    q, k, v, log_a = shp(q) * (dk ** -0.5), shp(k), shp(v), shp(log_a)
    b = jnp.cumsum(log_a, axis=2)
    qe = q * jnp.exp(b)
    ke = k * jnp.exp(-b)
    causal = jnp.tril(jnp.ones((GLA_CHUNK, GLA_CHUNK), bool))
    att = jnp.einsum('bnihd,bnjhd->bnhij', qe, ke)
    att = jnp.where(causal, att, 0.0)
    o_intra = jnp.einsum('bnhij,bnjhv->bnihv', att, v)
    b_last = b[:, :, -1]
    k_tail = k * jnp.exp(b_last[:, :, None] - b)
    contrib = jnp.einsum('bnjhd,bnjhv->bnhdv', k_tail, v)
    s_prev = inter_chunk_states(jnp.exp(b_last)[..., None], contrib)
    o_inter = jnp.einsum('bnihd,bnhdv->bnihv', qe, s_prev)
    return (o_intra + o_inter).reshape(bsz, seq, nh, dv)


def ssd_chunked(xh, dt, a, bm, cm):
    bsz, seq, nh, p = xh.shape
    rep = nh // bm.shape[2]
    nc = seq // SSD_CHUNK
    f = lambda t: t.astype(jnp.float32).reshape((bsz, nc, SSD_CHUNK) + t.shape[2:])
    xh, dt = f(xh), f(dt)
    bh = f(jnp.repeat(bm, rep, axis=2))
    ch = f(jnp.repeat(cm, rep, axis=2))
    cum = jnp.cumsum(dt * a, axis=2)
    causal = jnp.tril(jnp.ones((SSD_CHUNK, SSD_CHUNK), bool))[:, :, None]
    seg = cum[:, :, :, None, :] - cum[:, :, None, :, :]
    decay = jnp.exp(jnp.where(causal, seg, -jnp.inf))
    xdt = xh * dt[..., None]
    scores = jnp.einsum('bnihs,bnjhs->bnijh', ch, bh) * decay
    y_intra = jnp.einsum('bnijh,bnjhp->bnihp', scores, xdt)
    cum_last = cum[:, :, -1]
    contrib = jnp.einsum('bnjhs,bnjhp->bnhps',
                         bh * jnp.exp(cum_last[:, :, None] - cum)[..., None], xdt)
    s_prev = inter_chunk_states(jnp.exp(cum_last)[..., None, None], contrib)
    y_inter = jnp.einsum('bnihs,bnhps->bnihp', ch * jnp.exp(cum)[..., None], s_prev)
    return (y_intra + y_inter).reshape(bsz, seq, nh, p)


def causal_depthwise_conv(u, w, b):
    kw, cdim = w.shape
    out = lax.conv_general_dilated(u, w.astype(u.dtype)[:, None, :], window_strides=(1,),
                                   padding=[(kw - 1, 0)],
                                   dimension_numbers=('NWC', 'WIO', 'NWC'),
                                   feature_group_count=cdim)
    return out + b.astype(u.dtype)


def hybrid_mixer(h, w_in, gla_a_w, gla_a_b, gla_norm_w, conv_w, conv_b,
                 dt_bias, a_log, d_skip, ssd_norm_w, w_out):
    bsz, seq, _ = h.shape
    proj = h @ w_in
    q, k, v, g, alr, z, xs, bm, cm, dt = jnp.split(proj, IN_SPLITS, axis=-1)
    log_a = jax.nn.log_sigmoid((alr @ gla_a_w + gla_a_b).astype(jnp.float32)) / GLA_TAU
    o = gla_chunked(q.reshape(bsz, seq, GLA_HEADS, GLA_DK),
                    k.reshape(bsz, seq, GLA_HEADS, GLA_DK),
                    v.reshape(bsz, seq, GLA_HEADS, GLA_DV),
                    log_a.reshape(bsz, seq, GLA_HEADS, GLA_DK))
    o = rmsnorm(o, gla_norm_w) * jax.nn.silu(g.astype(jnp.float32).reshape(bsz, seq, GLA_HEADS, GLA_DV))
    o = o.reshape(bsz, seq, GLA_WIDTH)
    xbc = jax.nn.silu(causal_depthwise_conv(jnp.concatenate([xs, bm, cm], axis=-1), conv_w, conv_b))
    xs, bm, cm = jnp.split(xbc, [SSD_WIDTH, SSD_WIDTH + SSD_BC], axis=-1)
    dt = jax.nn.softplus(dt.astype(jnp.float32) + dt_bias.astype(jnp.float32))
    a = -jnp.exp(a_log.astype(jnp.float32))
    xh = xs.reshape(bsz, seq, SSD_HEADS, SSD_HEADDIM)
    y = ssd_chunked(xh, dt, a,
                    bm.reshape(bsz, seq, SSD_GROUPS, SSD_STATE),
                    cm.reshape(bsz, seq, SSD_GROUPS, SSD_STATE))
    y = y + d_skip.astype(jnp.float32)[:, None] * xh.astype(jnp.float32)
    y = y.reshape(bsz, seq, SSD_WIDTH) * jax.nn.silu(z.astype(jnp.float32))
    y = rmsnorm(y, ssd_norm_w)
    merged = jnp.concatenate([o, y], axis=-1).astype(h.dtype)
    return merged @ w_out


def swiglu(t, w_gate, w_up, w_down):
    return (jax.nn.silu(t @ w_gate) * (t @ w_up)) @ w_down


def moe_swiglu(h, router_w, w_gate, w_up, w_down):
    bsz, seq, d = h.shape
    t = h.reshape(bsz * seq, d)
    logits = (t @ router_w).astype(jnp.float32)
    top_v, top_i = lax.top_k(logits, TOP_K)
    top_p = jax.nn.softmax(top_v, axis=-1)
    gates = jnp.sum(jax.nn.one_hot(top_i, N_EXPERTS, dtype=jnp.float32) * top_p[..., None], axis=1)
    gates = gates.astype(h.dtype)
    out = jnp.zeros_like(t)
    for e in range(N_EXPERTS):
        out = out + gates[:, e:e + 1] * swiglu(t, w_gate[e], w_up[e], w_down[e])
    return out.reshape(bsz, seq, d)


def setup_inputs(seed: int = 0) -> dict:
    key = jax.random.key(seed)
    ks = iter(list(jax.random.split(key, 40)))
    f32 = jnp.float32
    nrm = lambda shape, scale: jax.random.normal(next(ks), shape, f32) * scale
    n_dense = (DEPTH + 1) // 2
    n_moe = DEPTH // 2
    d = D_MODEL
    x = nrm((BATCH, SEQ, d), 1.0)
    c = nrm((BATCH, d), 1.0)
    ada_w = nrm((DEPTH, d, 6 * d), 0.5 * d ** -0.5)
    ada_b = nrm((DEPTH, 6 * d), 0.02)
    norm1_w = 1.0 + nrm((DEPTH, d), 0.02)
    w_in = nrm((DEPTH, d, D_IN_PROJ), d ** -0.5)
    gla_a_w = nrm((DEPTH, GLA_LOWRANK, GLA_QK), GLA_LOWRANK ** -0.5)
    gla_a_b = nrm((DEPTH, GLA_QK), 0.1)
    gla_norm_w = 1.0 + nrm((DEPTH, GLA_DV), 0.02)
    conv_w = nrm((DEPTH, SSD_CONV, SSD_CONV_DIM), SSD_CONV ** -0.5)
    conv_b = nrm((DEPTH, SSD_CONV_DIM), 0.02)
    dt0 = jnp.exp(jax.random.uniform(next(ks), (DEPTH, SSD_HEADS), f32,
                                     np.log(1e-3).astype(np.float32), np.log(1e-1).astype(np.float32)))
    dt_bias = dt0 + jnp.log(-jnp.expm1(-dt0))
    a_log = jnp.log(jax.random.uniform(next(ks), (DEPTH, SSD_HEADS), f32, 1.0, 16.0))
    d_skip = 1.0 + nrm((DEPTH, SSD_HEADS), 0.1)
    ssd_norm_w = 1.0 + nrm((DEPTH, SSD_WIDTH), 0.02)
    w_out = nrm((DEPTH, D_MIX, d), D_MIX ** -0.5)
    norm2_w = 1.0 + nrm((DEPTH, d), 0.02)
    ffn_w_gate = nrm((n_dense, d, D_FF), d ** -0.5)
    ffn_w_up = nrm((n_dense, d, D_FF), d ** -0.5)
    ffn_w_down = nrm((n_dense, D_FF, d), D_FF ** -0.5)
    router_w = nrm((n_moe, d, N_EXPERTS), d ** -0.5)
    moe_w_gate = nrm((n_moe, N_EXPERTS, d, D_FF), d ** -0.5)
    moe_w_up = nrm((n_moe, N_EXPERTS, d, D_FF), d ** -0.5)
    moe_w_down = nrm((n_moe, N_EXPERTS, D_FF, d), D_FF ** -0.5)
    final_norm_w = 1.0 + nrm((d,), 0.02)
    return {"x": x, "c": c, "ada_w": ada_w, "ada_b": ada_b, "norm1_w": norm1_w,
            "w_in": w_in, "gla_a_w": gla_a_w, "gla_a_b": gla_a_b, "gla_norm_w": gla_norm_w,
            "conv_w": conv_w, "conv_b": conv_b, "dt_bias": dt_bias, "a_log": a_log,
            "d_skip": d_skip, "ssd_norm_w": ssd_norm_w, "w_out": w_out, "norm2_w": norm2_w,
            "ffn_w_gate": ffn_w_gate, "ffn_w_up": ffn_w_up, "ffn_w_down": ffn_w_down,
            "router_w": router_w, "moe_w_gate": moe_w_gate, "moe_w_up": moe_w_up,
            "moe_w_down": moe_w_down, "final_norm_w": final_norm_w}


def reference(x, c, ada_w, ada_b, norm1_w, w_in, gla_a_w, gla_a_b, gla_norm_w,
              conv_w, conv_b, dt_bias, a_log, d_skip, ssd_norm_w, w_out, norm2_w,
              ffn_w_gate, ffn_w_up, ffn_w_down, router_w, moe_w_gate, moe_w_up,
              moe_w_down, final_norm_w):
    s = jax.nn.silu(c)
    for l in range(DEPTH):
        mod = s @ ada_w[l] + ada_b[l]
        sh1, sc1, g1, sh2, sc2, g2 = [m[:, None, :] for m in jnp.split(mod, 6, axis=-1)]
        h = rmsnorm(x, norm1_w[l]) * (1.0 + sc1) + sh1
        mix = hybrid_mixer(h, w_in[l], gla_a_w[l], gla_a_b[l], gla_norm_w[l], conv_w[l],
                           conv_b[l], dt_bias[l], a_log[l], d_skip[l], ssd_norm_w[l], w_out[l])
        x = x + g1 * mix
        h = rmsnorm(x, norm2_w[l]) * (1.0 + sc2) + sh2
        if l % 2 == 0:
            i = l // 2
            ff = swiglu(h, ffn_w_gate[i], ffn_w_up[i], ffn_w_down[i])
        else:
            i = l // 2
            ff = moe_swiglu(h, router_w[i], moe_w_gate[i], moe_w_up[i], moe_w_down[i])
        x = x + g2 * ff
    return rmsnorm(x, final_norm_w)
```

```python
import functools

import jax
import jax.numpy as jnp
from jax import lax
from jax.experimental import pallas as pl
from jax.experimental.pallas import tpu as pltpu

D_MODEL = 1024
BATCH = 2
SEQ = 8192
DEPTH = 4
TOKENS = BATCH * SEQ

GLA_HEADS = 4
GLA_DK = 64
GLA_DV = 128
GLA_QK = GLA_HEADS * GLA_DK
GLA_WIDTH = GLA_HEADS * GLA_DV
GLA_LOWRANK = 16
GLA_TAU = 16.0
GLA_CHUNK = 64

SSD_HEADS = 8
SSD_HEADDIM = 64
SSD_WIDTH = SSD_HEADS * SSD_HEADDIM
SSD_GROUPS = 2
SSD_STATE = 64
SSD_BC = SSD_GROUPS * SSD_STATE
SSD_CONV = 4
SSD_CONV_DIM = SSD_WIDTH + 2 * SSD_BC
SSD_CHUNK = 128

D_FF = 3584
N_EXPERTS = 8
EPS = 1e-6

LANES = 128
GLA_COLS = 2 * GLA_QK + 2 * GLA_WIDTH + LANES
SSD_COLS = 2 * SSD_WIDTH + 2 * SSD_BC + LANES
G_Q, G_K, G_V, G_G, G_A = 0, GLA_QK, 2 * GLA_QK, 2 * GLA_QK + GLA_WIDTH, 2 * GLA_QK + 2 * GLA_WIDTH
S_Z, S_X, S_DT = 0, SSD_WIDTH, SSD_WIDTH + SSD_CONV_DIM

TM_PROJ = 512
TB_SCAN = 512
TM_FFN = 1024
TF_FFN = 512
VMEM_LIMIT = 56 * 1024 * 1024

F32 = jnp.float32
BF16 = jnp.bfloat16


def _dot(a, b):
    return jnp.dot(a, b, preferred_element_type=F32)


def _dot_nt(a, b):
    return lax.dot_general(a, b, (((1,), (1,)), ((), ())), preferred_element_type=F32)


def _dot_tn(a, b):
    return lax.dot_general(a, b, (((0,), (0,)), ((), ())), preferred_element_type=F32)


def _split(a):
    hi = a.astype(BF16)
    lo = (a - hi.astype(F32)).astype(BF16)
    return hi, lo


def _dot3(a, b):
    a_hi, a_lo = _split(a)
    b_hi, b_lo = _split(b)
    return _dot(a_hi, b_hi) + _dot(a_lo, b_hi) + _dot(a_hi, b_lo)


def _dot_exact_rhs(a, b_bf16):
    a_hi, a_lo = _split(a)
    return _dot(a_hi, b_bf16) + _dot(a_lo, b_bf16)


def _dot_exact_lhs(a_bf16, b):
    b_hi, b_lo = _split(b)
    return _dot(a_bf16, b_hi) + _dot(a_bf16, b_lo)


def _silu(x):
    return x * jax.nn.sigmoid(x)


def _softplus(x):
    return jnp.maximum(x, 0.0) + jnp.log1p(jnp.exp(-jnp.abs(x)))


def _norm_mod(x, w, scale, shift):
    ms = jnp.mean(x * x, axis=-1, keepdims=True)
    return (x * lax.rsqrt(ms + EPS) * w) * (1.0 + scale) + shift


def _params(*sem):
    return pltpu.CompilerParams(dimension_semantics=sem, vmem_limit_bytes=VMEM_LIMIT)


def _adaln_kernel(c_ref, w_ref, b_ref, o_ref):
    s = _silu(c_ref[...])
    o_ref[0] = _dot3(s, w_ref[0]) + b_ref[0]


def _adaln(c_pad, ada_w, ada_b):
    n_col = 6 * D_MODEL // D_MODEL
    return pl.pallas_call(
        _adaln_kernel,
        out_shape=jax.ShapeDtypeStruct((DEPTH, 8, 6 * D_MODEL), F32),
        grid=(DEPTH, n_col),
        in_specs=[
            pl.BlockSpec((8, D_MODEL), lambda l, j: (0, 0)),
            pl.BlockSpec((1, D_MODEL, D_MODEL), lambda l, j: (l, 0, j)),
            pl.BlockSpec((1, 1, D_MODEL), lambda l, j: (l, 0, j)),
        ],
        out_specs=pl.BlockSpec((1, 8, D_MODEL), lambda l, j: (l, 0, j)),
        compiler_params=_params("arbitrary", "arbitrary"),
        name="adaln",
    )(c_pad, ada_w, ada_b.reshape(DEPTH, 1, 6 * D_MODEL))


def _inproj_kernel(x_ref, mod_ref, nw_ref, w_ref, og_ref, os_ref):
    mod = mod_ref[0, 0]
    h = _norm_mod(x_ref[...], nw_ref[...], mod[1:2], mod[0:1]).astype(BF16)
    for c0 in range(0, GLA_COLS, 512):
        c1 = min(c0 + 512, GLA_COLS)
        og_ref[:, c0:c1] = _dot(h, w_ref[:, c0:c1])
    for c0 in range(0, SSD_COLS, 512):
        c1 = min(c0 + 512, SSD_COLS)
        os_ref[:, c0:c1] = _dot(h, w_ref[:, GLA_COLS + c0:GLA_COLS + c1])


def _inproj(x, mod_l, norm_w, w_in_p):
    tiles_per_batch = SEQ // TM_PROJ
    return pl.pallas_call(
        _inproj_kernel,
        out_shape=(jax.ShapeDtypeStruct((TOKENS, GLA_COLS), F32),
                   jax.ShapeDtypeStruct((TOKENS, SSD_COLS), F32)),
        grid=(TOKENS // TM_PROJ,),
        in_specs=[
            pl.BlockSpec((TM_PROJ, D_MODEL), lambda i: (i, 0)),
            pl.BlockSpec((1, 1, 6, D_MODEL), lambda i: (i // tiles_per_batch, 0, 0, 0)),
            pl.BlockSpec((1, D_MODEL), lambda i: (0, 0)),
            pl.BlockSpec((D_MODEL, GLA_COLS + SSD_COLS), lambda i: (0, 0)),
        ],
        out_specs=(pl.BlockSpec((TM_PROJ, GLA_COLS), lambda i: (i, 0)),
                   pl.BlockSpec((TM_PROJ, SSD_COLS), lambda i: (i, 0))),
        compiler_params=_params("arbitrary"),
        name="inproj",
    )(x, mod_l, norm_w, w_in_p)


def _gla_kernel(pg_ref, aw_ref, ab_ref, nw_ref, o_ref, la_scr, st_scr):
    @pl.when(pl.program_id(1) == 0)
    def _():
        st_scr[...] = jnp.zeros_like(st_scr)

    pre = _dot3(pg_ref[:, G_A:G_A + LANES], aw_ref[...]) + ab_ref[...]
    la_scr[...] = -_softplus(-pre) * (1.0 / GLA_TAU)

    c = GLA_CHUNK
    row = lax.broadcasted_iota(jnp.int32, (c, c), 0)
    col = lax.broadcasted_iota(jnp.int32, (c, c), 1)
    causal = row >= col
    tril = jnp.where(causal, 1.0, 0.0).astype(BF16)
    low_q = lax.broadcasted_iota(jnp.int32, (c, LANES), 1) < GLA_DK
    low_s = lax.broadcasted_iota(jnp.int32, (GLA_DV, LANES), 1) < GLA_DK
    nw = nw_ref[...]

    def chunk(ci, carry):
        r0 = pl.multiple_of(ci * c, c)
        rows = pl.ds(r0, c)
        b_all = _dot_exact_lhs(tril, la_scr[rows, :])
        for p in range(GLA_HEADS // 2):
            b = b_all[:, p * LANES:(p + 1) * LANES]
            b_last = b[c - 1:c, :]
            q = pg_ref[rows, G_Q + p * LANES:G_Q + (p + 1) * LANES] * (GLA_DK ** -0.5)
            k = pg_ref[rows, G_K + p * LANES:G_K + (p + 1) * LANES]
            qe = q * jnp.exp(b)
            ke = (k * jnp.exp(-b)).astype(BF16)
            k_tail = (k * jnp.exp(b_last - b)).astype(BF16)
            st = st_scr[p]
            st_bf = st.astype(BF16)
            contrib = []
            for j in range(2):
                h = 2 * p + j
                qm = jnp.where(low_q if j == 0 else jnp.logical_not(low_q), qe, 0.0).astype(BF16)
                att = jnp.where(causal, _dot_nt(qm, ke), 0.0).astype(BF16)
                v = pg_ref[rows, G_V + h * GLA_DV:G_V + (h + 1) * GLA_DV].astype(BF16)
                o = _dot(att, v) + _dot_nt(qm, st_bf)
                ms = jnp.mean(o * o, axis=-1, keepdims=True)
                g = pg_ref[rows, G_G + h * GLA_DV:G_G + (h + 1) * GLA_DV]
                o = (o * lax.rsqrt(ms + EPS) * nw) * _silu(g)
                o_ref[rows, h * GLA_DV:(h + 1) * GLA_DV] = o.astype(BF16)
                contrib.append(_dot_tn(v, k_tail))
            st_scr[p] = st * jnp.exp(b_last) + jnp.where(low_s, contrib[0], contrib[1])
        return carry

    lax.fori_loop(0, TB_SCAN // c, chunk, 0)


def _gla(pg, a_w, a_b, norm_w):
    nblk = SEQ // TB_SCAN
    return pl.pallas_call(
        _gla_kernel,
        out_shape=jax.ShapeDtypeStruct((TOKENS, GLA_WIDTH), BF16),
        grid=(BATCH, nblk),
        in_specs=[
            pl.BlockSpec((TB_SCAN, GLA_COLS), lambda b, i: (b * nblk + i, 0)),
            pl.BlockSpec((LANES, GLA_QK), lambda b, i: (0, 0)),
            pl.BlockSpec((1, GLA_QK), lambda b, i: (0, 0)),
            pl.BlockSpec((1, GLA_DV), lambda b, i: (0, 0)),
        ],
        out_specs=pl.BlockSpec((TB_SCAN, GLA_WIDTH), lambda b, i: (b * nblk + i, 0)),
        scratch_shapes=[pltpu.VMEM((TB_SCAN, GLA_QK), F32),
                        pltpu.VMEM((GLA_HEADS // 2, GLA_DV, LANES), F32)],
        compiler_params=_params("arbitrary", "arbitrary"),
        name="gla",
    )(pg, a_w, a_b, norm_w)


def _ssd_kernel(ps_ref, cw_ref, cb_ref, dtb_ref, alog_ref, dsk_ref, nw_ref, e64_ref, e128_ref,
                o_ref, xbc_scr, act_scr, st_scr):
    first = pl.program_id(1) == 0
    halo = 8

    @pl.when(first)
    def _():
        xbc_scr[0:halo, :] = jnp.zeros((halo, SSD_CONV_DIM), F32)
        st_scr[...] = jnp.zeros_like(st_scr)

    @pl.when(jnp.logical_not(first))
    def _():
        xbc_scr[0:halo, :] = xbc_scr[TB_SCAN:TB_SCAN + halo, :]

    xbc_scr[halo:halo + TB_SCAN, :] = ps_ref[:, S_X:S_X + SSD_CONV_DIM]
    conv = cb_ref[...] + cw_ref[0:1, :] * xbc_scr[halo - 3:halo - 3 + TB_SCAN, :]
    for kk in range(1, SSD_CONV):
        conv = conv + cw_ref[kk:kk + 1, :] * xbc_scr[halo - 3 + kk:halo - 3 + kk + TB_SCAN, :]
    act_scr[...] = _silu(conv)

    c = SSD_CHUNK
    row = lax.broadcasted_iota(jnp.int32, (c, c), 0)
    col = lax.broadcasted_iota(jnp.int32, (c, c), 1)
    causal = row >= col
    tril = jnp.where(causal, 1.0, 0.0).astype(BF16)
    head_lane = col < SSD_HEADS
    low_half = col < SSD_HEADDIM
    st_row = lax.broadcasted_iota(jnp.int32, (c, SSD_WIDTH), 0)
    st_col = lax.broadcasted_iota(jnp.int32, (c, SSD_WIDTH), 1)
    blockdiag = (st_row < SSD_STATE) == (st_col < SSD_WIDTH // SSD_GROUPS)
    a_neg = -jnp.exp(alog_ref[...])
    heads_per_group = SSD_HEADS // SSD_GROUPS

    def chunk(ci, carry):
        r0 = pl.multiple_of(ci * c, c)
        rows = pl.ds(r0, c)
        dt = jnp.where(head_lane, _softplus(ps_ref[rows, S_DT:S_DT + LANES] + dtb_ref[...]), 0.0)
        cum = _dot_exact_lhs(tril, dt * a_neg)
        cum_t = cum.T
        cum64 = _dot_exact_rhs(cum, e64_ref[...])
        dt64 = _dot_exact_rhs(dt, e64_ref[...])
        cum_col = _dot_exact_rhs(cum, e128_ref[...])
        cl64 = cum64[c - 1:c, :]
        xs = act_scr[rows, 0:SSD_WIDTH]
        bm = act_scr[rows, SSD_WIDTH:SSD_WIDTH + SSD_BC].astype(BF16)
        cm = act_scr[rows, SSD_WIDTH + SSD_BC:SSD_CONV_DIM]
        xdt = xs * dt64
        xdt_bf = xdt.astype(BF16)
        y_parts = []
        for g in range(SSD_GROUPS):
            cm_g = jnp.where(low_half if g == 0 else jnp.logical_not(low_half), cm, 0.0).astype(BF16)
            scores = _dot_nt(cm_g, bm)
            for pp in range(heads_per_group // 2):
                p = g * (heads_per_group // 2) + pp
                ys = []
                for j in range(2):
                    h = 2 * p + j
                    seg = cum_col[:, h * LANES:(h + 1) * LANES] - cum_t[h:h + 1, :]
                    decay = jnp.exp(jnp.where(causal, seg, -jnp.inf))
                    ys.append(_dot((scores * decay).astype(BF16), xdt_bf[:, p * LANES:(p + 1) * LANES]))
                y_parts.append(jnp.where(low_half, ys[0], ys[1]))
        y = jnp.concatenate(y_parts, axis=1)
        st = st_scr[...]
        y = y + _dot(cm.astype(BF16), st.astype(BF16)) * jnp.exp(cum64)
        contrib = _dot_tn(bm, (xdt * jnp.exp(cl64 - cum64)).astype(BF16))
        st_scr[...] = st * jnp.exp(cl64) + jnp.where(blockdiag, contrib, 0.0)
        y = y + dsk_ref[...] * xs
        y = y * _silu(ps_ref[rows, S_Z:S_Z + SSD_WIDTH])
        ms = jnp.mean(y * y, axis=-1, keepdims=True)
        o_ref[rows, :] = (y * lax.rsqrt(ms + EPS) * nw_ref[...]).astype(BF16)
        return carry

    lax.fori_loop(0, TB_SCAN // c, chunk, 0)


def _ssd(ps, conv_w, conv_b, dt_bias, a_log, d_skip, norm_w, e64, e128):
    nblk = SEQ // TB_SCAN
    const = lambda b, i: (0, 0)
    return pl.pallas_call(
        _ssd_kernel,
        out_shape=jax.ShapeDtypeStruct((TOKENS, SSD_WIDTH), BF16),
        grid=(BATCH, nblk),
        in_specs=[
            pl.BlockSpec((TB_SCAN, SSD_COLS), lambda b, i: (b * nblk + i, 0)),
            pl.BlockSpec((SSD_CONV, SSD_CONV_DIM), const),
            pl.BlockSpec((1, SSD_CONV_DIM), const),
            pl.BlockSpec((1, LANES), const),
            pl.BlockSpec((1, LANES), const),
            pl.BlockSpec((1, SSD_WIDTH), const),
            pl.BlockSpec((1, SSD_WIDTH), const),
            pl.BlockSpec((LANES, SSD_WIDTH), const),
            pl.BlockSpec((LANES, SSD_HEADS * LANES), const),
        ],
        out_specs=pl.BlockSpec((TB_SCAN, SSD_WIDTH), lambda b, i: (b * nblk + i, 0)),
        scratch_shapes=[pltpu.VMEM((TB_SCAN + 8, SSD_CONV_DIM), F32),
                        pltpu.VMEM((TB_SCAN, SSD_CONV_DIM), F32),
                        pltpu.VMEM((SSD_BC, SSD_WIDTH), F32)],
        compiler_params=_params("arbitrary", "arbitrary"),
        name="ssd",
    )(ps, conv_w, conv_b, dt_bias, a_log, d_skip, norm_w, e64, e128)


def _outproj_kernel(x_ref, og_ref, oy_ref, wo_ref, mod_ref, nw_ref, *rest, with_router):
    if with_router:
        rw_ref, xo_ref, h_ref, gates_ref = rest
    else:
        xo_ref, h_ref = rest
    mod = mod_ref[0, 0]
    mix = _dot(og_ref[...], wo_ref[0:GLA_WIDTH, :]) + _dot(oy_ref[...], wo_ref[GLA_WIDTH:, :])
    x_new = x_ref[...] + mod[2:3] * mix
    xo_ref[...] = x_new
    h = _norm_mod(x_new, nw_ref[...], mod[4:5], mod[3:4])
    h_ref[...] = h.astype(BF16)
    if with_router:
        logits = _dot3(h, rw_ref[...])
        lane = lax.broadcasted_iota(jnp.int32, logits.shape, 1)
        lg = jnp.where(lane < N_EXPERTS, logits, -jnp.inf)
        v1 = jnp.max(lg, axis=-1, keepdims=True)
        i1 = jnp.min(jnp.where(lg == v1, lane, LANES), axis=-1, keepdims=True)
        lg2 = jnp.where(lane == i1, -jnp.inf, lg)
        v2 = jnp.max(lg2, axis=-1, keepdims=True)
        i2 = jnp.min(jnp.where(lg2 == v2, lane, LANES), axis=-1, keepdims=True)
        e2 = jnp.exp(v2 - v1)
        p1 = 1.0 / (1.0 + e2)
        p2 = e2 / (1.0 + e2)
        gates_ref[...] = jnp.where(lane == i1, p1, 0.0) + jnp.where(lane == i2, p2, 0.0)


def _outproj(x, og, oy, w_out, mod_l, norm_w, router_w):
    tiles_per_batch = SEQ // TM_PROJ
    with_router = router_w is not None
    tile = lambda i: (i, 0)
    const = lambda i: (0, 0)
    in_specs = [
        pl.BlockSpec((TM_PROJ, D_MODEL), tile),
        pl.BlockSpec((TM_PROJ, GLA_WIDTH), tile),
        pl.BlockSpec((TM_PROJ, SSD_WIDTH), tile),
        pl.BlockSpec((D_MODEL, D_MODEL), const),
        pl.BlockSpec((1, 1, 6, D_MODEL), lambda i: (i // tiles_per_batch, 0, 0, 0)),
        pl.BlockSpec((1, D_MODEL), const),
    ]
    out_shape = [jax.ShapeDtypeStruct((TOKENS, D_MODEL), F32),
                 jax.ShapeDtypeStruct((TOKENS, D_MODEL), BF16)]
    out_specs = [pl.BlockSpec((TM_PROJ, D_MODEL), tile), pl.BlockSpec((TM_PROJ, D_MODEL), tile)]
    args = [x, og, oy, w_out, mod_l, norm_w]
    if with_router:
        in_specs.append(pl.BlockSpec((D_MODEL, LANES), const))
        out_shape.append(jax.ShapeDtypeStruct((TOKENS, LANES), F32))
        out_specs.append(pl.BlockSpec((TM_PROJ, LANES), tile))
        args.append(router_w)
    return pl.pallas_call(
        functools.partial(_outproj_kernel, with_router=with_router),
        out_shape=tuple(out_shape),
        grid=(TOKENS // TM_PROJ,),
        in_specs=in_specs,
        out_specs=tuple(out_specs),
        compiler_params=_params("arbitrary"),
        name="outproj_router" if with_router else "outproj",
    )(*args)


def _ffn_kernel(h_ref, x_ref, mod_ref, *rest, n_exp):
    if n_exp:
        gates_ref, wg_ref, wu_ref, wd_ref, o_ref, acc_ref = rest
        e, j = pl.program_id(1), pl.program_id(2)
        last_e = e == n_exp - 1
        wg, wu, wd = wg_ref[0, 0], wu_ref[0, 0], wd_ref[0, 0]
    else:
        wg_ref, wu_ref, wd_ref, o_ref, acc_ref = rest
        e, j = 0, pl.program_id(1)
        last_e = True
        wg, wu, wd = wg_ref[0], wu_ref[0], wd_ref[0]
    h = h_ref[...]
    a = (_silu(_dot(h, wg.astype(BF16))) * _dot(h, wu.astype(BF16))).astype(BF16)
    d = _dot(a, wd.astype(BF16))
    if n_exp:
        lane = lax.broadcasted_iota(jnp.int32, gates_ref.shape, 1)
        gate = jnp.sum(jnp.where(lane == e, gates_ref[...], 0.0), axis=-1, keepdims=True)
        d = gate * d
    start = jnp.logical_and(e == 0, j == 0)

    @pl.when(start)
    def _():
        acc_ref[...] = d

    @pl.when(jnp.logical_not(start))
    def _():
        acc_ref[...] += d

    @pl.when(jnp.logical_and(last_e, j == D_FF // TF_FFN - 1))
    def _():
        o_ref[...] = x_ref[...] + mod_ref[0, 0, 5:6, :] * acc_ref[...]


def _ffn(h, x, mod_l, w_gate, w_up, w_down, layer_idx, gates=None):
    tiles_per_batch = SEQ // TM_FFN
    n_ff = D_FF // TF_FFN
    moe = gates is not None
    if moe:
        grid = (TOKENS // TM_FFN, N_EXPERTS, n_ff)
        tile = lambda i, e, j: (i, 0)
        modm = lambda i, e, j: (i // tiles_per_batch, 0, 0, 0)
        w_specs = [pl.BlockSpec((1, 1, D_MODEL, TF_FFN), lambda i, e, j: (layer_idx, e, 0, j)),
                   pl.BlockSpec((1, 1, D_MODEL, TF_FFN), lambda i, e, j: (layer_idx, e, 0, j)),
                   pl.BlockSpec((1, 1, TF_FFN, D_MODEL), lambda i, e, j: (layer_idx, e, j, 0))]
        extra_specs = [pl.BlockSpec((TM_FFN, LANES), tile)]
        extra_args = [gates]
        sem = ("arbitrary", "arbitrary", "arbitrary")
    else:
        grid = (TOKENS // TM_FFN, n_ff)
        tile = lambda i, j: (i, 0)
        modm = lambda i, j: (i // tiles_per_batch, 0, 0, 0)
        w_specs = [pl.BlockSpec((1, D_MODEL, TF_FFN), lambda i, j: (layer_idx, 0, j)),
                   pl.BlockSpec((1, D_MODEL, TF_FFN), lambda i, j: (layer_idx, 0, j)),
                   pl.BlockSpec((1, TF_FFN, D_MODEL), lambda i, j: (layer_idx, j, 0))]
        extra_specs, extra_args = [], []
        sem = ("arbitrary", "arbitrary")
    return pl.pallas_call(
        functools.partial(_ffn_kernel, n_exp=N_EXPERTS if moe else 0),
        out_shape=jax.ShapeDtypeStruct((TOKENS, D_MODEL), F32),
        grid=grid,
        in_specs=[pl.BlockSpec((TM_FFN, D_MODEL), tile),
                  pl.BlockSpec((TM_FFN, D_MODEL), tile),
                  pl.BlockSpec((1, 1, 6, D_MODEL), modm)] + extra_specs + w_specs,
        out_specs=pl.BlockSpec((TM_FFN, D_MODEL), tile),
        scratch_shapes=[pltpu.VMEM((TM_FFN, D_MODEL), F32)],
        compiler_params=_params(*sem),
        name="moe_ffn" if moe else "dense_ffn",
    )(h, x, mod_l, *extra_args, w_gate, w_up, w_down)


def _final_norm_kernel(x_ref, w_ref, o_ref):
    x = x_ref[...]
    ms = jnp.mean(x * x, axis=-1, keepdims=True)
    o_ref[...] = x * lax.rsqrt(ms + EPS) * w_ref[...]


def _final_norm(x, w):
    return pl.pallas_call(
        _final_norm_kernel,
        out_shape=jax.ShapeDtypeStruct((TOKENS, D_MODEL), F32),
        grid=(TOKENS // TM_FFN,),
        in_specs=[pl.BlockSpec((TM_FFN, D_MODEL), lambda i: (i, 0)),
                  pl.BlockSpec((1, D_MODEL), lambda i: (0, 0))],
        out_specs=pl.BlockSpec((TM_FFN, D_MODEL), lambda i: (i, 0)),
        compiler_params=_params("arbitrary"),
        name="final_norm",
    )(x, w)


def _pad_cols(a, width):
    return jnp.pad(a, [(0, 0)] * (a.ndim - 1) + [(0, width - a.shape[-1])])


def _relayout_w_in(w_in):
    o = 0
    parts = []
    for name, size in (("q", GLA_QK), ("k", GLA_QK), ("v", GLA_WIDTH), ("g", GLA_WIDTH), ("a", GLA_LOWRANK),
                       ("z", SSD_WIDTH), ("x", SSD_WIDTH), ("b", SSD_BC), ("c", SSD_BC), ("dt", SSD_HEADS)):
        piece = w_in[:, :, o:o + size]
        if size < LANES:
            piece = _pad_cols(piece, LANES)
        parts.append(piece)
        o += size
    return jnp.concatenate(parts, axis=-1).astype(BF16)


def kernel(x, c, ada_w, ada_b, norm1_w, w_in, gla_a_w, gla_a_b, gla_norm_w, conv_w, conv_b, dt_bias, a_log,
           d_skip, ssd_norm_w, w_out, norm2_w, ffn_w_gate, ffn_w_up, ffn_w_down, router_w, moe_w_gate,
           moe_w_up, moe_w_down, final_norm_w):
    xt = x.reshape(TOKENS, D_MODEL)
    c_pad = jnp.pad(c, ((0, 8 - BATCH), (0, 0)))
    mod = _adaln(c_pad, ada_w, ada_b)[:, :BATCH].reshape(DEPTH, BATCH, 6, D_MODEL)

    w_in_p = _relayout_w_in(w_in)
    w_out_bf = w_out.astype(BF16)
    a_w_p = jnp.pad(gla_a_w, ((0, 0), (0, LANES - GLA_LOWRANK), (0, 0)))
    dtb_p = _pad_cols(dt_bias, LANES)
    alog_p = _pad_cols(a_log, LANES)
    dsk_p = jnp.repeat(d_skip, SSD_HEADDIM, axis=-1)
    rw_p = _pad_cols(router_w, LANES)
    lane_head = jnp.arange(LANES)[:, None]
    e64 = (lane_head == jnp.arange(SSD_WIDTH)[None, :] // SSD_HEADDIM).astype(BF16)
    e128 = (lane_head == jnp.arange(SSD_HEADS * LANES)[None, :] // LANES).astype(BF16)

    for l in range(DEPTH):
        mod_l = mod[l].reshape(BATCH, 1, 6, D_MODEL)
        pg, ps = _inproj(xt, mod_l, norm1_w[l][None], w_in_p[l])
        og = _gla(pg, a_w_p[l], gla_a_b[l][None], gla_norm_w[l][None])
        oy = _ssd(ps, conv_w[l], conv_b[l][None], dtb_p[l][None], alog_p[l][None], dsk_p[l][None],
                  ssd_norm_w[l][None], e64, e128)
        i = l // 2
        if l % 2 == 0:
            xt, h2 = _outproj(xt, og, oy, w_out_bf[l], mod_l, norm2_w[l][None], None)
            xt = _ffn(h2, xt, mod_l, ffn_w_gate, ffn_w_up, ffn_w_down, i)
        else:
            xt, h2, gates = _outproj(xt, og, oy, w_out_bf[l], mod_l, norm2_w[l][None], rw_p[i])
            xt = _ffn(h2, xt, mod_l, moe_w_gate, moe_w_up, moe_w_down, i, gates=gates)
    return _final_norm(xt, final_norm_w[None]).reshape(BATCH, SEQ, D_MODEL)
```

```python
import functools

import jax
import jax.numpy as jnp
from jax import lax
from jax.experimental import pallas as pl
from jax.experimental.pallas import tpu as pltpu

D_MODEL = 1024
BATCH = 2
SEQ = 8192
DEPTH = 4
TOKENS = BATCH * SEQ

GLA_HEADS = 4
GLA_DK = 64
GLA_DV = 128
GLA_QK = GLA_HEADS * GLA_DK
GLA_WIDTH = GLA_HEADS * GLA_DV
GLA_LOWRANK = 16
GLA_TAU = 16.0
GLA_CHUNK = 64

SSD_HEADS = 8
SSD_HEADDIM = 64
SSD_WIDTH = SSD_HEADS * SSD_HEADDIM
SSD_GROUPS = 2
SSD_STATE = 64
SSD_BC = SSD_GROUPS * SSD_STATE
SSD_CONV = 4
SSD_CONV_DIM = SSD_WIDTH + 2 * SSD_BC
SSD_CHUNK = 128

D_FF = 3584
N_EXPERTS = 8
EPS = 1e-6

LANES = 128
GLA_COLS = 2 * GLA_QK + 2 * GLA_WIDTH + LANES
SSD_COLS = 2 * SSD_WIDTH + 2 * SSD_BC + LANES
G_Q, G_K, G_V, G_G, G_A = 0, GLA_QK, 2 * GLA_QK, 2 * GLA_QK + GLA_WIDTH, 2 * GLA_QK + 2 * GLA_WIDTH
S_Z, S_X, S_DT = 0, SSD_WIDTH, SSD_WIDTH + SSD_CONV_DIM

TM_PROJ = 512
TB_SCAN = 512
TM_FFN = 1024
TF_FFN = 512
N_FF_CHUNKS = D_FF // TF_FFN
VMEM_LIMIT = 56 * 1024 * 1024

TOP_K = 2
SEG_ALIGN = 16
SEG_PIECES = tuple(TM_PROJ >> s for s in range(6))
MOE_TILE = 1024
MOE_SUB = 512
N_TILES = TOKENS // TM_PROJ
MOE_LOC_ROWS = -(-(TOP_K * TM_PROJ + N_EXPERTS * (SEG_ALIGN - 1)) // LANES) * LANES
MOE_MAX_TILES = (TOP_K * TOKENS + N_TILES * N_EXPERTS * (SEG_ALIGN - 1) + N_EXPERTS * (MOE_TILE - 1)) // MOE_TILE

F32 = jnp.float32
BF16 = jnp.bfloat16


def _dot(a, b):
    return jnp.dot(a, b, preferred_element_type=F32)


def _dot_nt(a, b):
    return lax.dot_general(a, b, (((1,), (1,)), ((), ())), preferred_element_type=F32)


def _dot_tn(a, b):
    return lax.dot_general(a, b, (((0,), (0,)), ((), ())), preferred_element_type=F32)


def _split(a):
    hi = a.astype(BF16)
    lo = (a - hi.astype(F32)).astype(BF16)
    return hi, lo


def _dot3(a, b):
    a_hi, a_lo = _split(a)
    b_hi, b_lo = _split(b)
    return _dot(a_hi, b_hi) + _dot(a_lo, b_hi) + _dot(a_hi, b_lo)


def _dot_exact_rhs(a, b_bf16):
    a_hi, a_lo = _split(a)
    return _dot(a_hi, b_bf16) + _dot(a_lo, b_bf16)


def _dot_exact_lhs(a_bf16, b):
    b_hi, b_lo = _split(b)
    return _dot(a_bf16, b_hi) + _dot(a_bf16, b_lo)


def _silu(x):
    return x * jax.nn.sigmoid(x)


def _softplus(x):
    return jnp.maximum(x, 0.0) + jnp.log1p(jnp.exp(-jnp.abs(x)))


def _norm_mod(x, w, scale, shift):
    ms = jnp.mean(x * x, axis=-1, keepdims=True)
    return (x * lax.rsqrt(ms + EPS) * w) * (1.0 + scale) + shift


def _params(*sem):
    return pltpu.CompilerParams(dimension_semantics=sem, vmem_limit_bytes=VMEM_LIMIT)


def _adaln_kernel(c_ref, w_ref, b_ref, o_ref):
    s = _silu(c_ref[...])
    o_ref[0] = _dot3(s, w_ref[0]) + b_ref[0]


def _adaln(c_pad, ada_w, ada_b):
    n_col = 6 * D_MODEL // D_MODEL
    return pl.pallas_call(
        _adaln_kernel,
        out_shape=jax.ShapeDtypeStruct((DEPTH, 8, 6 * D_MODEL), F32),
        grid=(DEPTH, n_col),
        in_specs=[
            pl.BlockSpec((8, D_MODEL), lambda l, j: (0, 0)),
            pl.BlockSpec((1, D_MODEL, D_MODEL), lambda l, j: (l, 0, j)),
            pl.BlockSpec((1, 1, D_MODEL), lambda l, j: (l, 0, j)),
        ],
        out_specs=pl.BlockSpec((1, 8, D_MODEL), lambda l, j: (l, 0, j)),
        compiler_params=_params("arbitrary", "arbitrary"),
        name="adaln",
    )(c_pad, ada_w, ada_b.reshape(DEPTH, 1, 6 * D_MODEL))


def _inproj_kernel(x_ref, mod_ref, nw_ref, w_ref, og_ref, os_ref):
    mod = mod_ref[0, 0]
    h = _norm_mod(x_ref[...], nw_ref[...], mod[1:2], mod[0:1]).astype(BF16)
    for c0 in range(0, GLA_COLS, 512):
        c1 = min(c0 + 512, GLA_COLS)
        og_ref[:, c0:c1] = _dot(h, w_ref[:, c0:c1])
    for c0 in range(0, SSD_COLS, 512):
        c1 = min(c0 + 512, SSD_COLS)
        os_ref[:, c0:c1] = _dot(h, w_ref[:, GLA_COLS + c0:GLA_COLS + c1])


def _inproj(x, mod_l, norm_w, w_in_p):
    tiles_per_batch = SEQ // TM_PROJ
    return pl.pallas_call(
        _inproj_kernel,
        out_shape=(jax.ShapeDtypeStruct((TOKENS, GLA_COLS), F32),
                   jax.ShapeDtypeStruct((TOKENS, SSD_COLS), F32)),
        grid=(TOKENS // TM_PROJ,),
        in_specs=[
            pl.BlockSpec((TM_PROJ, D_MODEL), lambda i: (i, 0)),
            pl.BlockSpec((1, 1, 6, D_MODEL), lambda i: (i // tiles_per_batch, 0, 0, 0)),
            pl.BlockSpec((1, D_MODEL), lambda i: (0, 0)),
            pl.BlockSpec((D_MODEL, GLA_COLS + SSD_COLS), lambda i: (0, 0)),
        ],
        out_specs=(pl.BlockSpec((TM_PROJ, GLA_COLS), lambda i: (i, 0)),
                   pl.BlockSpec((TM_PROJ, SSD_COLS), lambda i: (i, 0))),
        compiler_params=_params("arbitrary"),
        name="inproj",
    )(x, mod_l, norm_w, w_in_p)


def _gla_kernel(pg_ref, aw_ref, ab_ref, nw_ref, o_ref, la_scr, st_scr):
    @pl.when(pl.program_id(1) == 0)
    def _():
        st_scr[...] = jnp.zeros_like(st_scr)

    pre = _dot3(pg_ref[:, G_A:G_A + LANES], aw_ref[...]) + ab_ref[...]
    la_scr[...] = -_softplus(-pre) * (1.0 / GLA_TAU)

    c = GLA_CHUNK
    row = lax.broadcasted_iota(jnp.int32, (c, c), 0)
    col = lax.broadcasted_iota(jnp.int32, (c, c), 1)
    causal = row >= col
    tril = jnp.where(causal, 1.0, 0.0).astype(BF16)
    low_q = lax.broadcasted_iota(jnp.int32, (c, LANES), 1) < GLA_DK
    low_s = lax.broadcasted_iota(jnp.int32, (GLA_DV, LANES), 1) < GLA_DK
    nw = nw_ref[...]

    def chunk(ci, carry):
        r0 = pl.multiple_of(ci * c, c)
        rows = pl.ds(r0, c)
        b_all = _dot_exact_lhs(tril, la_scr[rows, :])
        for p in range(GLA_HEADS // 2):
            b = b_all[:, p * LANES:(p + 1) * LANES]
            b_last = b[c - 1:c, :]
            q = pg_ref[rows, G_Q + p * LANES:G_Q + (p + 1) * LANES] * (GLA_DK ** -0.5)
            k = pg_ref[rows, G_K + p * LANES:G_K + (p + 1) * LANES]
            qe = q * jnp.exp(b)
            ke = (k * jnp.exp(-b)).astype(BF16)
            k_tail = (k * jnp.exp(b_last - b)).astype(BF16)
            st = st_scr[p]
            st_bf = st.astype(BF16)
            contrib = []
            for j in range(2):
                h = 2 * p + j
                qm = jnp.where(low_q if j == 0 else jnp.logical_not(low_q), qe, 0.0).astype(BF16)
                att = jnp.where(causal, _dot_nt(qm, ke), 0.0).astype(BF16)
                v = pg_ref[rows, G_V + h * GLA_DV:G_V + (h + 1) * GLA_DV].astype(BF16)
                o = _dot(att, v) + _dot_nt(qm, st_bf)
                ms = jnp.mean(o * o, axis=-1, keepdims=True)
                g = pg_ref[rows, G_G + h * GLA_DV:G_G + (h + 1) * GLA_DV]
                o = (o * lax.rsqrt(ms + EPS) * nw) * _silu(g)
                o_ref[rows, h * GLA_DV:(h + 1) * GLA_DV] = o.astype(BF16)
                contrib.append(_dot_tn(v, k_tail))
            st_scr[p] = st * jnp.exp(b_last) + jnp.where(low_s, contrib[0], contrib[1])
        return carry

    lax.fori_loop(0, TB_SCAN // c, chunk, 0)


def _gla(pg, a_w, a_b, norm_w):
    nblk = SEQ // TB_SCAN
    return pl.pallas_call(
        _gla_kernel,
        out_shape=jax.ShapeDtypeStruct((TOKENS, GLA_WIDTH), BF16),
        grid=(BATCH, nblk),
        in_specs=[
            pl.BlockSpec((TB_SCAN, GLA_COLS), lambda b, i: (b * nblk + i, 0)),
            pl.BlockSpec((LANES, GLA_QK), lambda b, i: (0, 0)),
            pl.BlockSpec((1, GLA_QK), lambda b, i: (0, 0)),
            pl.BlockSpec((1, GLA_DV), lambda b, i: (0, 0)),
        ],
        out_specs=pl.BlockSpec((TB_SCAN, GLA_WIDTH), lambda b, i: (b * nblk + i, 0)),
        scratch_shapes=[pltpu.VMEM((TB_SCAN, GLA_QK), F32),
                        pltpu.VMEM((GLA_HEADS // 2, GLA_DV, LANES), F32)],
        compiler_params=_params("arbitrary", "arbitrary"),
        name="gla",
    )(pg, a_w, a_b, norm_w)


def _ssd_kernel(ps_ref, cw_ref, cb_ref, dtb_ref, alog_ref, dsk_ref, nw_ref, e64_ref, e128_ref,
                o_ref, xbc_scr, act_scr, st_scr):
    first = pl.program_id(1) == 0
    halo = 8

    @pl.when(first)
    def _():
        xbc_scr[0:halo, :] = jnp.zeros((halo, SSD_CONV_DIM), F32)
        st_scr[...] = jnp.zeros_like(st_scr)

    @pl.when(jnp.logical_not(first))
    def _():
        xbc_scr[0:halo, :] = xbc_scr[TB_SCAN:TB_SCAN + halo, :]

    xbc_scr[halo:halo + TB_SCAN, :] = ps_ref[:, S_X:S_X + SSD_CONV_DIM]
    conv = cb_ref[...] + cw_ref[0:1, :] * xbc_scr[halo - 3:halo - 3 + TB_SCAN, :]
    for kk in range(1, SSD_CONV):
        conv = conv + cw_ref[kk:kk + 1, :] * xbc_scr[halo - 3 + kk:halo - 3 + kk + TB_SCAN, :]
    act_scr[...] = _silu(conv)

    c = SSD_CHUNK
    row = lax.broadcasted_iota(jnp.int32, (c, c), 0)
    col = lax.broadcasted_iota(jnp.int32, (c, c), 1)
    causal = row >= col
    tril = jnp.where(causal, 1.0, 0.0).astype(BF16)
    head_lane = col < SSD_HEADS
    low_half = col < SSD_HEADDIM
    st_row = lax.broadcasted_iota(jnp.int32, (c, SSD_WIDTH), 0)
    st_col = lax.broadcasted_iota(jnp.int32, (c, SSD_WIDTH), 1)
    blockdiag = (st_row < SSD_STATE) == (st_col < SSD_WIDTH // SSD_GROUPS)
    a_neg = -jnp.exp(alog_ref[...])
    heads_per_group = SSD_HEADS // SSD_GROUPS

    def chunk(ci, carry):
        r0 = pl.multiple_of(ci * c, c)
        rows = pl.ds(r0, c)
        dt = jnp.where(head_lane, _softplus(ps_ref[rows, S_DT:S_DT + LANES] + dtb_ref[...]), 0.0)
        cum = _dot_exact_lhs(tril, dt * a_neg)
        cum_t = cum.T
        cum64 = _dot_exact_rhs(cum, e64_ref[...])
        dt64 = _dot_exact_rhs(dt, e64_ref[...])
        cum_col = _dot_exact_rhs(cum, e128_ref[...])
        cl64 = cum64[c - 1:c, :]
        xs = act_scr[rows, 0:SSD_WIDTH]
        bm = act_scr[rows, SSD_WIDTH:SSD_WIDTH + SSD_BC].astype(BF16)
        cm = act_scr[rows, SSD_WIDTH + SSD_BC:SSD_CONV_DIM]
        xdt = xs * dt64
        xdt_bf = xdt.astype(BF16)
        y_parts = []
        for g in range(SSD_GROUPS):
            cm_g = jnp.where(low_half if g == 0 else jnp.logical_not(low_half), cm, 0.0).astype(BF16)
            scores = _dot_nt(cm_g, bm)
            for pp in range(heads_per_group // 2):
                p = g * (heads_per_group // 2) + pp
                ys = []
                for j in range(2):
                    h = 2 * p + j
                    seg = cum_col[:, h * LANES:(h + 1) * LANES] - cum_t[h:h + 1, :]
                    decay = jnp.exp(jnp.where(causal, seg, -jnp.inf))
                    ys.append(_dot((scores * decay).astype(BF16), xdt_bf[:, p * LANES:(p + 1) * LANES]))
                y_parts.append(jnp.where(low_half, ys[0], ys[1]))
        y = jnp.concatenate(y_parts, axis=1)
        st = st_scr[...]
        y = y + _dot(cm.astype(BF16), st.astype(BF16)) * jnp.exp(cum64)
        contrib = _dot_tn(bm, (xdt * jnp.exp(cl64 - cum64)).astype(BF16))
        st_scr[...] = st * jnp.exp(cl64) + jnp.where(blockdiag, contrib, 0.0)
        y = y + dsk_ref[...] * xs
        y = y * _silu(ps_ref[rows, S_Z:S_Z + SSD_WIDTH])
        ms = jnp.mean(y * y, axis=-1, keepdims=True)
        o_ref[rows, :] = (y * lax.rsqrt(ms + EPS) * nw_ref[...]).astype(BF16)
        return carry

    lax.fori_loop(0, TB_SCAN // c, chunk, 0)


def _ssd(ps, conv_w, conv_b, dt_bias, a_log, d_skip, norm_w, e64, e128):
    nblk = SEQ // TB_SCAN
    const = lambda b, i: (0, 0)
    return pl.pallas_call(
        _ssd_kernel,
        out_shape=jax.ShapeDtypeStruct((TOKENS, SSD_WIDTH), BF16),
        grid=(BATCH, nblk),
        in_specs=[
            pl.BlockSpec((TB_SCAN, SSD_COLS), lambda b, i: (b * nblk + i, 0)),
            pl.BlockSpec((SSD_CONV, SSD_CONV_DIM), const),
            pl.BlockSpec((1, SSD_CONV_DIM), const),
            pl.BlockSpec((1, LANES), const),
            pl.BlockSpec((1, LANES), const),
            pl.BlockSpec((1, SSD_WIDTH), const),
            pl.BlockSpec((1, SSD_WIDTH), const),
            pl.BlockSpec((LANES, SSD_WIDTH), const),
            pl.BlockSpec((LANES, SSD_HEADS * LANES), const),
        ],
        out_specs=pl.BlockSpec((TB_SCAN, SSD_WIDTH), lambda b, i: (b * nblk + i, 0)),
        scratch_shapes=[pltpu.VMEM((TB_SCAN + 8, SSD_CONV_DIM), F32),
                        pltpu.VMEM((TB_SCAN, SSD_CONV_DIM), F32),
                        pltpu.VMEM((SSD_BC, SSD_WIDTH), F32)],
        compiler_params=_params("arbitrary", "arbitrary"),
        name="ssd",
    )(ps, conv_w, conv_b, dt_bias, a_log, d_skip, norm_w, e64, e128)


def _outproj_kernel(x_ref, og_ref, oy_ref, wo_ref, mod_ref, nw_ref, *rest, with_router):
    if with_router:
        rw_ref, xo_ref, h_ref, gates_ref, sel1_ref, sel2_ref, cnt_ref = rest
    else:
        xo_ref, h_ref = rest
    mod = mod_ref[0, 0]
    mix = _dot(og_ref[...], wo_ref[0:GLA_WIDTH, :]) + _dot(oy_ref[...], wo_ref[GLA_WIDTH:, :])
    x_new = x_ref[...] + mod[2:3] * mix
    xo_ref[...] = x_new
    h = _norm_mod(x_new, nw_ref[...], mod[4:5], mod[3:4])
    h_ref[...] = h.astype(BF16)
    if with_router:
        logits = _dot3(h, rw_ref[...])
        lane = lax.broadcasted_iota(jnp.int32, logits.shape, 1)
        lg = jnp.where(lane < N_EXPERTS, logits, -jnp.inf)
        v1 = jnp.max(lg, axis=-1, keepdims=True)
        i1 = jnp.min(jnp.where(lg == v1, lane, LANES), axis=-1, keepdims=True)
        lg2 = jnp.where(lane == i1, -jnp.inf, lg)
        v2 = jnp.max(lg2, axis=-1, keepdims=True)
        i2 = jnp.min(jnp.where(lg2 == v2, lane, LANES), axis=-1, keepdims=True)
        e2 = jnp.exp(v2 - v1)
        p1 = 1.0 / (1.0 + e2)
        p2 = e2 / (1.0 + e2)
        sel1 = jnp.where(lane == i1, 1.0, 0.0)
        sel2 = jnp.where(lane == i2, 1.0, 0.0)
        gates_ref[...] = sel1 * p1 + sel2 * p2
        sel1_ref[...] = sel1
        sel2_ref[...] = sel2
        counts = jnp.sum(sel1 + sel2, axis=0, keepdims=True).astype(jnp.int32)
        cnt_ref[0] = jnp.broadcast_to(counts, (8, LANES))


def _outproj(x, og, oy, w_out, mod_l, norm_w, router_w):
    tiles_per_batch = SEQ // TM_PROJ
    with_router = router_w is not None
    tile = lambda i: (i, 0)
    const = lambda i: (0, 0)
    in_specs = [
        pl.BlockSpec((TM_PROJ, D_MODEL), tile),
        pl.BlockSpec((TM_PROJ, GLA_WIDTH), tile),
        pl.BlockSpec((TM_PROJ, SSD_WIDTH), tile),
        pl.BlockSpec((D_MODEL, D_MODEL), const),
        pl.BlockSpec((1, 1, 6, D_MODEL), lambda i: (i // tiles_per_batch, 0, 0, 0)),
        pl.BlockSpec((1, D_MODEL), const),
    ]
    out_shape = [jax.ShapeDtypeStruct((TOKENS, D_MODEL), F32),
                 jax.ShapeDtypeStruct((TOKENS, D_MODEL), BF16)]
    out_specs = [pl.BlockSpec((TM_PROJ, D_MODEL), tile), pl.BlockSpec((TM_PROJ, D_MODEL), tile)]
    args = [x, og, oy, w_out, mod_l, norm_w]
    if with_router:
        in_specs.append(pl.BlockSpec((D_MODEL, LANES), const))
        for _ in range(3):
            out_shape.append(jax.ShapeDtypeStruct((TOKENS, LANES), F32))
            out_specs.append(pl.BlockSpec((TM_PROJ, LANES), tile))
        out_shape.append(jax.ShapeDtypeStruct((TOKENS // TM_PROJ, 8, LANES), jnp.int32))
        out_specs.append(pl.BlockSpec((1, 8, LANES), lambda i: (i, 0, 0)))
        args.append(router_w)
    return pl.pallas_call(
        functools.partial(_outproj_kernel, with_router=with_router),
        out_shape=tuple(out_shape),
        grid=(TOKENS // TM_PROJ,),
        in_specs=in_specs,
        out_specs=tuple(out_specs),
        compiler_params=_params("arbitrary"),
        name="outproj_router" if with_router else "outproj",
    )(*args)


def _swiglu_chunk(h, wg, wu, wd):
    a = (_silu(_dot(h, wg.astype(BF16))) * _dot(h, wu.astype(BF16))).astype(BF16)
    return _dot(a, wd.astype(BF16))


def _ffn_kernel(h_ref, x_ref, mod_ref, wg_ref, wu_ref, wd_ref, o_ref, acc_ref):
    j = pl.program_id(1)
    d = _swiglu_chunk(h_ref[...], wg_ref[0], wu_ref[0], wd_ref[0])

    @pl.when(j == 0)
    def _():
        acc_ref[...] = d

    @pl.when(j > 0)
    def _():
        acc_ref[...] += d

    @pl.when(j == N_FF_CHUNKS - 1)
    def _():
        o_ref[...] = x_ref[...] + mod_ref[0, 0, 5:6, :] * acc_ref[...]


def _ffn(h, x, mod_l, w_gate, w_up, w_down, layer_idx):
    tiles_per_batch = SEQ // TM_FFN
    tile = lambda i, j: (i, 0)
    return pl.pallas_call(
        _ffn_kernel,
        out_shape=jax.ShapeDtypeStruct((TOKENS, D_MODEL), F32),
        grid=(TOKENS // TM_FFN, N_FF_CHUNKS),
        in_specs=[pl.BlockSpec((TM_FFN, D_MODEL), tile),
                  pl.BlockSpec((TM_FFN, D_MODEL), tile),
                  pl.BlockSpec((1, 1, 6, D_MODEL), lambda i, j: (i // tiles_per_batch, 0, 0, 0)),
                  pl.BlockSpec((1, D_MODEL, TF_FFN), lambda i, j: (layer_idx, 0, j)),
                  pl.BlockSpec((1, D_MODEL, TF_FFN), lambda i, j: (layer_idx, 0, j)),
                  pl.BlockSpec((1, TF_FFN, D_MODEL), lambda i, j: (layer_idx, j, 0))],
        out_specs=pl.BlockSpec((TM_FFN, D_MODEL), tile),
        scratch_shapes=[pltpu.VMEM((TM_FFN, D_MODEL), F32)],
        compiler_params=_params("arbitrary", "arbitrary"),
        name="dense_ffn",
    )(h, x, mod_l, w_gate, w_up, w_down)


def _moe_plan(cnt):
    seg_len = (cnt + SEG_ALIGN - 1) // SEG_ALIGN * SEG_ALIGN
    loc_off = jnp.cumsum(seg_len, axis=1) - seg_len
    n_rows = seg_len.sum(axis=0)
    region = (n_rows + MOE_TILE - 1) // MOE_TILE * MOE_TILE
    base = jnp.cumsum(region) - region
    seg_start = base[None, :] + jnp.cumsum(seg_len, axis=0) - seg_len
    tiles_e = region // MOE_TILE
    tile_end = jnp.cumsum(tiles_e)
    n_act = tile_end[-1]
    r = jnp.arange(MOE_MAX_TILES, dtype=jnp.int32)
    r_act = jnp.minimum(r, n_act - 1)
    tile_exp = jnp.sum(r_act[:, None] >= tile_end[None, :], axis=1).astype(jnp.int32)
    rows_left = n_rows[tile_exp] - (r_act - (tile_end - tiles_e)[tile_exp]) * MOE_TILE
    n_sub = jnp.clip((rows_left + MOE_SUB - 1) // MOE_SUB, 0, MOE_TILE // MOE_SUB)
    n_sub = jnp.where(r < n_act, n_sub, 0)
    i32 = lambda a: a.reshape(-1).astype(jnp.int32)
    return dict(seg=(i32(seg_start), i32(loc_off), i32(seg_len)), tiles=(tile_exp, i32(r_act), i32(n_sub)))


def _segment_dma(src, dst, src_off, dst_off, length, sem, wait):
    off = jnp.int32(0)
    for size in SEG_PIECES:
        take = (length & size) != 0
        s0 = pl.multiple_of(src_off + off, SEG_ALIGN)
        d0 = pl.multiple_of(dst_off + off, SEG_ALIGN)

        @pl.when(take)
        def _():
            cp = pltpu.make_async_copy(src.at[pl.ds(s0, size)], dst.at[pl.ds(d0, size)], sem)
            if wait:
                cp.wait()
            else:
                cp.start()

        off = off + jnp.where(take, size, 0)


def _moe_sort_kernel(ss_ref, lo_ref, ln_ref, h_ref, sel1_ref, sel2_ref, xs_in_ref, xs_ref, loc_scr, sem):
    del xs_in_ref
    i = pl.program_id(0)
    tm = TM_PROJ
    member = sel1_ref[...] + sel2_ref[...]
    tr = lax.broadcasted_iota(jnp.int32, (tm, tm), 0)
    tc = lax.broadcasted_iota(jnp.int32, (tm, tm), 1)
    before = jnp.where(tr < tc, 1.0, 0.0).astype(BF16)
    rank_t = _dot_tn(member.astype(BF16), before)[0:8]
    mem_t = member.T[0:8]
    sub = lax.broadcasted_iota(jnp.int32, (8, tm), 0)
    loc = jnp.zeros((8, tm), F32)
    for e in range(N_EXPERTS):
        loc = jnp.where(sub == e, lo_ref[i * N_EXPERTS + e].astype(F32), loc)
    pos_t = jnp.where(mem_t > 0.0, rank_t + loc, -1.0)
    h = h_ref[...]
    blk = MOE_LOC_ROWS // 3
    for rb in range(3):
        rid = (lax.broadcasted_iota(jnp.int32, (blk, tm), 0) + rb * blk).astype(F32)
        perm = jnp.zeros((blk, tm), F32)
        for e in range(N_EXPERTS):
            perm = jnp.where(rid == pos_t[e:e + 1, :], 1.0, perm)
        loc_scr[rb * blk:(rb + 1) * blk, :] = _dot(perm.astype(BF16), h).astype(BF16)
    for wait in (False, True):
        for e in range(N_EXPERTS):
            k = i * N_EXPERTS + e
            _segment_dma(loc_scr, xs_ref, lo_ref[k], ss_ref[k], ln_ref[k], sem, wait)


def _moe_sort(plan, h, sel1, sel2):
    tile = lambda i, *_: (i, 0)
    xs_init = jnp.zeros((MOE_MAX_TILES * MOE_TILE, D_MODEL), BF16)
    return pl.pallas_call(
        _moe_sort_kernel,
        out_shape=jax.ShapeDtypeStruct(xs_init.shape, BF16),
        grid_spec=pltpu.PrefetchScalarGridSpec(
            num_scalar_prefetch=3,
            grid=(TOKENS // TM_PROJ,),
            in_specs=[pl.BlockSpec((TM_PROJ, D_MODEL), tile),
                      pl.BlockSpec((TM_PROJ, LANES), tile),
                      pl.BlockSpec((TM_PROJ, LANES), tile),
                      pl.BlockSpec(memory_space=pl.ANY)],
            out_specs=pl.BlockSpec(memory_space=pl.ANY),
            scratch_shapes=[pltpu.VMEM((MOE_LOC_ROWS, D_MODEL), BF16), pltpu.SemaphoreType.DMA],
        ),
        input_output_aliases={6: 0},
        compiler_params=_params("arbitrary"),
        name="moe_sort",
    )(*plan["seg"], h, sel1, sel2, xs_init)


def _moe_ffn_kernel(te_ref, ra_ref, ns_ref, xs_ref, wg_ref, wu_ref, wd_ref, o_ref, acc_ref):
    del te_ref, ra_ref
    r, j = pl.program_id(0), pl.program_id(1)
    n_sub = ns_ref[r]
    for s in range(MOE_TILE // MOE_SUB):
        rows = slice(s * MOE_SUB, (s + 1) * MOE_SUB)

        @pl.when(s < n_sub)
        def _():
            d = _swiglu_chunk(xs_ref[rows, :], wg_ref[0, 0], wu_ref[0, 0], wd_ref[0, 0])

            @pl.when(j == 0)
            def _():
                acc_ref[rows, :] = d

            @pl.when(jnp.logical_and(j > 0, j < N_FF_CHUNKS - 1))
            def _():
                acc_ref[rows, :] += d

            @pl.when(j == N_FF_CHUNKS - 1)
            def _():
                o_ref[rows, :] = (acc_ref[rows, :] + d).astype(BF16)

        @pl.when(jnp.logical_and(s >= n_sub, j == N_FF_CHUNKS - 1))
        def _():
            o_ref[rows, :] = jnp.zeros((MOE_SUB, D_MODEL), BF16)


def _moe_ffn(plan, xs, w_gate, w_up, w_down, layer_idx):
    last = N_FF_CHUNKS - 1
    rows = lambda r, j, te, ra, ns: (ra[r], 0)
    chunk = lambda r, j, ns: jnp.where(ns[r] > 0, j, last)
    return pl.pallas_call(
        _moe_ffn_kernel,
        out_shape=jax.ShapeDtypeStruct(xs.shape, BF16),
        grid_spec=pltpu.PrefetchScalarGridSpec(
            num_scalar_prefetch=3,
            grid=(MOE_MAX_TILES, N_FF_CHUNKS),
            in_specs=[pl.BlockSpec((MOE_TILE, D_MODEL), rows),
                      pl.BlockSpec((1, 1, D_MODEL, TF_FFN),
                                   lambda r, j, te, ra, ns: (layer_idx, te[r], 0, chunk(r, j, ns))),
                      pl.BlockSpec((1, 1, D_MODEL, TF_FFN),
                                   lambda r, j, te, ra, ns: (layer_idx, te[r], 0, chunk(r, j, ns))),
                      pl.BlockSpec((1, 1, TF_FFN, D_MODEL),
                                   lambda r, j, te, ra, ns: (layer_idx, te[r], chunk(r, j, ns), 0))],
            out_specs=pl.BlockSpec((MOE_TILE, D_MODEL), lambda r, j, *_: (r, 0)),
            scratch_shapes=[pltpu.VMEM((MOE_TILE, D_MODEL), F32)],
        ),
        compiler_params=_params("arbitrary", "arbitrary"),
        name="moe_ffn",
    )(*plan["tiles"], xs, w_gate, w_up, w_down)


def _moe_combine_kernel(ss_ref, lo_ref, ln_ref, x_ref, mod_ref, gates_ref, sel1_ref, sel2_ref, ys_ref,
                        o_ref, loc_scr, sem):
    i = pl.program_id(0)
    tm = TM_PROJ
    loc_scr[...] = jnp.zeros_like(loc_scr)
    for e in range(N_EXPERTS):
        k = i * N_EXPERTS + e
        _segment_dma(ys_ref, loc_scr, ss_ref[k], lo_ref[k], ln_ref[k], sem, False)
    sel1, sel2, gates = sel1_ref[...], sel2_ref[...], gates_ref[...]
    tr = lax.broadcasted_iota(jnp.int32, (tm, tm), 0)
    tc = lax.broadcasted_iota(jnp.int32, (tm, tm), 1)
    earlier = jnp.where(tr > tc, 1.0, 0.0).astype(BF16)
    lane = lax.broadcasted_iota(jnp.int32, (1, LANES), 1)
    loc = jnp.zeros((1, LANES), F32)
    for e in range(N_EXPERTS):
        loc = jnp.where(lane == e, lo_ref[i * N_EXPERTS + e].astype(F32), loc)
    pos = _dot(earlier, (sel1 + sel2).astype(BF16)) + loc
    pos1 = jnp.sum(sel1 * pos, axis=-1, keepdims=True)
    pos2 = jnp.sum(sel2 * pos, axis=-1, keepdims=True)
    p1 = jnp.sum(sel1 * gates, axis=-1, keepdims=True)
    p2 = jnp.sum(sel2 * gates, axis=-1, keepdims=True)
    cid = lax.broadcasted_iota(jnp.int32, (tm, MOE_LOC_ROWS), 1).astype(F32)
    pick1 = jnp.where(cid == pos1, 1.0, 0.0).astype(BF16)
    pick2 = jnp.where(cid == pos2, 1.0, 0.0).astype(BF16)
    for e in range(N_EXPERTS):
        k = i * N_EXPERTS + e
        _segment_dma(ys_ref, loc_scr, ss_ref[k], lo_ref[k], ln_ref[k], sem, True)
    y = loc_scr[...]
    ff = p1 * _dot(pick1, y) + p2 * _dot(pick2, y)
    o_ref[...] = x_ref[...] + mod_ref[0, 0, 5:6, :] * ff


def _moe_combine(plan, x, mod_l, gates, sel1, sel2, ys):
    tiles_per_batch = SEQ // TM_PROJ
    tile = lambda i, *_: (i, 0)
    return pl.pallas_call(
        _moe_combine_kernel,
        out_shape=jax.ShapeDtypeStruct((TOKENS, D_MODEL), F32),
        grid_spec=pltpu.PrefetchScalarGridSpec(
            num_scalar_prefetch=3,
            grid=(TOKENS // TM_PROJ,),
            in_specs=[pl.BlockSpec((TM_PROJ, D_MODEL), tile),
                      pl.BlockSpec((1, 1, 6, D_MODEL), lambda i, *_: (i // tiles_per_batch, 0, 0, 0)),
                      pl.BlockSpec((TM_PROJ, LANES), tile),
                      pl.BlockSpec((TM_PROJ, LANES), tile),
                      pl.BlockSpec((TM_PROJ, LANES), tile),
                      pl.BlockSpec(memory_space=pl.ANY)],
            out_specs=pl.BlockSpec((TM_PROJ, D_MODEL), tile),
            scratch_shapes=[pltpu.VMEM((MOE_LOC_ROWS, D_MODEL), BF16), pltpu.SemaphoreType.DMA],
        ),
        compiler_params=_params("arbitrary"),
        name="moe_combine",
    )(*plan["seg"], x, mod_l, gates, sel1, sel2, ys)


def _final_norm_kernel(x_ref, w_ref, o_ref):
    x = x_ref[...]
    ms = jnp.mean(x * x, axis=-1, keepdims=True)
    o_ref[...] = x * lax.rsqrt(ms + EPS) * w_ref[...]


def _final_norm(x, w):
    return pl.pallas_call(
        _final_norm_kernel,
        out_shape=jax.ShapeDtypeStruct((TOKENS, D_MODEL), F32),
        grid=(TOKENS // TM_FFN,),
        in_specs=[pl.BlockSpec((TM_FFN, D_MODEL), lambda i: (i, 0)),
                  pl.BlockSpec((1, D_MODEL), lambda i: (0, 0))],
        out_specs=pl.BlockSpec((TM_FFN, D_MODEL), lambda i: (i, 0)),
        compiler_params=_params("arbitrary"),
        name="final_norm",
    )(x, w)


def _pad_cols(a, width):
    return jnp.pad(a, [(0, 0)] * (a.ndim - 1) + [(0, width - a.shape[-1])])


def _relayout_w_in(w_in):
    o = 0
    parts = []
    for name, size in (("q", GLA_QK), ("k", GLA_QK), ("v", GLA_WIDTH), ("g", GLA_WIDTH), ("a", GLA_LOWRANK),
                       ("z", SSD_WIDTH), ("x", SSD_WIDTH), ("b", SSD_BC), ("c", SSD_BC), ("dt", SSD_HEADS)):
        piece = w_in[:, :, o:o + size]
        if size < LANES:
            piece = _pad_cols(piece, LANES)
        parts.append(piece)
        o += size
    return jnp.concatenate(parts, axis=-1).astype(BF16)


def kernel(x, c, ada_w, ada_b, norm1_w, w_in, gla_a_w, gla_a_b, gla_norm_w, conv_w, conv_b, dt_bias, a_log,
           d_skip, ssd_norm_w, w_out, norm2_w, ffn_w_gate, ffn_w_up, ffn_w_down, router_w, moe_w_gate,
           moe_w_up, moe_w_down, final_norm_w):
    xt = x.reshape(TOKENS, D_MODEL)
    c_pad = jnp.pad(c, ((0, 8 - BATCH), (0, 0)))
    mod = _adaln(c_pad, ada_w, ada_b)[:, :BATCH].reshape(DEPTH, BATCH, 6, D_MODEL)

    w_in_p = _relayout_w_in(w_in)
    w_out_bf = w_out.astype(BF16)
    a_w_p = jnp.pad(gla_a_w, ((0, 0), (0, LANES - GLA_LOWRANK), (0, 0)))
    dtb_p = _pad_cols(dt_bias, LANES)
    alog_p = _pad_cols(a_log, LANES)
    dsk_p = jnp.repeat(d_skip, SSD_HEADDIM, axis=-1)
    rw_p = _pad_cols(router_w, LANES)
    lane_head = jnp.arange(LANES)[:, None]
    e64 = (lane_head == jnp.arange(SSD_WIDTH)[None, :] // SSD_HEADDIM).astype(BF16)
    e128 = (lane_head == jnp.arange(SSD_HEADS * LANES)[None, :] // LANES).astype(BF16)

    for l in range(DEPTH):
        mod_l = mod[l].reshape(BATCH, 1, 6, D_MODEL)
        pg, ps = _inproj(xt, mod_l, norm1_w[l][None], w_in_p[l])
        og = _gla(pg, a_w_p[l], gla_a_b[l][None], gla_norm_w[l][None])
        oy = _ssd(ps, conv_w[l], conv_b[l][None], dtb_p[l][None], alog_p[l][None], dsk_p[l][None],
                  ssd_norm_w[l][None], e64, e128)
        i = l // 2
        if l % 2 == 0:
            xt, h2 = _outproj(xt, og, oy, w_out_bf[l], mod_l, norm2_w[l][None], None)
            xt = _ffn(h2, xt, mod_l, ffn_w_gate, ffn_w_up, ffn_w_down, i)
        else:
            xt, h2, gates, sel1, sel2, cnt = _outproj(xt, og, oy, w_out_bf[l], mod_l, norm2_w[l][None], rw_p[i])
            plan = _moe_plan(cnt[:, 0, :N_EXPERTS])
            xs = _moe_sort(plan, h2, sel1, sel2)
            ys = _moe_ffn(plan, xs, moe_w_gate, moe_w_up, moe_w_down, i)
            xt = _moe_combine(plan, xt, mod_l, gates, sel1, sel2, ys)
    return _final_norm(xt, final_norm_w[None]).reshape(BATCH, SEQ, D_MODEL)
```

```python
import functools

import jax
import jax.numpy as jnp
from jax import lax
from jax.experimental import pallas as pl
from jax.experimental.pallas import tpu as pltpu

D_MODEL = 1024
BATCH = 2
SEQ = 8192
DEPTH = 4
TOKENS = BATCH * SEQ

GLA_HEADS = 4
GLA_DK = 64
GLA_DV = 128
GLA_QK = GLA_HEADS * GLA_DK
GLA_WIDTH = GLA_HEADS * GLA_DV
GLA_LOWRANK = 16
GLA_TAU = 16.0
GLA_CHUNK = 64

SSD_HEADS = 8
SSD_HEADDIM = 64
SSD_WIDTH = SSD_HEADS * SSD_HEADDIM
SSD_GROUPS = 2
SSD_STATE = 64
SSD_BC = SSD_GROUPS * SSD_STATE
SSD_CONV = 4
SSD_CONV_DIM = SSD_WIDTH + 2 * SSD_BC
SSD_CHUNK = 128

D_FF = 3584
N_EXPERTS = 8
EPS = 1e-6

LANES = 128
GLA_COLS = 2 * GLA_QK + 2 * GLA_WIDTH + LANES
SSD_COLS = 2 * SSD_WIDTH + 2 * SSD_BC + LANES
G_Q, G_K, G_V, G_G, G_A = 0, GLA_QK, 2 * GLA_QK, 2 * GLA_QK + GLA_WIDTH, 2 * GLA_QK + 2 * GLA_WIDTH
S_Z, S_X, S_DT = 0, SSD_WIDTH, SSD_WIDTH + SSD_CONV_DIM

TM_PROJ = 512
TB_SCAN = 512
TM_FFN = 1024
TF_FFN = 512
GLA_UNROLL = 4
SSD_UNROLL = 4
N_FF_CHUNKS = D_FF // TF_FFN
VMEM_LIMIT = 56 * 1024 * 1024

TOP_K = 2
SEG_ALIGN = 16
SEG_PIECES = tuple(TM_PROJ >> s for s in range(6))
MOE_TILE = 1024
MOE_SUB = 512
N_TILES = TOKENS // TM_PROJ
MOE_LOC_ROWS = -(-(TOP_K * TM_PROJ + N_EXPERTS * (SEG_ALIGN - 1)) // LANES) * LANES
MOE_MAX_TILES = (TOP_K * TOKENS + N_TILES * N_EXPERTS * (SEG_ALIGN - 1) + N_EXPERTS * (MOE_TILE - 1)) // MOE_TILE

F32 = jnp.float32
BF16 = jnp.bfloat16


def _dot(a, b):
    return jnp.dot(a, b, preferred_element_type=F32)


def _dot_nt(a, b):
    return lax.dot_general(a, b, (((1,), (1,)), ((), ())), preferred_element_type=F32)


def _dot_tn(a, b):
    return lax.dot_general(a, b, (((0,), (0,)), ((), ())), preferred_element_type=F32)


def _split(a):
    hi = a.astype(BF16)
    lo = (a - hi.astype(F32)).astype(BF16)
    return hi, lo


def _dot3(a, b):
    a_hi, a_lo = _split(a)
    b_hi, b_lo = _split(b)
    return _dot(a_hi, b_hi) + _dot(a_lo, b_hi) + _dot(a_hi, b_lo)


def _dot_exact_rhs(a, b_bf16):
    a_hi, a_lo = _split(a)
    return _dot(a_hi, b_bf16) + _dot(a_lo, b_bf16)


def _dot_exact_lhs(a_bf16, b):
    b_hi, b_lo = _split(b)
    return _dot(a_bf16, b_hi) + _dot(a_bf16, b_lo)


def _silu(x):
    return x * jax.nn.sigmoid(x)


def _softplus(x):
    return jnp.maximum(x, 0.0) + jnp.log1p(jnp.exp(-jnp.abs(x)))


def _norm_mod(x, w, scale, shift):
    ms = jnp.mean(x * x, axis=-1, keepdims=True)
    return (x * lax.rsqrt(ms + EPS) * w) * (1.0 + scale) + shift


def _params(*sem):
    return pltpu.CompilerParams(dimension_semantics=sem, vmem_limit_bytes=VMEM_LIMIT)


def _adaln_kernel(c_ref, w_ref, b_ref, o_ref):
    s = _silu(c_ref[...])
    o_ref[0] = _dot3(s, w_ref[0]) + b_ref[0]


def _adaln(c_pad, ada_w, ada_b):
    n_col = 6 * D_MODEL // D_MODEL
    return pl.pallas_call(
        _adaln_kernel,
        out_shape=jax.ShapeDtypeStruct((DEPTH, 8, 6 * D_MODEL), F32),
        grid=(DEPTH, n_col),
        in_specs=[
            pl.BlockSpec((8, D_MODEL), lambda l, j: (0, 0)),
            pl.BlockSpec((1, D_MODEL, D_MODEL), lambda l, j: (l, 0, j)),
            pl.BlockSpec((1, 1, D_MODEL), lambda l, j: (l, 0, j)),
        ],
        out_specs=pl.BlockSpec((1, 8, D_MODEL), lambda l, j: (l, 0, j)),
        compiler_params=_params("arbitrary", "arbitrary"),
        name="adaln",
    )(c_pad, ada_w, ada_b.reshape(DEPTH, 1, 6 * D_MODEL))


def _inproj_kernel(x_ref, mod_ref, nw_ref, wg_ref, ws_ref, og_ref, os_ref):
    mod = mod_ref[0, 0]
    h = _norm_mod(x_ref[...], nw_ref[...], mod[1:2], mod[0:1]).astype(BF16)
    for w_ref, o_ref in ((wg_ref, og_ref), (ws_ref, os_ref)):
        n_cols = o_ref.shape[1]
        for c0 in range(0, n_cols, 512):
            c1 = min(c0 + 512, n_cols)
            o_ref[:, c0:c1] = _dot(h, w_ref[0, :, c0:c1])


def _inproj(x, mod_l, norm_w, w_gla, w_ssd, layer):
    tiles_per_batch = SEQ // TM_PROJ
    return pl.pallas_call(
        _inproj_kernel,
        out_shape=(jax.ShapeDtypeStruct((TOKENS, GLA_COLS), F32),
                   jax.ShapeDtypeStruct((TOKENS, SSD_COLS), F32)),
        grid=(TOKENS // TM_PROJ,),
        in_specs=[
            pl.BlockSpec((TM_PROJ, D_MODEL), lambda i: (i, 0)),
            pl.BlockSpec((1, 1, 6, D_MODEL), lambda i: (i // tiles_per_batch, 0, 0, 0)),
            pl.BlockSpec((1, D_MODEL), lambda i: (0, 0)),
            pl.BlockSpec((1, D_MODEL, GLA_COLS), lambda i: (layer, 0, 0)),
            pl.BlockSpec((1, D_MODEL, SSD_COLS), lambda i: (layer, 0, 0)),
        ],
        out_specs=(pl.BlockSpec((TM_PROJ, GLA_COLS), lambda i: (i, 0)),
                   pl.BlockSpec((TM_PROJ, SSD_COLS), lambda i: (i, 0))),
        compiler_params=_params("arbitrary"),
        name="inproj",
    )(x, mod_l, norm_w, w_gla, w_ssd)


def _gla_kernel(pg_ref, aw_ref, ab_ref, nw_ref, o_ref, la_scr, st_scr):
    @pl.when(pl.program_id(1) == 0)
    def _():
        st_scr[...] = jnp.zeros_like(st_scr)

    pre = _dot3(pg_ref[:, G_A:G_A + LANES], aw_ref[...]) + ab_ref[...]
    la_scr[...] = -_softplus(-pre) * (1.0 / GLA_TAU)

    c = GLA_CHUNK
    row = lax.broadcasted_iota(jnp.int32, (c, c), 0)
    col = lax.broadcasted_iota(jnp.int32, (c, c), 1)
    causal = row >= col
    tril = jnp.where(causal, 1.0, 0.0).astype(BF16)
    low_q = lax.broadcasted_iota(jnp.int32, (c, LANES), 1) < GLA_DK
    low_s = lax.broadcasted_iota(jnp.int32, (GLA_DV, LANES), 1) < GLA_DK
    nw = nw_ref[...]

    def chunk(ci, carry):
        r0 = pl.multiple_of(ci * c, c)
        rows = pl.ds(r0, c)
        b_all = _dot_exact_lhs(tril, la_scr[rows, :])
        for p in range(GLA_HEADS // 2):
            b = b_all[:, p * LANES:(p + 1) * LANES]
            b_last = b[c - 1:c, :]
            q = pg_ref[rows, G_Q + p * LANES:G_Q + (p + 1) * LANES] * (GLA_DK ** -0.5)
            k = pg_ref[rows, G_K + p * LANES:G_K + (p + 1) * LANES]
            qe = q * jnp.exp(b)
            ke = (k * jnp.exp(-b)).astype(BF16)
            k_tail = (k * jnp.exp(b_last - b)).astype(BF16)
            st = st_scr[p]
            st_bf = st.astype(BF16)
            contrib = []
            for j in range(2):
                h = 2 * p + j
                qm = jnp.where(low_q if j == 0 else jnp.logical_not(low_q), qe, 0.0).astype(BF16)
                att = jnp.where(causal, _dot_nt(qm, ke), 0.0).astype(BF16)
                v = pg_ref[rows, G_V + h * GLA_DV:G_V + (h + 1) * GLA_DV].astype(BF16)
                o = _dot(att, v) + _dot_nt(qm, st_bf)
                ms = jnp.mean(o * o, axis=-1, keepdims=True)
                g = pg_ref[rows, G_G + h * GLA_DV:G_G + (h + 1) * GLA_DV]
                o = (o * lax.rsqrt(ms + EPS) * nw) * _silu(g)
                o_ref[rows, h * GLA_DV:(h + 1) * GLA_DV] = o.astype(BF16)
                contrib.append(_dot_tn(v, k_tail))
            st_scr[p] = st * jnp.exp(b_last) + jnp.where(low_s, contrib[0], contrib[1])
        return carry

    lax.fori_loop(0, TB_SCAN // c, chunk, 0, unroll=GLA_UNROLL)


def _gla(pg, a_w, a_b, norm_w):
    nblk = SEQ // TB_SCAN
    return pl.pallas_call(
        _gla_kernel,
        out_shape=jax.ShapeDtypeStruct((TOKENS, GLA_WIDTH), BF16),
        grid=(BATCH, nblk),
        in_specs=[
            pl.BlockSpec((TB_SCAN, GLA_COLS), lambda b, i: (b * nblk + i, 0)),
            pl.BlockSpec((LANES, GLA_QK), lambda b, i: (0, 0)),
            pl.BlockSpec((1, GLA_QK), lambda b, i: (0, 0)),
            pl.BlockSpec((1, GLA_DV), lambda b, i: (0, 0)),
        ],
        out_specs=pl.BlockSpec((TB_SCAN, GLA_WIDTH), lambda b, i: (b * nblk + i, 0)),
        scratch_shapes=[pltpu.VMEM((TB_SCAN, GLA_QK), F32),
                        pltpu.VMEM((GLA_HEADS // 2, GLA_DV, LANES), F32)],
        compiler_params=_params("arbitrary", "arbitrary"),
        name="gla",
    )(pg, a_w, a_b, norm_w)


def _ssd_kernel(ps_ref, cw_ref, cb_ref, dtb_ref, alog_ref, dsk_ref, nw_ref, e64_ref, e128_ref,
                o_ref, xbc_scr, act_scr, st_scr):
    first = pl.program_id(1) == 0
    halo = 8

    @pl.when(first)
    def _():
        xbc_scr[0:halo, :] = jnp.zeros((halo, SSD_CONV_DIM), F32)
        st_scr[...] = jnp.zeros_like(st_scr)

    @pl.when(jnp.logical_not(first))
    def _():
        xbc_scr[0:halo, :] = xbc_scr[TB_SCAN:TB_SCAN + halo, :]

    xbc_scr[halo:halo + TB_SCAN, :] = ps_ref[:, S_X:S_X + SSD_CONV_DIM]
    conv = cb_ref[...] + cw_ref[0:1, :] * xbc_scr[halo - 3:halo - 3 + TB_SCAN, :]
    for kk in range(1, SSD_CONV):
        conv = conv + cw_ref[kk:kk + 1, :] * xbc_scr[halo - 3 + kk:halo - 3 + kk + TB_SCAN, :]
    act_scr[...] = _silu(conv)

    c = SSD_CHUNK
    row = lax.broadcasted_iota(jnp.int32, (c, c), 0)
    col = lax.broadcasted_iota(jnp.int32, (c, c), 1)
    causal = row >= col
    tril = jnp.where(causal, 1.0, 0.0).astype(BF16)
    head_lane = col < SSD_HEADS
    low_half = col < SSD_HEADDIM
    st_row = lax.broadcasted_iota(jnp.int32, (c, SSD_WIDTH), 0)
    st_col = lax.broadcasted_iota(jnp.int32, (c, SSD_WIDTH), 1)
    blockdiag = (st_row < SSD_STATE) == (st_col < SSD_WIDTH // SSD_GROUPS)
    a_neg = -jnp.exp(alog_ref[...])
    heads_per_group = SSD_HEADS // SSD_GROUPS

    def chunk(ci, carry):
        r0 = pl.multiple_of(ci * c, c)
        rows = pl.ds(r0, c)
        dt = jnp.where(head_lane, _softplus(ps_ref[rows, S_DT:S_DT + LANES] + dtb_ref[...]), 0.0)
        cum = _dot_exact_lhs(tril, dt * a_neg)
        cum_t = cum.T
        cum64 = _dot_exact_rhs(cum, e64_ref[...])
        dt64 = _dot_exact_rhs(dt, e64_ref[...])
        cum_col = _dot_exact_rhs(cum, e128_ref[...])
        cl64 = cum64[c - 1:c, :]
        xs = act_scr[rows, 0:SSD_WIDTH]
        bm = act_scr[rows, SSD_WIDTH:SSD_WIDTH + SSD_BC].astype(BF16)
        cm = act_scr[rows, SSD_WIDTH + SSD_BC:SSD_CONV_DIM]
        xdt = xs * dt64
        xdt_bf = xdt.astype(BF16)
        y_parts = []
        for g in range(SSD_GROUPS):
            cm_g = jnp.where(low_half if g == 0 else jnp.logical_not(low_half), cm, 0.0).astype(BF16)
            scores = _dot_nt(cm_g, bm)
            for pp in range(heads_per_group // 2):
                p = g * (heads_per_group // 2) + pp
                ys = []
                for j in range(2):
                    h = 2 * p + j
                    seg = cum_col[:, h * LANES:(h + 1) * LANES] - cum_t[h:h + 1, :]
                    decay = jnp.exp(jnp.where(causal, seg, -jnp.inf))
                    ys.append(_dot((scores * decay).astype(BF16), xdt_bf[:, p * LANES:(p + 1) * LANES]))
                y_parts.append(jnp.where(low_half, ys[0], ys[1]))
        y = jnp.concatenate(y_parts, axis=1)
        st = st_scr[...]
        y = y + _dot(cm.astype(BF16), st.astype(BF16)) * jnp.exp(cum64)
        contrib = _dot_tn(bm, (xdt * jnp.exp(cl64 - cum64)).astype(BF16))
        st_scr[...] = st * jnp.exp(cl64) + jnp.where(blockdiag, contrib, 0.0)
        y = y + dsk_ref[...] * xs
        y = y * _silu(ps_ref[rows, S_Z:S_Z + SSD_WIDTH])
        ms = jnp.mean(y * y, axis=-1, keepdims=True)
        o_ref[rows, :] = (y * lax.rsqrt(ms + EPS) * nw_ref[...]).astype(BF16)
        return carry

    lax.fori_loop(0, TB_SCAN // c, chunk, 0, unroll=SSD_UNROLL)


def _ssd(ps, conv_w, conv_b, dt_bias, a_log, d_skip, norm_w, e64, e128):
    nblk = SEQ // TB_SCAN
    const = lambda b, i: (0, 0)
    return pl.pallas_call(
        _ssd_kernel,
        out_shape=jax.ShapeDtypeStruct((TOKENS, SSD_WIDTH), BF16),
        grid=(BATCH, nblk),
        in_specs=[
            pl.BlockSpec((TB_SCAN, SSD_COLS), lambda b, i: (b * nblk + i, 0)),
            pl.BlockSpec((SSD_CONV, SSD_CONV_DIM), const),
            pl.BlockSpec((1, SSD_CONV_DIM), const),
            pl.BlockSpec((1, LANES), const),
            pl.BlockSpec((1, LANES), const),
            pl.BlockSpec((1, SSD_WIDTH), const),
            pl.BlockSpec((1, SSD_WIDTH), const),
            pl.BlockSpec((LANES, SSD_WIDTH), const),
            pl.BlockSpec((LANES, SSD_HEADS * LANES), const),
        ],
        out_specs=pl.BlockSpec((TB_SCAN, SSD_WIDTH), lambda b, i: (b * nblk + i, 0)),
        scratch_shapes=[pltpu.VMEM((TB_SCAN + 8, SSD_CONV_DIM), F32),
                        pltpu.VMEM((TB_SCAN, SSD_CONV_DIM), F32),
                        pltpu.VMEM((SSD_BC, SSD_WIDTH), F32)],
        compiler_params=_params("arbitrary", "arbitrary"),
        name="ssd",
    )(ps, conv_w, conv_b, dt_bias, a_log, d_skip, norm_w, e64, e128)


def _outproj_kernel(x_ref, og_ref, oy_ref, wo_ref, mod_ref, nw_ref, *rest, with_router):
    if with_router:
        rw_ref, xo_ref, h_ref, gates_ref, sel1_ref, sel2_ref, cnt_ref = rest
    else:
        xo_ref, h_ref = rest
    mod = mod_ref[0, 0]
    mix = _dot(og_ref[...], wo_ref[0, 0:GLA_WIDTH, :]) + _dot(oy_ref[...], wo_ref[0, GLA_WIDTH:, :])
    x_new = x_ref[...] + mod[2:3] * mix
    xo_ref[...] = x_new
    h = _norm_mod(x_new, nw_ref[...], mod[4:5], mod[3:4])
    h_ref[...] = h.astype(BF16)
    if with_router:
        logits = _dot3(h, rw_ref[...])
        lane = lax.broadcasted_iota(jnp.int32, logits.shape, 1)
        lg = jnp.where(lane < N_EXPERTS, logits, -jnp.inf)
        v1 = jnp.max(lg, axis=-1, keepdims=True)
        i1 = jnp.min(jnp.where(lg == v1, lane, LANES), axis=-1, keepdims=True)
        lg2 = jnp.where(lane == i1, -jnp.inf, lg)
        v2 = jnp.max(lg2, axis=-1, keepdims=True)
        i2 = jnp.min(jnp.where(lg2 == v2, lane, LANES), axis=-1, keepdims=True)
        e2 = jnp.exp(v2 - v1)
        p1 = 1.0 / (1.0 + e2)
        p2 = e2 / (1.0 + e2)
        sel1 = jnp.where(lane == i1, 1.0, 0.0)
        sel2 = jnp.where(lane == i2, 1.0, 0.0)
        gates_ref[...] = sel1 * p1 + sel2 * p2
        sel1_ref[...] = sel1
        sel2_ref[...] = sel2
        counts = jnp.sum(sel1 + sel2, axis=0, keepdims=True).astype(jnp.int32)
        cnt_ref[0] = jnp.broadcast_to(counts, (8, LANES))


def _outproj(x, og, oy, w_out, layer, mod_l, norm_w, router_w):
    tiles_per_batch = SEQ // TM_PROJ
    with_router = router_w is not None
    tile = lambda i: (i, 0)
    const = lambda i: (0, 0)
    in_specs = [
        pl.BlockSpec((TM_PROJ, D_MODEL), tile),
        pl.BlockSpec((TM_PROJ, GLA_WIDTH), tile),
        pl.BlockSpec((TM_PROJ, SSD_WIDTH), tile),
        pl.BlockSpec((1, D_MODEL, D_MODEL), lambda i: (layer, 0, 0)),
        pl.BlockSpec((1, 1, 6, D_MODEL), lambda i: (i // tiles_per_batch, 0, 0, 0)),
        pl.BlockSpec((1, D_MODEL), const),
    ]
    out_shape = [jax.ShapeDtypeStruct((TOKENS, D_MODEL), F32),
                 jax.ShapeDtypeStruct((TOKENS, D_MODEL), BF16)]
    out_specs = [pl.BlockSpec((TM_PROJ, D_MODEL), tile), pl.BlockSpec((TM_PROJ, D_MODEL), tile)]
    args = [x, og, oy, w_out, mod_l, norm_w]
    if with_router:
        in_specs.append(pl.BlockSpec((D_MODEL, LANES), const))
        for _ in range(3):
            out_shape.append(jax.ShapeDtypeStruct((TOKENS, LANES), F32))
            out_specs.append(pl.BlockSpec((TM_PROJ, LANES), tile))
        out_shape.append(jax.ShapeDtypeStruct((TOKENS // TM_PROJ, 8, LANES), jnp.int32))
        out_specs.append(pl.BlockSpec((1, 8, LANES), lambda i: (i, 0, 0)))
        args.append(router_w)
    return pl.pallas_call(
        functools.partial(_outproj_kernel, with_router=with_router),
        out_shape=tuple(out_shape),
        grid=(TOKENS // TM_PROJ,),
        in_specs=in_specs,
        out_specs=tuple(out_specs),
        compiler_params=_params("arbitrary"),
        name="outproj_router" if with_router else "outproj",
    )(*args)


def _swiglu_chunk(h, wg, wu, wd):
    a = (_silu(_dot(h, wg.astype(BF16))) * _dot(h, wu.astype(BF16))).astype(BF16)
    return _dot(a, wd.astype(BF16))


def _ffn_kernel(h_ref, x_ref, mod_ref, wg_ref, wu_ref, wd_ref, o_ref):
    @pl.when(pl.program_id(1) == 0)
    def _():
        o_ref[...] = x_ref[...]

    o_ref[...] += mod_ref[0, 0, 5:6, :] * _swiglu_chunk(h_ref[...], wg_ref[0], wu_ref[0], wd_ref[0])


def _ffn(h, x, mod_l, w_gate, w_up, w_down, layer_idx):
    tiles_per_batch = SEQ // TM_FFN
    tile = lambda i, j: (i, 0)
    return pl.pallas_call(
        _ffn_kernel,
        out_shape=jax.ShapeDtypeStruct((TOKENS, D_MODEL), F32),
        grid=(TOKENS // TM_FFN, N_FF_CHUNKS),
        in_specs=[pl.BlockSpec((TM_FFN, D_MODEL), tile),
                  pl.BlockSpec((TM_FFN, D_MODEL), tile),
                  pl.BlockSpec((1, 1, 6, D_MODEL), lambda i, j: (i // tiles_per_batch, 0, 0, 0)),
                  pl.BlockSpec((1, D_MODEL, TF_FFN), lambda i, j: (layer_idx, 0, j)),
                  pl.BlockSpec((1, D_MODEL, TF_FFN), lambda i, j: (layer_idx, 0, j)),
                  pl.BlockSpec((1, TF_FFN, D_MODEL), lambda i, j: (layer_idx, j, 0))],
        out_specs=pl.BlockSpec((TM_FFN, D_MODEL), tile),
        compiler_params=_params("arbitrary", "arbitrary"),
        name="dense_ffn",
    )(h, x, mod_l, w_gate, w_up, w_down)


def _moe_plan(cnt):
    seg_len = (cnt + SEG_ALIGN - 1) // SEG_ALIGN * SEG_ALIGN
    loc_off = jnp.cumsum(seg_len, axis=1) - seg_len
    n_rows = seg_len.sum(axis=0)
    region = (n_rows + MOE_TILE - 1) // MOE_TILE * MOE_TILE
    base = jnp.cumsum(region) - region
    seg_start = base[None, :] + jnp.cumsum(seg_len, axis=0) - seg_len
    tiles_e = region // MOE_TILE
    tile_end = jnp.cumsum(tiles_e)
    n_act = tile_end[-1]
    r = jnp.arange(MOE_MAX_TILES, dtype=jnp.int32)
    r_act = jnp.minimum(r, n_act - 1)
    tile_exp = jnp.sum(r_act[:, None] >= tile_end[None, :], axis=1).astype(jnp.int32)
    rows_left = n_rows[tile_exp] - (r_act - (tile_end - tiles_e)[tile_exp]) * MOE_TILE
    n_sub = jnp.clip((rows_left + MOE_SUB - 1) // MOE_SUB, 0, MOE_TILE // MOE_SUB)
    n_sub = jnp.where(r < n_act, n_sub, 0)
    i32 = lambda a: a.reshape(-1).astype(jnp.int32)
    return dict(seg=(i32(seg_start), i32(loc_off), i32(seg_len)), tiles=(tile_exp, i32(r_act), i32(n_sub)))


def _segment_dma(src, dst, src_off, dst_off, length, sem, wait):
    off = jnp.int32(0)
    for size in SEG_PIECES:
        take = (length & size) != 0
        s0 = pl.multiple_of(src_off + off, SEG_ALIGN)
        d0 = pl.multiple_of(dst_off + off, SEG_ALIGN)

        @pl.when(take)
        def _():
            cp = pltpu.make_async_copy(src.at[pl.ds(s0, size)], dst.at[pl.ds(d0, size)], sem)
            if wait:
                cp.wait()
            else:
                cp.start()

        off = off + jnp.where(take, size, 0)


def _moe_sort_kernel(ss_ref, lo_ref, ln_ref, h_ref, sel1_ref, sel2_ref, xs_in_ref, xs_ref, loc_scr, sem):
    del xs_in_ref
    i = pl.program_id(0)
    tm = TM_PROJ
    member = sel1_ref[...] + sel2_ref[...]
    tr = lax.broadcasted_iota(jnp.int32, (tm, tm), 0)
    tc = lax.broadcasted_iota(jnp.int32, (tm, tm), 1)
    before = jnp.where(tr < tc, 1.0, 0.0).astype(BF16)
    rank_t = _dot_tn(member.astype(BF16), before)[0:8]
    mem_t = member.T[0:8]
    sub = lax.broadcasted_iota(jnp.int32, (8, tm), 0)
    loc = jnp.zeros((8, tm), F32)
    for e in range(N_EXPERTS):
        loc = jnp.where(sub == e, lo_ref[i * N_EXPERTS + e].astype(F32), loc)
    pos_t = jnp.where(mem_t > 0.0, rank_t + loc, -1.0)
    h = h_ref[...]
    blk = MOE_LOC_ROWS // 3
    for rb in range(3):
        rid = (lax.broadcasted_iota(jnp.int32, (blk, tm), 0) + rb * blk).astype(F32)
        perm = jnp.zeros((blk, tm), F32)
        for e in range(N_EXPERTS):
            perm = jnp.where(rid == pos_t[e:e + 1, :], 1.0, perm)
        loc_scr[rb * blk:(rb + 1) * blk, :] = _dot(perm.astype(BF16), h).astype(BF16)
    for wait in (False, True):
        for e in range(N_EXPERTS):
            k = i * N_EXPERTS + e
            _segment_dma(loc_scr, xs_ref, lo_ref[k], ss_ref[k], ln_ref[k], sem, wait)


def _moe_sort(plan, h, sel1, sel2):
    tile = lambda i, *_: (i, 0)
    xs_init = jnp.zeros((MOE_MAX_TILES * MOE_TILE, D_MODEL), BF16)
    return pl.pallas_call(
        _moe_sort_kernel,
        out_shape=jax.ShapeDtypeStruct(xs_init.shape, BF16),
        grid_spec=pltpu.PrefetchScalarGridSpec(
            num_scalar_prefetch=3,
            grid=(TOKENS // TM_PROJ,),
            in_specs=[pl.BlockSpec((TM_PROJ, D_MODEL), tile),
                      pl.BlockSpec((TM_PROJ, LANES), tile),
                      pl.BlockSpec((TM_PROJ, LANES), tile),
                      pl.BlockSpec(memory_space=pl.ANY)],
            out_specs=pl.BlockSpec(memory_space=pl.ANY),
            scratch_shapes=[pltpu.VMEM((MOE_LOC_ROWS, D_MODEL), BF16), pltpu.SemaphoreType.DMA],
        ),
        input_output_aliases={6: 0},
        compiler_params=_params("arbitrary"),
        name="moe_sort",
    )(*plan["seg"], h, sel1, sel2, xs_init)


def _moe_ffn_kernel(te_ref, ra_ref, ns_ref, xs_ref, wg_ref, wu_ref, wd_ref, o_ref, acc_ref):
    del te_ref, ra_ref
    r, j = pl.program_id(0), pl.program_id(1)
    n_sub = ns_ref[r]
    for s in range(MOE_TILE // MOE_SUB):
        rows = slice(s * MOE_SUB, (s + 1) * MOE_SUB)

        @pl.when(s < n_sub)
        def _():
            @pl.when(j == 0)
            def _():
                acc_ref[rows, :] = jnp.zeros((MOE_SUB, D_MODEL), F32)

            acc_ref[rows, :] += _swiglu_chunk(xs_ref[rows, :], wg_ref[0, 0], wu_ref[0, 0], wd_ref[0, 0])

            @pl.when(j == N_FF_CHUNKS - 1)
            def _():
                o_ref[rows, :] = acc_ref[rows, :].astype(BF16)

        @pl.when(jnp.logical_and(s >= n_sub, j == N_FF_CHUNKS - 1))
        def _():
            o_ref[rows, :] = jnp.zeros((MOE_SUB, D_MODEL), BF16)


def _moe_ffn(plan, xs, w_gate, w_up, w_down, layer_idx):
    last = N_FF_CHUNKS - 1
    rows = lambda r, j, te, ra, ns: (ra[r], 0)
    chunk = lambda r, j, ns: jnp.where(ns[r] > 0, j, last)
    return pl.pallas_call(
        _moe_ffn_kernel,
        out_shape=jax.ShapeDtypeStruct(xs.shape, BF16),
        grid_spec=pltpu.PrefetchScalarGridSpec(
            num_scalar_prefetch=3,
            grid=(MOE_MAX_TILES, N_FF_CHUNKS),
            in_specs=[pl.BlockSpec((MOE_TILE, D_MODEL), rows),
                      pl.BlockSpec((1, 1, D_MODEL, TF_FFN),
                                   lambda r, j, te, ra, ns: (layer_idx, te[r], 0, chunk(r, j, ns))),
                      pl.BlockSpec((1, 1, D_MODEL, TF_FFN),
                                   lambda r, j, te, ra, ns: (layer_idx, te[r], 0, chunk(r, j, ns))),
                      pl.BlockSpec((1, 1, TF_FFN, D_MODEL),
                                   lambda r, j, te, ra, ns: (layer_idx, te[r], chunk(r, j, ns), 0))],
            out_specs=pl.BlockSpec((MOE_TILE, D_MODEL), lambda r, j, *_: (r, 0)),
            scratch_shapes=[pltpu.VMEM((MOE_TILE, D_MODEL), F32)],
        ),
        compiler_params=_params("arbitrary", "arbitrary"),
        name="moe_ffn",
    )(*plan["tiles"], xs, w_gate, w_up, w_down)


def _moe_combine_kernel(ss_ref, lo_ref, ln_ref, x_ref, mod_ref, gates_ref, sel1_ref, sel2_ref, ys_ref,
                        o_ref, loc_scr, sem):
    i = pl.program_id(0)
    tm = TM_PROJ
    loc_scr[...] = jnp.zeros_like(loc_scr)
    for e in range(N_EXPERTS):
        k = i * N_EXPERTS + e
        _segment_dma(ys_ref, loc_scr, ss_ref[k], lo_ref[k], ln_ref[k], sem, False)
    sel1, sel2, gates = sel1_ref[...], sel2_ref[...], gates_ref[...]
    tr = lax.broadcasted_iota(jnp.int32, (tm, tm), 0)
    tc = lax.broadcasted_iota(jnp.int32, (tm, tm), 1)
    earlier = jnp.where(tr > tc, 1.0, 0.0).astype(BF16)
    lane = lax.broadcasted_iota(jnp.int32, (1, LANES), 1)
    loc = jnp.zeros((1, LANES), F32)
    for e in range(N_EXPERTS):
        loc = jnp.where(lane == e, lo_ref[i * N_EXPERTS + e].astype(F32), loc)
    pos = _dot(earlier, (sel1 + sel2).astype(BF16)) + loc
    pos1 = jnp.sum(sel1 * pos, axis=-1, keepdims=True)
    pos2 = jnp.sum(sel2 * pos, axis=-1, keepdims=True)
    p1 = jnp.sum(sel1 * gates, axis=-1, keepdims=True)
    p2 = jnp.sum(sel2 * gates, axis=-1, keepdims=True)
    cid = lax.broadcasted_iota(jnp.int32, (tm, MOE_LOC_ROWS), 1).astype(F32)
    pick1 = jnp.where(cid == pos1, 1.0, 0.0).astype(BF16)
    pick2 = jnp.where(cid == pos2, 1.0, 0.0).astype(BF16)
    for e in range(N_EXPERTS):
        k = i * N_EXPERTS + e
        _segment_dma(ys_ref, loc_scr, ss_ref[k], lo_ref[k], ln_ref[k], sem, True)
    y = loc_scr[...]
    ff = p1 * _dot(pick1, y) + p2 * _dot(pick2, y)
    o_ref[...] = x_ref[...] + mod_ref[0, 0, 5:6, :] * ff


def _moe_combine(plan, x, mod_l, gates, sel1, sel2, ys):
    tiles_per_batch = SEQ // TM_PROJ
    tile = lambda i, *_: (i, 0)
    return pl.pallas_call(
        _moe_combine_kernel,
        out_shape=jax.ShapeDtypeStruct((TOKENS, D_MODEL), F32),
        grid_spec=pltpu.PrefetchScalarGridSpec(
            num_scalar_prefetch=3,
            grid=(TOKENS // TM_PROJ,),
            in_specs=[pl.BlockSpec((TM_PROJ, D_MODEL), tile),
                      pl.BlockSpec((1, 1, 6, D_MODEL), lambda i, *_: (i // tiles_per_batch, 0, 0, 0)),
                      pl.BlockSpec((TM_PROJ, LANES), tile),
                      pl.BlockSpec((TM_PROJ, LANES), tile),
                      pl.BlockSpec((TM_PROJ, LANES), tile),
                      pl.BlockSpec(memory_space=pl.ANY)],
            out_specs=pl.BlockSpec((TM_PROJ, D_MODEL), tile),
            scratch_shapes=[pltpu.VMEM((MOE_LOC_ROWS, D_MODEL), BF16), pltpu.SemaphoreType.DMA],
        ),
        compiler_params=_params("arbitrary"),
        name="moe_combine",
    )(*plan["seg"], x, mod_l, gates, sel1, sel2, ys)


def _final_norm_kernel(x_ref, w_ref, o_ref):
    x = x_ref[...]
    ms = jnp.mean(x * x, axis=-1, keepdims=True)
    o_ref[...] = x * lax.rsqrt(ms + EPS) * w_ref[...]


def _final_norm(x, w):
    return pl.pallas_call(
        _final_norm_kernel,
        out_shape=jax.ShapeDtypeStruct((TOKENS, D_MODEL), F32),
        grid=(TOKENS // TM_FFN,),
        in_specs=[pl.BlockSpec((TM_FFN, D_MODEL), lambda i: (i, 0)),
                  pl.BlockSpec((1, D_MODEL), lambda i: (0, 0))],
        out_specs=pl.BlockSpec((TM_FFN, D_MODEL), lambda i: (i, 0)),
        compiler_params=_params("arbitrary"),
        name="final_norm",
    )(x, w)


def _pad_cols(a, width):
    return jnp.pad(a, [(0, 0)] * (a.ndim - 1) + [(0, width - a.shape[-1])])


def _split_w_in(w_in):
    ssd_start = G_A + GLA_LOWRANK
    return w_in[:, :, :GLA_COLS].astype(BF16), _pad_cols(w_in[:, :, ssd_start:], SSD_COLS).astype(BF16)


def kernel(x, c, ada_w, ada_b, norm1_w, w_in, gla_a_w, gla_a_b, gla_norm_w, conv_w, conv_b, dt_bias, a_log,
           d_skip, ssd_norm_w, w_out, norm2_w, ffn_w_gate, ffn_w_up, ffn_w_down, router_w, moe_w_gate,
           moe_w_up, moe_w_down, final_norm_w):
    xt = x.reshape(TOKENS, D_MODEL)
    c_pad = jnp.pad(c, ((0, 8 - BATCH), (0, 0)))
    mod = _adaln(c_pad, ada_w, ada_b)[:, :BATCH].reshape(DEPTH, BATCH, 6, D_MODEL)

    w_gla, w_ssd = _split_w_in(w_in)
    w_out_bf = w_out.astype(BF16)
    a_w_p = jnp.pad(gla_a_w, ((0, 0), (0, LANES - GLA_LOWRANK), (0, 0)))
    dtb_p = _pad_cols(dt_bias, LANES)
    alog_p = _pad_cols(a_log, LANES)
    dsk_p = jnp.repeat(d_skip, SSD_HEADDIM, axis=-1)
    rw_p = _pad_cols(router_w, LANES)
    lane_head = jnp.arange(LANES)[:, None]
    e64 = (lane_head == jnp.arange(SSD_WIDTH)[None, :] // SSD_HEADDIM).astype(BF16)
    e128 = (lane_head == jnp.arange(SSD_HEADS * LANES)[None, :] // LANES).astype(BF16)

    for l in range(DEPTH):
        mod_l = mod[l].reshape(BATCH, 1, 6, D_MODEL)
        pg, ps = _inproj(xt, mod_l, norm1_w[l][None], w_gla, w_ssd, l)
        og = _gla(pg, a_w_p[l], gla_a_b[l][None], gla_norm_w[l][None])
        oy = _ssd(ps, conv_w[l], conv_b[l][None], dtb_p[l][None], alog_p[l][None], dsk_p[l][None],
                  ssd_norm_w[l][None], e64, e128)
        i = l // 2
        if l % 2 == 0:
            xt, h2 = _outproj(xt, og, oy, w_out_bf, l, mod_l, norm2_w[l][None], None)
            xt = _ffn(h2, xt, mod_l, ffn_w_gate, ffn_w_up, ffn_w_down, i)
        else:
            xt, h2, gates, sel1, sel2, cnt = _outproj(xt, og, oy, w_out_bf, l, mod_l, norm2_w[l][None], rw_p[i])
            plan = _moe_plan(cnt[:, 0, :N_EXPERTS])
            xs = _moe_sort(plan, h2, sel1, sel2)
            ys = _moe_ffn(plan, xs, moe_w_gate, moe_w_up, moe_w_down, i)
            xt = _moe_combine(plan, xt, mod_l, gates, sel1, sel2, ys)
    return _final_norm(xt, final_norm_w[None]).reshape(BATCH, SEQ, D_MODEL)
```

```python
import functools

import jax
import jax.numpy as jnp
from jax import lax
from jax.experimental import pallas as pl
from jax.experimental.pallas import tpu as pltpu

D_MODEL = 1024
BATCH = 2
SEQ = 8192
DEPTH = 4
TOKENS = BATCH * SEQ

GLA_HEADS = 4
GLA_DK = 64
GLA_DV = 128
GLA_QK = GLA_HEADS * GLA_DK
GLA_WIDTH = GLA_HEADS * GLA_DV
GLA_LOWRANK = 16
GLA_TAU = 16.0
GLA_CHUNK = 64

SSD_HEADS = 8
SSD_HEADDIM = 64
SSD_WIDTH = SSD_HEADS * SSD_HEADDIM
SSD_GROUPS = 2
SSD_STATE = 64
SSD_BC = SSD_GROUPS * SSD_STATE
SSD_CONV = 4
SSD_CONV_DIM = SSD_WIDTH + 2 * SSD_BC
SSD_CHUNK = 128

D_FF = 3584
N_EXPERTS = 8
EPS = 1e-6

LANES = 128
GLA_COLS = 2 * GLA_QK + 2 * GLA_WIDTH + LANES
SSD_COLS = 2 * SSD_WIDTH + 2 * SSD_BC + LANES
G_Q, G_K, G_V, G_G, G_A = 0, GLA_QK, 2 * GLA_QK, 2 * GLA_QK + GLA_WIDTH, 2 * GLA_QK + 2 * GLA_WIDTH
S_Z, S_X, S_DT = 0, SSD_WIDTH, SSD_WIDTH + SSD_CONV_DIM

TM_PROJ = 512
TB_SCAN = 512
TM_FFN = 1024
TF_FFN = 512
GLA_UNROLL = 4
SSD_UNROLL = 4
N_FF_CHUNKS = D_FF // TF_FFN
VMEM_LIMIT = 56 * 1024 * 1024

TOP_K = 2
SEG_ALIGN = 16
SEG_PIECES = tuple(TM_PROJ >> s for s in range(6))
MOE_TILE = 1024
MOE_SUB = 512
TF_MOE = 896
N_FF_MOE = D_FF // TF_MOE
N_TILES = TOKENS // TM_PROJ
MOE_LOC_ROWS = -(-(TOP_K * TM_PROJ + N_EXPERTS * (SEG_ALIGN - 1)) // LANES) * LANES
MOE_MAX_TILES = (TOP_K * TOKENS + N_TILES * N_EXPERTS * (SEG_ALIGN - 1) + N_EXPERTS * (MOE_TILE - 1)) // MOE_TILE

F32 = jnp.float32
BF16 = jnp.bfloat16


def _dot(a, b):
    return jnp.dot(a, b, preferred_element_type=F32)


def _dot_nt(a, b):
    return lax.dot_general(a, b, (((1,), (1,)), ((), ())), preferred_element_type=F32)


def _dot_tn(a, b):
    return lax.dot_general(a, b, (((0,), (0,)), ((), ())), preferred_element_type=F32)


def _split(a):
    hi = a.astype(BF16)
    lo = (a - hi.astype(F32)).astype(BF16)
    return hi, lo


def _dot3(a, b):
    a_hi, a_lo = _split(a)
    b_hi, b_lo = _split(b)
    return _dot(a_hi, b_hi) + _dot(a_lo, b_hi) + _dot(a_hi, b_lo)


def _dot_exact_rhs(a, b_bf16):
    a_hi, a_lo = _split(a)
    return _dot(a_hi, b_bf16) + _dot(a_lo, b_bf16)


def _dot_exact_lhs(a_bf16, b):
    b_hi, b_lo = _split(b)
    return _dot(a_bf16, b_hi) + _dot(a_bf16, b_lo)


def _silu(x):
    return x * jax.nn.sigmoid(x)


def _softplus(x):
    return jnp.maximum(x, 0.0) + jnp.log1p(jnp.exp(-jnp.abs(x)))


def _norm_mod(x, w, scale, shift):
    ms = jnp.mean(x * x, axis=-1, keepdims=True)
    return (x * lax.rsqrt(ms + EPS) * w) * (1.0 + scale) + shift


def _params(*sem):
    return pltpu.CompilerParams(dimension_semantics=sem, vmem_limit_bytes=VMEM_LIMIT)


def _adaln_kernel(c_ref, w_ref, b_ref, o_ref):
    s = _silu(c_ref[...])
    o_ref[0] = _dot3(s, w_ref[0]) + b_ref[0]


def _adaln(c_pad, ada_w, ada_b):
    n_col = 6 * D_MODEL // D_MODEL
    return pl.pallas_call(
        _adaln_kernel,
        out_shape=jax.ShapeDtypeStruct((DEPTH, 8, 6 * D_MODEL), F32),
        grid=(DEPTH, n_col),
        in_specs=[
            pl.BlockSpec((8, D_MODEL), lambda l, j: (0, 0)),
            pl.BlockSpec((1, D_MODEL, D_MODEL), lambda l, j: (l, 0, j)),
            pl.BlockSpec((1, 1, D_MODEL), lambda l, j: (l, 0, j)),
        ],
        out_specs=pl.BlockSpec((1, 8, D_MODEL), lambda l, j: (l, 0, j)),
        compiler_params=_params("arbitrary", "arbitrary"),
        name="adaln",
    )(c_pad, ada_w, ada_b.reshape(DEPTH, 1, 6 * D_MODEL))


def _inproj_kernel(x_ref, mod_ref, nw_ref, wg_ref, ws_ref, og_ref, os_ref):
    mod = mod_ref[0, 0]
    h = _norm_mod(x_ref[...], nw_ref[...], mod[1:2], mod[0:1]).astype(BF16)
    for w_ref, o_ref in ((wg_ref, og_ref), (ws_ref, os_ref)):
        n_cols = o_ref.shape[1]
        for c0 in range(0, n_cols, 512):
            c1 = min(c0 + 512, n_cols)
            o_ref[:, c0:c1] = _dot(h, w_ref[0, :, c0:c1])


def _inproj(x, mod_l, norm_w, w_gla, w_ssd, layer):
    tiles_per_batch = SEQ // TM_PROJ
    return pl.pallas_call(
        _inproj_kernel,
        out_shape=(jax.ShapeDtypeStruct((TOKENS, GLA_COLS), F32),
                   jax.ShapeDtypeStruct((TOKENS, SSD_COLS), F32)),
        grid=(TOKENS // TM_PROJ,),
        in_specs=[
            pl.BlockSpec((TM_PROJ, D_MODEL), lambda i: (i, 0)),
            pl.BlockSpec((1, 1, 6, D_MODEL), lambda i: (i // tiles_per_batch, 0, 0, 0)),
            pl.BlockSpec((1, D_MODEL), lambda i: (0, 0)),
            pl.BlockSpec((1, D_MODEL, GLA_COLS), lambda i: (layer, 0, 0)),
            pl.BlockSpec((1, D_MODEL, SSD_COLS), lambda i: (layer, 0, 0)),
        ],
        out_specs=(pl.BlockSpec((TM_PROJ, GLA_COLS), lambda i: (i, 0)),
                   pl.BlockSpec((TM_PROJ, SSD_COLS), lambda i: (i, 0))),
        compiler_params=_params("arbitrary"),
        name="inproj",
    )(x, mod_l, norm_w, w_gla, w_ssd)


def _gla_kernel(pg_ref, aw_ref, ab_ref, nw_ref, o_ref, la_scr, st_scr):
    @pl.when(pl.program_id(1) == 0)
    def _():
        st_scr[...] = jnp.zeros_like(st_scr)

    pre = _dot3(pg_ref[:, G_A:G_A + LANES], aw_ref[...]) + ab_ref[...]
    la_scr[...] = -_softplus(-pre) * (1.0 / GLA_TAU)

    c = GLA_CHUNK
    row = lax.broadcasted_iota(jnp.int32, (c, c), 0)
    col = lax.broadcasted_iota(jnp.int32, (c, c), 1)
    causal = row >= col
    tril = jnp.where(causal, 1.0, 0.0).astype(BF16)
    low_q = lax.broadcasted_iota(jnp.int32, (c, LANES), 1) < GLA_DK
    low_s = lax.broadcasted_iota(jnp.int32, (GLA_DV, LANES), 1) < GLA_DK
    nw = nw_ref[...]

    def chunk(ci, carry):
        r0 = pl.multiple_of(ci * c, c)
        rows = pl.ds(r0, c)
        b_all = _dot_exact_lhs(tril, la_scr[rows, :])
        for p in range(GLA_HEADS // 2):
            b = b_all[:, p * LANES:(p + 1) * LANES]
            b_last = b[c - 1:c, :]
            q = pg_ref[rows, G_Q + p * LANES:G_Q + (p + 1) * LANES] * (GLA_DK ** -0.5)
            k = pg_ref[rows, G_K + p * LANES:G_K + (p + 1) * LANES]
            qe = q * jnp.exp(b)
            ke = (k * jnp.exp(-b)).astype(BF16)
            k_tail = (k * jnp.exp(b_last - b)).astype(BF16)
            st = st_scr[p]
            st_bf = st.astype(BF16)
            contrib = []
            for j in range(2):
                h = 2 * p + j
                qm = jnp.where(low_q if j == 0 else jnp.logical_not(low_q), qe, 0.0).astype(BF16)
                att = jnp.where(causal, _dot_nt(qm, ke), 0.0).astype(BF16)
                v = pg_ref[rows, G_V + h * GLA_DV:G_V + (h + 1) * GLA_DV].astype(BF16)
                o = _dot(att, v) + _dot_nt(qm, st_bf)
                ms = jnp.mean(o * o, axis=-1, keepdims=True)
                g = pg_ref[rows, G_G + h * GLA_DV:G_G + (h + 1) * GLA_DV]
                o = (o * lax.rsqrt(ms + EPS) * nw) * _silu(g)
                o_ref[rows, h * GLA_DV:(h + 1) * GLA_DV] = o.astype(BF16)
                contrib.append(_dot_tn(v, k_tail))
            st_scr[p] = st * jnp.exp(b_last) + jnp.where(low_s, contrib[0], contrib[1])
        return carry

    lax.fori_loop(0, TB_SCAN // c, chunk, 0, unroll=GLA_UNROLL)


def _gla(pg, a_w, a_b, norm_w):
    nblk = SEQ // TB_SCAN
    return pl.pallas_call(
        _gla_kernel,
        out_shape=jax.ShapeDtypeStruct((TOKENS, GLA_WIDTH), BF16),
        grid=(BATCH, nblk),
        in_specs=[
            pl.BlockSpec((TB_SCAN, GLA_COLS), lambda b, i: (b * nblk + i, 0)),
            pl.BlockSpec((LANES, GLA_QK), lambda b, i: (0, 0)),
            pl.BlockSpec((1, GLA_QK), lambda b, i: (0, 0)),
            pl.BlockSpec((1, GLA_DV), lambda b, i: (0, 0)),
        ],
        out_specs=pl.BlockSpec((TB_SCAN, GLA_WIDTH), lambda b, i: (b * nblk + i, 0)),
        scratch_shapes=[pltpu.VMEM((TB_SCAN, GLA_QK), F32),
                        pltpu.VMEM((GLA_HEADS // 2, GLA_DV, LANES), F32)],
        compiler_params=_params("arbitrary", "arbitrary"),
        name="gla",
    )(pg, a_w, a_b, norm_w)


def _ssd_kernel(ps_ref, cw_ref, cb_ref, dtb_ref, alog_ref, dsk_ref, nw_ref, e64_ref, e128_ref,
                o_ref, xbc_scr, act_scr, st_scr):
    first = pl.program_id(1) == 0
    halo = 8

    @pl.when(first)
    def _():
        xbc_scr[0:halo, :] = jnp.zeros((halo, SSD_CONV_DIM), F32)
        st_scr[...] = jnp.zeros_like(st_scr)

    @pl.when(jnp.logical_not(first))
    def _():
        xbc_scr[0:halo, :] = xbc_scr[TB_SCAN:TB_SCAN + halo, :]

    xbc_scr[halo:halo + TB_SCAN, :] = ps_ref[:, S_X:S_X + SSD_CONV_DIM]
    conv = cb_ref[...] + cw_ref[0:1, :] * xbc_scr[halo - 3:halo - 3 + TB_SCAN, :]
    for kk in range(1, SSD_CONV):
        conv = conv + cw_ref[kk:kk + 1, :] * xbc_scr[halo - 3 + kk:halo - 3 + kk + TB_SCAN, :]
    act_scr[...] = _silu(conv)

    c = SSD_CHUNK
    row = lax.broadcasted_iota(jnp.int32, (c, c), 0)
    col = lax.broadcasted_iota(jnp.int32, (c, c), 1)
    causal = row >= col
    tril = jnp.where(causal, 1.0, 0.0).astype(BF16)
    head_lane = col < SSD_HEADS
    low_half = col < SSD_HEADDIM
    st_row = lax.broadcasted_iota(jnp.int32, (c, SSD_WIDTH), 0)
    st_col = lax.broadcasted_iota(jnp.int32, (c, SSD_WIDTH), 1)
    blockdiag = (st_row < SSD_STATE) == (st_col < SSD_WIDTH // SSD_GROUPS)
    a_neg = -jnp.exp(alog_ref[...])
    heads_per_group = SSD_HEADS // SSD_GROUPS

    def chunk(ci, carry):
        r0 = pl.multiple_of(ci * c, c)
        rows = pl.ds(r0, c)
        dt = jnp.where(head_lane, _softplus(ps_ref[rows, S_DT:S_DT + LANES] + dtb_ref[...]), 0.0)
        cum = _dot_exact_lhs(tril, dt * a_neg)
        cum_t = cum.T
        cum64 = _dot_exact_rhs(cum, e64_ref[...])
        dt64 = _dot_exact_rhs(dt, e64_ref[...])
        cum_col = _dot_exact_rhs(cum, e128_ref[...])
        cl64 = cum64[c - 1:c, :]
        xs = act_scr[rows, 0:SSD_WIDTH]
        bm = act_scr[rows, SSD_WIDTH:SSD_WIDTH + SSD_BC].astype(BF16)
        cm = act_scr[rows, SSD_WIDTH + SSD_BC:SSD_CONV_DIM]
        xdt = xs * dt64
        xdt_bf = xdt.astype(BF16)
        y_parts = []
        for g in range(SSD_GROUPS):
            cm_g = jnp.where(low_half if g == 0 else jnp.logical_not(low_half), cm, 0.0).astype(BF16)
            scores = _dot_nt(cm_g, bm)
            for pp in range(heads_per_group // 2):
                p = g * (heads_per_group // 2) + pp
                ys = []
                for j in range(2):
                    h = 2 * p + j
                    seg = cum_col[:, h * LANES:(h + 1) * LANES] - cum_t[h:h + 1, :]
                    decay = jnp.exp(jnp.where(causal, seg, -jnp.inf))
                    ys.append(_dot((scores * decay).astype(BF16), xdt_bf[:, p * LANES:(p + 1) * LANES]))
                y_parts.append(jnp.where(low_half, ys[0], ys[1]))
        y = jnp.concatenate(y_parts, axis=1)
        st = st_scr[...]
        y = y + _dot(cm.astype(BF16), st.astype(BF16)) * jnp.exp(cum64)
        contrib = _dot_tn(bm, (xdt * jnp.exp(cl64 - cum64)).astype(BF16))
        st_scr[...] = st * jnp.exp(cl64) + jnp.where(blockdiag, contrib, 0.0)
        y = y + dsk_ref[...] * xs
        y = y * _silu(ps_ref[rows, S_Z:S_Z + SSD_WIDTH])
        ms = jnp.mean(y * y, axis=-1, keepdims=True)
        o_ref[rows, :] = (y * lax.rsqrt(ms + EPS) * nw_ref[...]).astype(BF16)
        return carry

    lax.fori_loop(0, TB_SCAN // c, chunk, 0, unroll=SSD_UNROLL)


def _ssd(ps, conv_w, conv_b, dt_bias, a_log, d_skip, norm_w, e64, e128):
    nblk = SEQ // TB_SCAN
    const = lambda b, i: (0, 0)
    return pl.pallas_call(
        _ssd_kernel,
        out_shape=jax.ShapeDtypeStruct((TOKENS, SSD_WIDTH), BF16),
        grid=(BATCH, nblk),
        in_specs=[
            pl.BlockSpec((TB_SCAN, SSD_COLS), lambda b, i: (b * nblk + i, 0)),
            pl.BlockSpec((SSD_CONV, SSD_CONV_DIM), const),
            pl.BlockSpec((1, SSD_CONV_DIM), const),
            pl.BlockSpec((1, LANES), const),
            pl.BlockSpec((1, LANES), const),
            pl.BlockSpec((1, SSD_WIDTH), const),
            pl.BlockSpec((1, SSD_WIDTH), const),
            pl.BlockSpec((LANES, SSD_WIDTH), const),
            pl.BlockSpec((LANES, SSD_HEADS * LANES), const),
        ],
        out_specs=pl.BlockSpec((TB_SCAN, SSD_WIDTH), lambda b, i: (b * nblk + i, 0)),
        scratch_shapes=[pltpu.VMEM((TB_SCAN + 8, SSD_CONV_DIM), F32),
                        pltpu.VMEM((TB_SCAN, SSD_CONV_DIM), F32),
                        pltpu.VMEM((SSD_BC, SSD_WIDTH), F32)],
        compiler_params=_params("arbitrary", "arbitrary"),
        name="ssd",
    )(ps, conv_w, conv_b, dt_bias, a_log, d_skip, norm_w, e64, e128)


def _outproj_kernel(x_ref, og_ref, oy_ref, wo_ref, mod_ref, nw_ref, *rest, with_router):
    if with_router:
        rw_ref, xo_ref, h_ref, gates_ref, sel1_ref, sel2_ref, cnt_ref = rest
    else:
        xo_ref, h_ref = rest
    mod = mod_ref[0, 0]
    mix = _dot(og_ref[...], wo_ref[0, 0:GLA_WIDTH, :]) + _dot(oy_ref[...], wo_ref[0, GLA_WIDTH:, :])
    x_new = x_ref[...] + mod[2:3] * mix
    xo_ref[...] = x_new
    h = _norm_mod(x_new, nw_ref[...], mod[4:5], mod[3:4])
    h_ref[...] = h.astype(BF16)
    if with_router:
        logits = _dot3(h, rw_ref[...])
        lane = lax.broadcasted_iota(jnp.int32, logits.shape, 1)
        lg = jnp.where(lane < N_EXPERTS, logits, -jnp.inf)
        v1 = jnp.max(lg, axis=-1, keepdims=True)
        i1 = jnp.min(jnp.where(lg == v1, lane, LANES), axis=-1, keepdims=True)
        lg2 = jnp.where(lane == i1, -jnp.inf, lg)
        v2 = jnp.max(lg2, axis=-1, keepdims=True)
        i2 = jnp.min(jnp.where(lg2 == v2, lane, LANES), axis=-1, keepdims=True)
        e2 = jnp.exp(v2 - v1)
        p1 = 1.0 / (1.0 + e2)
        p2 = e2 / (1.0 + e2)
        sel1 = jnp.where(lane == i1, 1.0, 0.0)
        sel2 = jnp.where(lane == i2, 1.0, 0.0)
        gates_ref[...] = sel1 * p1 + sel2 * p2
        sel1_ref[...] = sel1
        sel2_ref[...] = sel2
        counts = jnp.sum(sel1 + sel2, axis=0, keepdims=True).astype(jnp.int32)
        cnt_ref[0] = jnp.broadcast_to(counts, (8, LANES))


def _outproj(x, og, oy, w_out, layer, mod_l, norm_w, router_w):
    tiles_per_batch = SEQ // TM_PROJ
    with_router = router_w is not None
    tile = lambda i: (i, 0)
    const = lambda i: (0, 0)
    in_specs = [
        pl.BlockSpec((TM_PROJ, D_MODEL), tile),
        pl.BlockSpec((TM_PROJ, GLA_WIDTH), tile),
        pl.BlockSpec((TM_PROJ, SSD_WIDTH), tile),
        pl.BlockSpec((1, D_MODEL, D_MODEL), lambda i: (layer, 0, 0)),
        pl.BlockSpec((1, 1, 6, D_MODEL), lambda i: (i // tiles_per_batch, 0, 0, 0)),
        pl.BlockSpec((1, D_MODEL), const),
    ]
    out_shape = [jax.ShapeDtypeStruct((TOKENS, D_MODEL), F32),
                 jax.ShapeDtypeStruct((TOKENS, D_MODEL), BF16)]
    out_specs = [pl.BlockSpec((TM_PROJ, D_MODEL), tile), pl.BlockSpec((TM_PROJ, D_MODEL), tile)]
    args = [x, og, oy, w_out, mod_l, norm_w]
    if with_router:
        in_specs.append(pl.BlockSpec((D_MODEL, LANES), const))
        for _ in range(3):
            out_shape.append(jax.ShapeDtypeStruct((TOKENS, LANES), F32))
            out_specs.append(pl.BlockSpec((TM_PROJ, LANES), tile))
        out_shape.append(jax.ShapeDtypeStruct((TOKENS // TM_PROJ, 8, LANES), jnp.int32))
        out_specs.append(pl.BlockSpec((1, 8, LANES), lambda i: (i, 0, 0)))
        args.append(router_w)
    return pl.pallas_call(
        functools.partial(_outproj_kernel, with_router=with_router),
        out_shape=tuple(out_shape),
        grid=(TOKENS // TM_PROJ,),
        in_specs=in_specs,
        out_specs=tuple(out_specs),
        compiler_params=_params("arbitrary"),
        name="outproj_router" if with_router else "outproj",
    )(*args)


def _swiglu_chunk(h, wg, wu, wd):
    a = (_silu(_dot(h, wg.astype(BF16))) * _dot(h, wu.astype(BF16))).astype(BF16)
    return _dot(a, wd.astype(BF16))


def _ffn_kernel(h_ref, x_ref, mod_ref, wg_ref, wu_ref, wd_ref, o_ref):
    @pl.when(pl.program_id(1) == 0)
    def _():
        o_ref[...] = x_ref[...]

    o_ref[...] += mod_ref[0, 0, 5:6, :] * _swiglu_chunk(h_ref[...], wg_ref[0], wu_ref[0], wd_ref[0])


def _ffn(h, x, mod_l, w_gate, w_up, w_down, layer_idx):
    tiles_per_batch = SEQ // TM_FFN
    tile = lambda i, j: (i, 0)
    return pl.pallas_call(
        _ffn_kernel,
        out_shape=jax.ShapeDtypeStruct((TOKENS, D_MODEL), F32),
        grid=(TOKENS // TM_FFN, N_FF_CHUNKS),
        in_specs=[pl.BlockSpec((TM_FFN, D_MODEL), tile),
                  pl.BlockSpec((TM_FFN, D_MODEL), tile),
                  pl.BlockSpec((1, 1, 6, D_MODEL), lambda i, j: (i // tiles_per_batch, 0, 0, 0)),
                  pl.BlockSpec((1, D_MODEL, TF_FFN), lambda i, j: (layer_idx, 0, j)),
                  pl.BlockSpec((1, D_MODEL, TF_FFN), lambda i, j: (layer_idx, 0, j)),
                  pl.BlockSpec((1, TF_FFN, D_MODEL), lambda i, j: (layer_idx, j, 0))],
        out_specs=pl.BlockSpec((TM_FFN, D_MODEL), tile),
        compiler_params=_params("arbitrary", "arbitrary"),
        name="dense_ffn",
    )(h, x, mod_l, w_gate, w_up, w_down)


def _moe_plan(cnt):
    seg_len = (cnt + SEG_ALIGN - 1) // SEG_ALIGN * SEG_ALIGN
    loc_off = jnp.cumsum(seg_len, axis=1) - seg_len
    n_rows = seg_len.sum(axis=0)
    region = (n_rows + MOE_TILE - 1) // MOE_TILE * MOE_TILE
    base = jnp.cumsum(region) - region
    seg_start = base[None, :] + jnp.cumsum(seg_len, axis=0) - seg_len
    tiles_e = region // MOE_TILE
    tile_end = jnp.cumsum(tiles_e)
    n_act = tile_end[-1]
    r = jnp.arange(MOE_MAX_TILES, dtype=jnp.int32)
    r_act = jnp.minimum(r, n_act - 1)
    tile_exp = jnp.sum(r_act[:, None] >= tile_end[None, :], axis=1).astype(jnp.int32)
    rows_left = n_rows[tile_exp] - (r_act - (tile_end - tiles_e)[tile_exp]) * MOE_TILE
    n_sub = jnp.clip((rows_left + MOE_SUB - 1) // MOE_SUB, 0, MOE_TILE // MOE_SUB)
    n_sub = jnp.where(r < n_act, n_sub, 0)
    i32 = lambda a: a.reshape(-1).astype(jnp.int32)
    return dict(seg=(i32(seg_start), i32(loc_off), i32(seg_len)), tiles=(tile_exp, i32(r_act), i32(n_sub)))


def _segment_dma(src, dst, src_off, dst_off, length, sem, wait):
    off = jnp.int32(0)
    for size in SEG_PIECES:
        take = (length & size) != 0
        s0 = pl.multiple_of(src_off + off, SEG_ALIGN)
        d0 = pl.multiple_of(dst_off + off, SEG_ALIGN)

        @pl.when(take)
        def _():
            cp = pltpu.make_async_copy(src.at[pl.ds(s0, size)], dst.at[pl.ds(d0, size)], sem)
            if wait:
                cp.wait()
            else:
                cp.start()

        off = off + jnp.where(take, size, 0)


def _moe_sort_kernel(ss_ref, lo_ref, ln_ref, h_ref, sel1_ref, sel2_ref, xs_in_ref, xs_ref, loc_scr, sem):
    del xs_in_ref
    i = pl.program_id(0)
    tm = TM_PROJ
    member = sel1_ref[...] + sel2_ref[...]
    tr = lax.broadcasted_iota(jnp.int32, (tm, tm), 0)
    tc = lax.broadcasted_iota(jnp.int32, (tm, tm), 1)
    before = jnp.where(tr < tc, 1.0, 0.0).astype(BF16)
    rank_t = _dot_tn(member.astype(BF16), before)[0:8]
    mem_t = member.T[0:8]
    sub = lax.broadcasted_iota(jnp.int32, (8, tm), 0)
    loc = jnp.zeros((8, tm), F32)
    for e in range(N_EXPERTS):
        loc = jnp.where(sub == e, lo_ref[i * N_EXPERTS + e].astype(F32), loc)
    pos_t = jnp.where(mem_t > 0.0, rank_t + loc, -1.0)
    h = h_ref[...]
    blk = MOE_LOC_ROWS // 3
    for rb in range(3):
        rid = (lax.broadcasted_iota(jnp.int32, (blk, tm), 0) + rb * blk).astype(F32)
        perm = jnp.zeros((blk, tm), F32)
        for e in range(N_EXPERTS):
            perm = jnp.where(rid == pos_t[e:e + 1, :], 1.0, perm)
        loc_scr[rb * blk:(rb + 1) * blk, :] = _dot(perm.astype(BF16), h).astype(BF16)
    for wait in (False, True):
        for e in range(N_EXPERTS):
            k = i * N_EXPERTS + e
            _segment_dma(loc_scr, xs_ref, lo_ref[k], ss_ref[k], ln_ref[k], sem, wait)


def _moe_sort(plan, h, sel1, sel2):
    tile = lambda i, *_: (i, 0)
    xs_init = jnp.zeros((MOE_MAX_TILES * MOE_TILE, D_MODEL), BF16)
    return pl.pallas_call(
        _moe_sort_kernel,
        out_shape=jax.ShapeDtypeStruct(xs_init.shape, BF16),
        grid_spec=pltpu.PrefetchScalarGridSpec(
            num_scalar_prefetch=3,
            grid=(TOKENS // TM_PROJ,),
            in_specs=[pl.BlockSpec((TM_PROJ, D_MODEL), tile),
                      pl.BlockSpec((TM_PROJ, LANES), tile),
                      pl.BlockSpec((TM_PROJ, LANES), tile),
                      pl.BlockSpec(memory_space=pl.ANY)],
            out_specs=pl.BlockSpec(memory_space=pl.ANY),
            scratch_shapes=[pltpu.VMEM((MOE_LOC_ROWS, D_MODEL), BF16), pltpu.SemaphoreType.DMA],
        ),
        input_output_aliases={6: 0},
        compiler_params=_params("arbitrary"),
        name="moe_sort",
    )(*plan["seg"], h, sel1, sel2, xs_init)


def _moe_ffn_kernel(te_ref, ra_ref, ns_ref, xs_ref, wg_ref, wu_ref, wd_ref, o_ref, acc_ref):
    del te_ref, ra_ref
    r, j = pl.program_id(0), pl.program_id(1)
    n_sub = ns_ref[r]
    n_parts = MOE_TILE // MOE_SUB
    for used in range(1, n_parts + 1):
        rows = slice(0, used * MOE_SUB)

        @pl.when(n_sub == used)
        def _():
            @pl.when(j == 0)
            def _():
                acc_ref[rows, :] = jnp.zeros((used * MOE_SUB, D_MODEL), F32)

            acc_ref[rows, :] += _swiglu_chunk(xs_ref[rows, :], wg_ref[0, 0], wu_ref[0, 0], wd_ref[0, 0])

            @pl.when(j == N_FF_MOE - 1)
            def _():
                o_ref[rows, :] = acc_ref[rows, :].astype(BF16)

    for s in range(n_parts):
        @pl.when(jnp.logical_and(s >= n_sub, j == N_FF_MOE - 1))
        def _():
            o_ref[s * MOE_SUB:(s + 1) * MOE_SUB, :] = jnp.zeros((MOE_SUB, D_MODEL), BF16)


def _moe_ffn(plan, xs, w_gate, w_up, w_down, layer_idx):
    last = N_FF_MOE - 1
    rows = lambda r, j, te, ra, ns: (ra[r], 0)
    chunk = lambda r, j, ns: jnp.where(ns[r] > 0, j, last)
    return pl.pallas_call(
        _moe_ffn_kernel,
        out_shape=jax.ShapeDtypeStruct(xs.shape, BF16),
        grid_spec=pltpu.PrefetchScalarGridSpec(
            num_scalar_prefetch=3,
            grid=(MOE_MAX_TILES, N_FF_MOE),
            in_specs=[pl.BlockSpec((MOE_TILE, D_MODEL), rows),
                      pl.BlockSpec((1, 1, D_MODEL, TF_MOE),
                                   lambda r, j, te, ra, ns: (layer_idx, te[r], 0, chunk(r, j, ns))),
                      pl.BlockSpec((1, 1, D_MODEL, TF_MOE),
                                   lambda r, j, te, ra, ns: (layer_idx, te[r], 0, chunk(r, j, ns))),
                      pl.BlockSpec((1, 1, TF_MOE, D_MODEL),
                                   lambda r, j, te, ra, ns: (layer_idx, te[r], chunk(r, j, ns), 0))],
            out_specs=pl.BlockSpec((MOE_TILE, D_MODEL), lambda r, j, *_: (r, 0)),
            scratch_shapes=[pltpu.VMEM((MOE_TILE, D_MODEL), F32)],
        ),
        compiler_params=_params("arbitrary", "arbitrary"),
        name="moe_ffn",
    )(*plan["tiles"], xs, w_gate, w_up, w_down)


def _moe_combine_kernel(ss_ref, lo_ref, ln_ref, x_ref, mod_ref, gates_ref, sel1_ref, sel2_ref, *rest,
                        final_norm):
    if final_norm:
        fw_ref, ys_ref, o_ref, loc_scr, sem = rest
    else:
        ys_ref, o_ref, loc_scr, sem = rest
    i = pl.program_id(0)
    tm = TM_PROJ
    slot = lax.rem(i, 2)

    def fetch(tile, buf, wait):
        for e in range(N_EXPERTS):
            k = tile * N_EXPERTS + e
            _segment_dma(ys_ref, loc_scr.at[buf], ss_ref[k], lo_ref[k], ln_ref[k], sem.at[buf], wait)

    @pl.when(i == 0)
    def _():
        loc_scr[...] = jnp.zeros_like(loc_scr)
        fetch(0, 0, False)

    @pl.when(i + 1 < pl.num_programs(0))
    def _():
        fetch(i + 1, 1 - slot, False)

    sel1, sel2, gates = sel1_ref[...], sel2_ref[...], gates_ref[...]
    tr = lax.broadcasted_iota(jnp.int32, (tm, tm), 0)
    tc = lax.broadcasted_iota(jnp.int32, (tm, tm), 1)
    earlier = jnp.where(tr > tc, 1.0, 0.0).astype(BF16)
    lane = lax.broadcasted_iota(jnp.int32, (1, LANES), 1)
    loc = jnp.zeros((1, LANES), F32)
    for e in range(N_EXPERTS):
        loc = jnp.where(lane == e, lo_ref[i * N_EXPERTS + e].astype(F32), loc)
    pos = _dot(earlier, (sel1 + sel2).astype(BF16)) + loc
    pos1 = jnp.sum(sel1 * pos, axis=-1, keepdims=True)
    pos2 = jnp.sum(sel2 * pos, axis=-1, keepdims=True)
    p1 = jnp.sum(sel1 * gates, axis=-1, keepdims=True)
    p2 = jnp.sum(sel2 * gates, axis=-1, keepdims=True)
    cid = lax.broadcasted_iota(jnp.int32, (tm, MOE_LOC_ROWS), 1).astype(F32)
    pick1 = jnp.where(cid == pos1, 1.0, 0.0).astype(BF16)
    pick2 = jnp.where(cid == pos2, 1.0, 0.0).astype(BF16)
    fetch(i, slot, True)
    y = loc_scr[slot]
    ff = p1 * _dot(pick1, y) + p2 * _dot(pick2, y)
    out = x_ref[...] + mod_ref[0, 0, 5:6, :] * ff
    if final_norm:
        ms = jnp.mean(out * out, axis=-1, keepdims=True)
        out = out * lax.rsqrt(ms + EPS) * fw_ref[...]
    o_ref[...] = out


def _moe_combine(plan, x, mod_l, gates, sel1, sel2, ys, final_w=None):
    tiles_per_batch = SEQ // TM_PROJ
    tile = lambda i, *_: (i, 0)
    final_norm = final_w is not None
    in_specs = [pl.BlockSpec((TM_PROJ, D_MODEL), tile),
                pl.BlockSpec((1, 1, 6, D_MODEL), lambda i, *_: (i // tiles_per_batch, 0, 0, 0)),
                pl.BlockSpec((TM_PROJ, LANES), tile),
                pl.BlockSpec((TM_PROJ, LANES), tile),
                pl.BlockSpec((TM_PROJ, LANES), tile)]
    args = [x, mod_l, gates, sel1, sel2]
    if final_norm:
        in_specs.append(pl.BlockSpec((1, D_MODEL), lambda i, *_: (0, 0)))
        args.append(final_w)
    return pl.pallas_call(
        functools.partial(_moe_combine_kernel, final_norm=final_norm),
        out_shape=jax.ShapeDtypeStruct((TOKENS, D_MODEL), F32),
        grid_spec=pltpu.PrefetchScalarGridSpec(
            num_scalar_prefetch=3,
            grid=(TOKENS // TM_PROJ,),
            in_specs=in_specs + [pl.BlockSpec(memory_space=pl.ANY)],
            out_specs=pl.BlockSpec((TM_PROJ, D_MODEL), tile),
            scratch_shapes=[pltpu.VMEM((2, MOE_LOC_ROWS, D_MODEL), BF16), pltpu.SemaphoreType.DMA((2,))],
        ),
        compiler_params=_params("arbitrary"),
        name="moe_combine_norm" if final_norm else "moe_combine",
    )(*plan["seg"], *args, ys)


def _final_norm_kernel(x_ref, w_ref, o_ref):
    x = x_ref[...]
    ms = jnp.mean(x * x, axis=-1, keepdims=True)
    o_ref[...] = x * lax.rsqrt(ms + EPS) * w_ref[...]


def _final_norm(x, w):
    return pl.pallas_call(
        _final_norm_kernel,
        out_shape=jax.ShapeDtypeStruct((TOKENS, D_MODEL), F32),
        grid=(TOKENS // TM_FFN,),
        in_specs=[pl.BlockSpec((TM_FFN, D_MODEL), lambda i: (i, 0)),
                  pl.BlockSpec((1, D_MODEL), lambda i: (0, 0))],
        out_specs=pl.BlockSpec((TM_FFN, D_MODEL), lambda i: (i, 0)),
        compiler_params=_params("arbitrary"),
        name="final_norm",
    )(x, w)


def _pad_cols(a, width):
    return jnp.pad(a, [(0, 0)] * (a.ndim - 1) + [(0, width - a.shape[-1])])


def _split_w_in(w_in):
    ssd_start = G_A + GLA_LOWRANK
    return w_in[:, :, :GLA_COLS].astype(BF16), _pad_cols(w_in[:, :, ssd_start:], SSD_COLS).astype(BF16)


def kernel(x, c, ada_w, ada_b, norm1_w, w_in, gla_a_w, gla_a_b, gla_norm_w, conv_w, conv_b, dt_bias, a_log,
           d_skip, ssd_norm_w, w_out, norm2_w, ffn_w_gate, ffn_w_up, ffn_w_down, router_w, moe_w_gate,
           moe_w_up, moe_w_down, final_norm_w):
    xt = x.reshape(TOKENS, D_MODEL)
    c_pad = jnp.pad(c, ((0, 8 - BATCH), (0, 0)))
    mod = _adaln(c_pad, ada_w, ada_b)[:, :BATCH].reshape(DEPTH, BATCH, 6, D_MODEL)

    w_gla, w_ssd = _split_w_in(w_in)
    w_out_bf = w_out.astype(BF16)
    a_w_p = jnp.pad(gla_a_w, ((0, 0), (0, LANES - GLA_LOWRANK), (0, 0)))
    dtb_p = _pad_cols(dt_bias, LANES)
    alog_p = _pad_cols(a_log, LANES)
    dsk_p = jnp.repeat(d_skip, SSD_HEADDIM, axis=-1)
    rw_p = _pad_cols(router_w, LANES)
    lane_head = jnp.arange(LANES)[:, None]
    e64 = (lane_head == jnp.arange(SSD_WIDTH)[None, :] // SSD_HEADDIM).astype(BF16)
    e128 = (lane_head == jnp.arange(SSD_HEADS * LANES)[None, :] // LANES).astype(BF16)

    for l in range(DEPTH):
        mod_l = mod[l].reshape(BATCH, 1, 6, D_MODEL)
        pg, ps = _inproj(xt, mod_l, norm1_w[l][None], w_gla, w_ssd, l)
        og = _gla(pg, a_w_p[l], gla_a_b[l][None], gla_norm_w[l][None])
        oy = _ssd(ps, conv_w[l], conv_b[l][None], dtb_p[l][None], alog_p[l][None], dsk_p[l][None],
                  ssd_norm_w[l][None], e64, e128)
        i = l // 2
        if l % 2 == 0:
            xt, h2 = _outproj(xt, og, oy, w_out_bf, l, mod_l, norm2_w[l][None], None)
            xt = _ffn(h2, xt, mod_l, ffn_w_gate, ffn_w_up, ffn_w_down, i)
        else:
            xt, h2, gates, sel1, sel2, cnt = _outproj(xt, og, oy, w_out_bf, l, mod_l, norm2_w[l][None], rw_p[i])
            plan = _moe_plan(cnt[:, 0, :N_EXPERTS])
            xs = _moe_sort(plan, h2, sel1, sel2)
            ys = _moe_ffn(plan, xs, moe_w_gate, moe_w_up, moe_w_down, i)
            last = l == DEPTH - 1
            xt = _moe_combine(plan, xt, mod_l, gates, sel1, sel2, ys, final_norm_w[None] if last else None)
    if DEPTH % 2:
        xt = _final_norm(xt, final_norm_w[None])
    return xt.reshape(BATCH, SEQ, D_MODEL)
```

```python
import functools

import jax
import jax.numpy as jnp
from jax import lax
from jax.experimental import pallas as pl
from jax.experimental.pallas import tpu as pltpu

D_MODEL = 1024
BATCH = 2
SEQ = 8192
DEPTH = 4
TOKENS = BATCH * SEQ

GLA_HEADS = 4
GLA_DK = 64
GLA_DV = 128
GLA_QK = GLA_HEADS * GLA_DK
GLA_WIDTH = GLA_HEADS * GLA_DV
GLA_LOWRANK = 16
GLA_TAU = 16.0
GLA_CHUNK = 64

SSD_HEADS = 8
SSD_HEADDIM = 64
SSD_WIDTH = SSD_HEADS * SSD_HEADDIM
SSD_GROUPS = 2
SSD_STATE = 64
SSD_BC = SSD_GROUPS * SSD_STATE
SSD_CONV = 4
SSD_CONV_DIM = SSD_WIDTH + 2 * SSD_BC
SSD_CHUNK = 128

D_FF = 3584
N_EXPERTS = 8
EPS = 1e-6

LANES = 128
GLA_COLS = 2 * GLA_QK + 2 * GLA_WIDTH + LANES
SSD_COLS = 2 * SSD_WIDTH + 2 * SSD_BC + LANES
G_Q, G_K, G_V, G_G, G_A = 0, GLA_QK, 2 * GLA_QK, 2 * GLA_QK + GLA_WIDTH, 2 * GLA_QK + 2 * GLA_WIDTH
S_Z, S_X, S_DT = 0, SSD_WIDTH, SSD_WIDTH + SSD_CONV_DIM

TM_PROJ = 512
TB_SCAN = 512
TM_FFN = 1024
TF_FFN = 512
GLA_GROUP = 4
SSD_HALO = 8
N_FF_CHUNKS = D_FF // TF_FFN
VMEM_LIMIT = 56 * 1024 * 1024

TOP_K = 2
SEG_ALIGN = 16
SEG_PIECES = tuple(TM_PROJ >> s for s in range(6))
MOE_TILE = 1024
MOE_SUB = 512
TF_MOE = 896
N_FF_MOE = D_FF // TF_MOE
N_TILES = TOKENS // TM_PROJ
MOE_LOC_ROWS = -(-(TOP_K * TM_PROJ + N_EXPERTS * (SEG_ALIGN - 1)) // LANES) * LANES
MOE_MAX_TILES = (TOP_K * TOKENS + N_TILES * N_EXPERTS * (SEG_ALIGN - 1) + N_EXPERTS * (MOE_TILE - 1)) // MOE_TILE

F32 = jnp.float32
BF16 = jnp.bfloat16


def _dot(a, b):
    return jnp.dot(a, b, preferred_element_type=F32)


def _dot_nt(a, b):
    return lax.dot_general(a, b, (((1,), (1,)), ((), ())), preferred_element_type=F32)


def _dot_tn(a, b):
    return lax.dot_general(a, b, (((0,), (0,)), ((), ())), preferred_element_type=F32)


def _split(a):
    hi = a.astype(BF16)
    lo = (a - hi.astype(F32)).astype(BF16)
    return hi, lo


def _dot3(a, b):
    a_hi, a_lo = _split(a)
    b_hi, b_lo = _split(b)
    return _dot(a_hi, b_hi) + _dot(a_lo, b_hi) + _dot(a_hi, b_lo)


def _dot_exact_rhs(a, b_bf16):
    a_hi, a_lo = _split(a)
    return _dot(a_hi, b_bf16) + _dot(a_lo, b_bf16)


def _dot_exact_lhs(a_bf16, b):
    b_hi, b_lo = _split(b)
    return _dot(a_bf16, b_hi) + _dot(a_bf16, b_lo)


def _silu(x):
    return x * (0.5 * jnp.tanh(0.5 * x) + 0.5)


def _softplus(x):
    return jnp.maximum(x, 0.0) + jnp.log1p(jnp.exp(-jnp.abs(x)))


def _norm_mod(x, w, scale, shift):
    ms = jnp.mean(x * x, axis=-1, keepdims=True)
    return (x * lax.rsqrt(ms + EPS) * w) * (1.0 + scale) + shift


def _params(*sem):
    return pltpu.CompilerParams(dimension_semantics=sem, vmem_limit_bytes=VMEM_LIMIT)


def _adaln_kernel(c_ref, w_ref, b_ref, o_ref):
    s = _silu(c_ref[...])
    o_ref[0] = _dot3(s, w_ref[0]) + b_ref[0]


def _adaln(c_pad, ada_w, ada_b):
    n_col = 6 * D_MODEL // D_MODEL
    return pl.pallas_call(
        _adaln_kernel,
        out_shape=jax.ShapeDtypeStruct((DEPTH, 8, 6 * D_MODEL), F32),
        grid=(DEPTH, n_col),
        in_specs=[
            pl.BlockSpec((8, D_MODEL), lambda l, j: (0, 0)),
            pl.BlockSpec((1, D_MODEL, D_MODEL), lambda l, j: (l, 0, j)),
            pl.BlockSpec((1, 1, D_MODEL), lambda l, j: (l, 0, j)),
        ],
        out_specs=pl.BlockSpec((1, 8, D_MODEL), lambda l, j: (l, 0, j)),
        compiler_params=_params("arbitrary", "arbitrary"),
        name="adaln",
    )(c_pad, ada_w, ada_b.reshape(DEPTH, 1, 6 * D_MODEL))


def _inproj_kernel(x_ref, mod_ref, nw_ref, wg_ref, ws_ref, og_ref, os_ref):
    mod = mod_ref[0, 0]
    half = TM_PROJ // 2
    for r0 in (0, half):
        rows = slice(r0, r0 + half)
        h = _norm_mod(x_ref[rows, :], nw_ref[...], mod[1:2], mod[0:1]).astype(BF16)
        for w_ref, o_ref in ((wg_ref, og_ref), (ws_ref, os_ref)):
            n_cols = o_ref.shape[1]
            for c0 in range(0, n_cols, 512):
                c1 = min(c0 + 512, n_cols)
                o_ref[rows, c0:c1] = _dot(h, w_ref[0, :, c0:c1])


def _inproj(x, mod_l, norm_w, w_gla, w_ssd, layer):
    tiles_per_batch = SEQ // TM_PROJ
    return pl.pallas_call(
        _inproj_kernel,
        out_shape=(jax.ShapeDtypeStruct((TOKENS, GLA_COLS), F32),
                   jax.ShapeDtypeStruct((TOKENS, SSD_COLS), F32)),
        grid=(TOKENS // TM_PROJ,),
        in_specs=[
            pl.BlockSpec((TM_PROJ, D_MODEL), lambda i: (i, 0)),
            pl.BlockSpec((1, 1, 6, D_MODEL), lambda i: (i // tiles_per_batch, 0, 0, 0)),
            pl.BlockSpec((1, D_MODEL), lambda i: (0, 0)),
            pl.BlockSpec((1, D_MODEL, GLA_COLS), lambda i: (layer, 0, 0)),
            pl.BlockSpec((1, D_MODEL, SSD_COLS), lambda i: (layer, 0, 0)),
        ],
        out_specs=(pl.BlockSpec((TM_PROJ, GLA_COLS), lambda i: (i, 0)),
                   pl.BlockSpec((TM_PROJ, SSD_COLS), lambda i: (i, 0))),
        compiler_params=_params("arbitrary"),
        name="inproj",
    )(x, mod_l, norm_w, w_gla, w_ssd)


def _gla_kernel(pg_ref, aw_ref, ab_ref, nw_ref, o_ref, la_scr, st_scr):
    @pl.when(pl.program_id(1) == 0)
    def _():
        st_scr[...] = jnp.zeros_like(st_scr)

    pre = _dot3(pg_ref[:, G_A:G_A + LANES], aw_ref[...]) + ab_ref[...]
    la_scr[...] = -_softplus(-pre) * (1.0 / GLA_TAU)

    c, gr = GLA_CHUNK, GLA_GROUP * GLA_CHUNK
    row = lax.broadcasted_iota(jnp.int32, (gr, gr), 0)
    col = lax.broadcasted_iota(jnp.int32, (gr, gr), 1)
    causal = jnp.logical_and(row >= col, jnp.bitwise_xor(row, col) < c)
    tril = jnp.where(causal, 1.0, 0.0).astype(BF16)
    causal2 = jnp.concatenate([causal, causal], axis=0)
    low_q = lax.broadcasted_iota(jnp.int32, (gr, LANES), 1) < GLA_DK
    low_s = lax.broadcasted_iota(jnp.int32, (GLA_DV, LANES), 1) < GLA_DK
    nw = nw_ref[...]

    for gi in range(TB_SCAN // gr):
        rows = slice(gi * gr, (gi + 1) * gr)
        b_all = _dot_exact_lhs(tril, la_scr[rows, :])
        for p in range(GLA_HEADS // 2):
            b = b_all[:, p * LANES:(p + 1) * LANES]
            b_last = [b[(ci + 1) * c - 1:(ci + 1) * c, :] for ci in range(GLA_GROUP)]
            b_last_rows = jnp.concatenate([jnp.broadcast_to(bl, (c, LANES)) for bl in b_last], axis=0)
            q = pg_ref[rows, G_Q + p * LANES:G_Q + (p + 1) * LANES] * (GLA_DK ** -0.5)
            k = pg_ref[rows, G_K + p * LANES:G_K + (p + 1) * LANES]
            qe = q * jnp.exp(b)
            ke = (k * jnp.exp(-b)).astype(BF16)
            k_tail = (k * jnp.exp(b_last_rows - b)).astype(BF16)
            v_pair = pg_ref[rows, G_V + 2 * p * GLA_DV:G_V + 2 * (p + 1) * GLA_DV].astype(BF16)
            qm = jnp.concatenate([jnp.where(low_q, qe, 0.0), jnp.where(low_q, 0.0, qe)], axis=0).astype(BF16)
            att = jnp.where(causal2, _dot_nt(qm, ke), 0.0).astype(BF16)
            o_intra = [_dot(att[j * gr:(j + 1) * gr], v_pair[:, j * GLA_DV:(j + 1) * GLA_DV]) for j in range(2)]
            contrib = [_dot_tn(v_pair[ci * c:(ci + 1) * c], k_tail[ci * c:(ci + 1) * c]) for ci in range(GLA_GROUP)]
            st = st_scr[p]
            o_inter = []
            for ci in range(GLA_GROUP):
                q_ci = jnp.concatenate([qm[ci * c:(ci + 1) * c], qm[gr + ci * c:gr + (ci + 1) * c]], axis=0)
                o_inter.append(_dot_nt(q_ci, st.astype(BF16)))
                st = st * jnp.exp(b_last[ci]) + jnp.where(low_s, contrib[ci][:GLA_DV], contrib[ci][GLA_DV:])
            st_scr[p] = st
            for j in range(2):
                h = 2 * p + j
                o = o_intra[j] + jnp.concatenate([oi[j * c:(j + 1) * c] for oi in o_inter], axis=0)
                ms = jnp.mean(o * o, axis=-1, keepdims=True)
                g = pg_ref[rows, G_G + h * GLA_DV:G_G + (h + 1) * GLA_DV]
                o = (o * lax.rsqrt(ms + EPS) * nw) * _silu(g)
                o_ref[rows, h * GLA_DV:(h + 1) * GLA_DV] = o.astype(BF16)


def _gla(pg, a_w, a_b, norm_w):
    nblk = SEQ // TB_SCAN
    return pl.pallas_call(
        _gla_kernel,
        out_shape=jax.ShapeDtypeStruct((TOKENS, GLA_WIDTH), BF16),
        grid=(BATCH, nblk),
        in_specs=[
            pl.BlockSpec((TB_SCAN, GLA_COLS), lambda b, i: (b * nblk + i, 0)),
            pl.BlockSpec((LANES, GLA_QK), lambda b, i: (0, 0)),
            pl.BlockSpec((1, GLA_QK), lambda b, i: (0, 0)),
            pl.BlockSpec((1, GLA_DV), lambda b, i: (0, 0)),
        ],
        out_specs=pl.BlockSpec((TB_SCAN, GLA_WIDTH), lambda b, i: (b * nblk + i, 0)),
        scratch_shapes=[pltpu.VMEM((TB_SCAN, GLA_QK), F32),
                        pltpu.VMEM((GLA_HEADS // 2, GLA_DV, LANES), F32)],
        compiler_params=_params("arbitrary", "arbitrary"),
        name="gla",
    )(pg, a_w, a_b, norm_w)


def _ssd_kernel(ps_ref, cw_ref, cb_ref, dtb_ref, alog_ref, dsk_ref, nw_ref, e64_ref, e128_ref,
                o_ref, xbc_scr, st_scr):
    first = pl.program_id(1) == 0
    halo = SSD_HALO
    c = SSD_CHUNK

    @pl.when(first)
    def _():
        xbc_scr[0:halo, :] = jnp.zeros((halo, SSD_CONV_DIM), F32)
        st_scr[...] = jnp.zeros_like(st_scr)

    @pl.when(jnp.logical_not(first))
    def _():
        xbc_scr[0:halo, :] = xbc_scr[TB_SCAN:TB_SCAN + halo, :]

    xbc_scr[halo:halo + TB_SCAN, :] = ps_ref[:, S_X:S_X + SSD_CONV_DIM]

    row = lax.broadcasted_iota(jnp.int32, (c, c), 0)
    col = lax.broadcasted_iota(jnp.int32, (c, c), 1)
    causal = row >= col
    tril = jnp.where(causal, 1.0, 0.0).astype(BF16)
    head_lane = col < SSD_HEADS
    low_half = col < SSD_HEADDIM
    st_row = lax.broadcasted_iota(jnp.int32, (c, SSD_WIDTH), 0)
    st_col = lax.broadcasted_iota(jnp.int32, (c, SSD_WIDTH), 1)
    blockdiag = (st_row < SSD_STATE) == (st_col < SSD_WIDTH // SSD_GROUPS)
    a_neg = -jnp.exp(alog_ref[...])
    heads_per_group = SSD_HEADS // SSD_GROUPS

    for ci in range(TB_SCAN // c):
        rows = slice(ci * c, (ci + 1) * c)
        win = xbc_scr[ci * c:(ci + 1) * c + halo, :]
        conv = cw_ref[0:1, :] * win
        for k in range(1, SSD_CONV):
            conv = pltpu.roll(conv, 1, axis=0) + cw_ref[k:k + 1, :] * win
        act = _silu(conv[halo:, :] + cb_ref[...])
        dt = jnp.where(head_lane, _softplus(ps_ref[rows, S_DT:S_DT + LANES] + dtb_ref[...]), 0.0)
        cum = _dot_exact_lhs(tril, dt * a_neg)
        cum_t = cum.T
        cum64 = _dot_exact_rhs(cum, e64_ref[...])
        dt64 = _dot_exact_rhs(dt, e64_ref[...])
        cum_col = _dot_exact_rhs(cum, e128_ref[...])
        cl64 = cum64[c - 1:c, :]
        xs = act[:, 0:SSD_WIDTH]
        bm = act[:, SSD_WIDTH:SSD_WIDTH + SSD_BC].astype(BF16)
        cm = act[:, SSD_WIDTH + SSD_BC:SSD_CONV_DIM]
        xdt = xs * dt64
        xdt_bf = xdt.astype(BF16)
        y_parts = []
        for g in range(SSD_GROUPS):
            cm_g = jnp.where(low_half if g == 0 else jnp.logical_not(low_half), cm, 0.0).astype(BF16)
            scores = _dot_nt(cm_g, bm)
            for pp in range(heads_per_group // 2):
                p = g * (heads_per_group // 2) + pp
                ys = []
                for j in range(2):
                    h = 2 * p + j
                    seg = cum_col[:, h * LANES:(h + 1) * LANES] - cum_t[h:h + 1, :]
                    decay = jnp.exp(jnp.where(causal, seg, -jnp.inf))
                    ys.append(_dot((scores * decay).astype(BF16), xdt_bf[:, p * LANES:(p + 1) * LANES]))
                y_parts.append(jnp.where(low_half, ys[0], ys[1]))
        y = jnp.concatenate(y_parts, axis=1)
        st = st_scr[...]
        y = y + _dot(cm.astype(BF16), st.astype(BF16)) * jnp.exp(cum64)
        contrib = _dot_tn(bm, (xdt * jnp.exp(cl64 - cum64)).astype(BF16))
        st_scr[...] = st * jnp.exp(cl64) + jnp.where(blockdiag, contrib, 0.0)
        y = y + dsk_ref[...] * xs
        y = y * _silu(ps_ref[rows, S_Z:S_Z + SSD_WIDTH])
        ms = jnp.mean(y * y, axis=-1, keepdims=True)
        o_ref[rows, :] = (y * lax.rsqrt(ms + EPS) * nw_ref[...]).astype(BF16)


def _ssd(ps, conv_w, conv_b, dt_bias, a_log, d_skip, norm_w, e64, e128):
    nblk = SEQ // TB_SCAN
    const = lambda b, i: (0, 0)
    return pl.pallas_call(
        _ssd_kernel,
        out_shape=jax.ShapeDtypeStruct((TOKENS, SSD_WIDTH), BF16),
        grid=(BATCH, nblk),
        in_specs=[
            pl.BlockSpec((TB_SCAN, SSD_COLS), lambda b, i: (b * nblk + i, 0)),
            pl.BlockSpec((SSD_CONV, SSD_CONV_DIM), const),
            pl.BlockSpec((1, SSD_CONV_DIM), const),
            pl.BlockSpec((1, LANES), const),
            pl.BlockSpec((1, LANES), const),
            pl.BlockSpec((1, SSD_WIDTH), const),
            pl.BlockSpec((1, SSD_WIDTH), const),
            pl.BlockSpec((LANES, SSD_WIDTH), const),
            pl.BlockSpec((LANES, SSD_HEADS * LANES), const),
        ],
        out_specs=pl.BlockSpec((TB_SCAN, SSD_WIDTH), lambda b, i: (b * nblk + i, 0)),
        scratch_shapes=[pltpu.VMEM((TB_SCAN + SSD_HALO, SSD_CONV_DIM), F32),
                        pltpu.VMEM((SSD_BC, SSD_WIDTH), F32)],
        compiler_params=_params("arbitrary", "arbitrary"),
        name="ssd",
    )(ps, conv_w, conv_b, dt_bias, a_log, d_skip, norm_w, e64, e128)


def _outproj_kernel(x_ref, og_ref, oy_ref, wo_ref, mod_ref, nw_ref, *rest, with_router):
    if with_router:
        rw_ref, xo_ref, h_ref, gates_ref, sel1_ref, sel2_ref, cnt_ref = rest
    else:
        xo_ref, h_ref = rest
    mod = mod_ref[0, 0]
    half = TM_PROJ // 2
    counts = jnp.zeros((1, LANES), F32)
    for r0 in (0, half):
        rows = slice(r0, r0 + half)
        mix = (_dot(og_ref[rows, :], wo_ref[0, 0:GLA_WIDTH, :])
               + _dot(oy_ref[rows, :], wo_ref[0, GLA_WIDTH:, :]))
        x_new = x_ref[rows, :] + mod[2:3] * mix
        xo_ref[rows, :] = x_new
        h = _norm_mod(x_new, nw_ref[...], mod[4:5], mod[3:4])
        h_ref[rows, :] = h.astype(BF16)
        if with_router:
            logits = _dot3(h, rw_ref[...])
            lane = lax.broadcasted_iota(jnp.int32, logits.shape, 1)
            lg = jnp.where(lane < N_EXPERTS, logits, -jnp.inf)
            v1 = jnp.max(lg, axis=-1, keepdims=True)
            i1 = jnp.min(jnp.where(lg == v1, lane, LANES), axis=-1, keepdims=True)
            lg2 = jnp.where(lane == i1, -jnp.inf, lg)
            v2 = jnp.max(lg2, axis=-1, keepdims=True)
            i2 = jnp.min(jnp.where(lg2 == v2, lane, LANES), axis=-1, keepdims=True)
            e2 = jnp.exp(v2 - v1)
            p1 = 1.0 / (1.0 + e2)
            p2 = e2 / (1.0 + e2)
            sel1 = jnp.where(lane == i1, 1.0, 0.0)
            sel2 = jnp.where(lane == i2, 1.0, 0.0)
            gates_ref[rows, :] = sel1 * p1 + sel2 * p2
            sel1_ref[rows, :] = sel1
            sel2_ref[rows, :] = sel2
            counts = counts + jnp.sum(sel1 + sel2, axis=0, keepdims=True)
    if with_router:
        cnt_ref[0] = jnp.broadcast_to(counts.astype(jnp.int32), (8, LANES))


def _outproj(x, og, oy, w_out, layer, mod_l, norm_w, router_w):
    tiles_per_batch = SEQ // TM_PROJ
    with_router = router_w is not None
    tile = lambda i: (i, 0)
    const = lambda i: (0, 0)
    in_specs = [
        pl.BlockSpec((TM_PROJ, D_MODEL), tile),
        pl.BlockSpec((TM_PROJ, GLA_WIDTH), tile),
        pl.BlockSpec((TM_PROJ, SSD_WIDTH), tile),
        pl.BlockSpec((1, D_MODEL, D_MODEL), lambda i: (layer, 0, 0)),
        pl.BlockSpec((1, 1, 6, D_MODEL), lambda i: (i // tiles_per_batch, 0, 0, 0)),
        pl.BlockSpec((1, D_MODEL), const),
    ]
    out_shape = [jax.ShapeDtypeStruct((TOKENS, D_MODEL), F32),
                 jax.ShapeDtypeStruct((TOKENS, D_MODEL), BF16)]
    out_specs = [pl.BlockSpec((TM_PROJ, D_MODEL), tile), pl.BlockSpec((TM_PROJ, D_MODEL), tile)]
    args = [x, og, oy, w_out, mod_l, norm_w]
    if with_router:
        in_specs.append(pl.BlockSpec((D_MODEL, LANES), const))
        for _ in range(3):
            out_shape.append(jax.ShapeDtypeStruct((TOKENS, LANES), F32))
            out_specs.append(pl.BlockSpec((TM_PROJ, LANES), tile))
        out_shape.append(jax.ShapeDtypeStruct((TOKENS // TM_PROJ, 8, LANES), jnp.int32))
        out_specs.append(pl.BlockSpec((1, 8, LANES), lambda i: (i, 0, 0)))
        args.append(router_w)
    return pl.pallas_call(
        functools.partial(_outproj_kernel, with_router=with_router),
        out_shape=tuple(out_shape),
        grid=(TOKENS // TM_PROJ,),
        in_specs=in_specs,
        out_specs=tuple(out_specs),
        compiler_params=_params("arbitrary"),
        name="outproj_router" if with_router else "outproj",
    )(*args)


def _swiglu_chunk(h, wg, wu, wd):
    a = (_silu(_dot(h, wg.astype(BF16))) * _dot(h, wu.astype(BF16))).astype(BF16)
    return _dot(a, wd.astype(BF16))


def _ffn_kernel(h_ref, x_ref, mod_ref, wg_ref, wu_ref, wd_ref, o_ref):
    @pl.when(pl.program_id(1) == 0)
    def _():
        o_ref[...] = x_ref[...]

    o_ref[...] += mod_ref[0, 0, 5:6, :] * _swiglu_chunk(h_ref[...], wg_ref[0], wu_ref[0], wd_ref[0])


def _ffn(h, x, mod_l, w_gate, w_up, w_down, layer_idx):
    tiles_per_batch = SEQ // TM_FFN
    tile = lambda i, j: (i, 0)
    return pl.pallas_call(
        _ffn_kernel,
        out_shape=jax.ShapeDtypeStruct((TOKENS, D_MODEL), F32),
        grid=(TOKENS // TM_FFN, N_FF_CHUNKS),
        in_specs=[pl.BlockSpec((TM_FFN, D_MODEL), tile),
                  pl.BlockSpec((TM_FFN, D_MODEL), tile),
                  pl.BlockSpec((1, 1, 6, D_MODEL), lambda i, j: (i // tiles_per_batch, 0, 0, 0)),
                  pl.BlockSpec((1, D_MODEL, TF_FFN), lambda i, j: (layer_idx, 0, j)),
                  pl.BlockSpec((1, D_MODEL, TF_FFN), lambda i, j: (layer_idx, 0, j)),
                  pl.BlockSpec((1, TF_FFN, D_MODEL), lambda i, j: (layer_idx, j, 0))],
        out_specs=pl.BlockSpec((TM_FFN, D_MODEL), tile),
        compiler_params=_params("arbitrary", "arbitrary"),
        name="dense_ffn",
    )(h, x, mod_l, w_gate, w_up, w_down)


def _moe_plan(cnt):
    seg_len = (cnt + SEG_ALIGN - 1) // SEG_ALIGN * SEG_ALIGN
    loc_off = jnp.cumsum(seg_len, axis=1) - seg_len
    n_rows = seg_len.sum(axis=0)
    region = (n_rows + MOE_TILE - 1) // MOE_TILE * MOE_TILE
    base = jnp.cumsum(region) - region
    seg_start = base[None, :] + jnp.cumsum(seg_len, axis=0) - seg_len
    tiles_e = region // MOE_TILE
    tile_end = jnp.cumsum(tiles_e)
    n_act = tile_end[-1]
    r = jnp.arange(MOE_MAX_TILES, dtype=jnp.int32)
    r_act = jnp.minimum(r, n_act - 1)
    tile_exp = jnp.sum(r_act[:, None] >= tile_end[None, :], axis=1).astype(jnp.int32)
    rows_left = n_rows[tile_exp] - (r_act - (tile_end - tiles_e)[tile_exp]) * MOE_TILE
    n_sub = jnp.clip((rows_left + MOE_SUB - 1) // MOE_SUB, 0, MOE_TILE // MOE_SUB)
    n_sub = jnp.where(r < n_act, n_sub, 0)
    i32 = lambda a: a.reshape(-1).astype(jnp.int32)
    return dict(seg=(i32(seg_start), i32(loc_off), i32(seg_len)), tiles=(tile_exp, i32(r_act), i32(n_sub)))


def _segment_dma(src, dst, src_off, dst_off, length, sem, wait):
    off = jnp.int32(0)
    for size in SEG_PIECES:
        take = (length & size) != 0
        s0 = pl.multiple_of(src_off + off, SEG_ALIGN)
        d0 = pl.multiple_of(dst_off + off, SEG_ALIGN)

        @pl.when(take)
        def _():
            cp = pltpu.make_async_copy(src.at[pl.ds(s0, size)], dst.at[pl.ds(d0, size)], sem)
            if wait:
                cp.wait()
            else:
                cp.start()

        off = off + jnp.where(take, size, 0)


def _moe_sort_kernel(ss_ref, lo_ref, ln_ref, h_ref, sel1_ref, sel2_ref, xs_in_ref, xs_ref, loc_scr, sem):
    del xs_in_ref
    i = pl.program_id(0)
    tm = TM_PROJ
    member = sel1_ref[...] + sel2_ref[...]
    tr = lax.broadcasted_iota(jnp.int32, (tm, tm), 0)
    tc = lax.broadcasted_iota(jnp.int32, (tm, tm), 1)
    before = jnp.where(tr < tc, 1.0, 0.0).astype(BF16)
    rank_t = _dot_tn(member.astype(BF16), before)[0:8]
    mem_t = member.T[0:8]
    sub = lax.broadcasted_iota(jnp.int32, (8, tm), 0)
    loc = jnp.zeros((8, tm), F32)
    for e in range(N_EXPERTS):
        loc = jnp.where(sub == e, lo_ref[i * N_EXPERTS + e].astype(F32), loc)
    pos_t = jnp.where(mem_t > 0.0, rank_t + loc, -1.0)
    h = h_ref[...]
    blk = MOE_LOC_ROWS // 3
    for rb in range(3):
        rid = (lax.broadcasted_iota(jnp.int32, (blk, tm), 0) + rb * blk).astype(F32)
        perm = jnp.zeros((blk, tm), F32)
        for e in range(N_EXPERTS):
            perm = jnp.where(rid == pos_t[e:e + 1, :], 1.0, perm)
        loc_scr[rb * blk:(rb + 1) * blk, :] = _dot(perm.astype(BF16), h).astype(BF16)
    for wait in (False, True):
        for e in range(N_EXPERTS):
            k = i * N_EXPERTS + e
            _segment_dma(loc_scr, xs_ref, lo_ref[k], ss_ref[k], ln_ref[k], sem, wait)


def _moe_sort(plan, h, sel1, sel2):
    tile = lambda i, *_: (i, 0)
    xs_init = jnp.zeros((MOE_MAX_TILES * MOE_TILE, D_MODEL), BF16)
    return pl.pallas_call(
        _moe_sort_kernel,
        out_shape=jax.ShapeDtypeStruct(xs_init.shape, BF16),
        grid_spec=pltpu.PrefetchScalarGridSpec(
            num_scalar_prefetch=3,
            grid=(TOKENS // TM_PROJ,),
            in_specs=[pl.BlockSpec((TM_PROJ, D_MODEL), tile),
                      pl.BlockSpec((TM_PROJ, LANES), tile),
                      pl.BlockSpec((TM_PROJ, LANES), tile),
                      pl.BlockSpec(memory_space=pl.ANY)],
            out_specs=pl.BlockSpec(memory_space=pl.ANY),
            scratch_shapes=[pltpu.VMEM((MOE_LOC_ROWS, D_MODEL), BF16), pltpu.SemaphoreType.DMA],
        ),
        input_output_aliases={6: 0},
        compiler_params=_params("arbitrary"),
        name="moe_sort",
    )(*plan["seg"], h, sel1, sel2, xs_init)


def _moe_ffn_kernel(te_ref, ra_ref, ns_ref, xs_ref, wg_ref, wu_ref, wd_ref, o_ref, acc_ref):
    del te_ref, ra_ref
    r, j = pl.program_id(0), pl.program_id(1)
    n_sub = ns_ref[r]
    n_parts = MOE_TILE // MOE_SUB
    for used in range(1, n_parts + 1):
        rows = slice(0, used * MOE_SUB)

        @pl.when(n_sub == used)
        def _():
            @pl.when(j == 0)
            def _():
                acc_ref[rows, :] = jnp.zeros((used * MOE_SUB, D_MODEL), F32)

            acc_ref[rows, :] += _swiglu_chunk(xs_ref[rows, :], wg_ref[0, 0], wu_ref[0, 0], wd_ref[0, 0])

            @pl.when(j == N_FF_MOE - 1)
            def _():
                o_ref[rows, :] = acc_ref[rows, :].astype(BF16)

    for s in range(n_parts):
        @pl.when(jnp.logical_and(s >= n_sub, j == N_FF_MOE - 1))
        def _():
            o_ref[s * MOE_SUB:(s + 1) * MOE_SUB, :] = jnp.zeros((MOE_SUB, D_MODEL), BF16)


def _moe_ffn(plan, xs, w_gate, w_up, w_down, layer_idx):
    last = N_FF_MOE - 1
    rows = lambda r, j, te, ra, ns: (ra[r], 0)
    chunk = lambda r, j, ns: jnp.where(ns[r] > 0, j, last)
    return pl.pallas_call(
        _moe_ffn_kernel,
        out_shape=jax.ShapeDtypeStruct(xs.shape, BF16),
        grid_spec=pltpu.PrefetchScalarGridSpec(
            num_scalar_prefetch=3,
            grid=(MOE_MAX_TILES, N_FF_MOE),
            in_specs=[pl.BlockSpec((MOE_TILE, D_MODEL), rows),
                      pl.BlockSpec((1, 1, D_MODEL, TF_MOE),
                                   lambda r, j, te, ra, ns: (layer_idx, te[r], 0, chunk(r, j, ns))),
                      pl.BlockSpec((1, 1, D_MODEL, TF_MOE),
                                   lambda r, j, te, ra, ns: (layer_idx, te[r], 0, chunk(r, j, ns))),
                      pl.BlockSpec((1, 1, TF_MOE, D_MODEL),
                                   lambda r, j, te, ra, ns: (layer_idx, te[r], chunk(r, j, ns), 0))],
            out_specs=pl.BlockSpec((MOE_TILE, D_MODEL), lambda r, j, *_: (r, 0)),
            scratch_shapes=[pltpu.VMEM((MOE_TILE, D_MODEL), F32)],
        ),
        compiler_params=_params("arbitrary", "arbitrary"),
        name="moe_ffn",
    )(*plan["tiles"], xs, w_gate, w_up, w_down)


def _moe_combine_kernel(ss_ref, lo_ref, ln_ref, x_ref, mod_ref, gates_ref, sel1_ref, sel2_ref, *rest,
                        final_norm):
    if final_norm:
        fw_ref, ys_ref, o_ref, loc_scr, sem = rest
    else:
        ys_ref, o_ref, loc_scr, sem = rest
    i = pl.program_id(0)
    tm = TM_PROJ
    slot = lax.rem(i, 2)

    def fetch(tile, buf, wait):
        for e in range(N_EXPERTS):
            k = tile * N_EXPERTS + e
            _segment_dma(ys_ref, loc_scr.at[buf], ss_ref[k], lo_ref[k], ln_ref[k], sem.at[buf], wait)

    @pl.when(i == 0)
    def _():
        loc_scr[...] = jnp.zeros_like(loc_scr)
        fetch(0, 0, False)

    @pl.when(i + 1 < pl.num_programs(0))
    def _():
        fetch(i + 1, 1 - slot, False)

    sel1, sel2, gates = sel1_ref[...], sel2_ref[...], gates_ref[...]
    tr = lax.broadcasted_iota(jnp.int32, (tm, tm), 0)
    tc = lax.broadcasted_iota(jnp.int32, (tm, tm), 1)
    earlier = jnp.where(tr > tc, 1.0, 0.0).astype(BF16)
    lane = lax.broadcasted_iota(jnp.int32, (1, LANES), 1)
    loc = jnp.zeros((1, LANES), F32)
    for e in range(N_EXPERTS):
        loc = jnp.where(lane == e, lo_ref[i * N_EXPERTS + e].astype(F32), loc)
    pos = _dot(earlier, (sel1 + sel2).astype(BF16)) + loc
    pos1 = jnp.sum(sel1 * pos, axis=-1, keepdims=True)
    pos2 = jnp.sum(sel2 * pos, axis=-1, keepdims=True)
    p1 = jnp.sum(sel1 * gates, axis=-1, keepdims=True)
    p2 = jnp.sum(sel2 * gates, axis=-1, keepdims=True)
    cid = lax.broadcasted_iota(jnp.int32, (tm, MOE_LOC_ROWS), 1).astype(F32)
    pick1 = jnp.where(cid == pos1, 1.0, 0.0).astype(BF16)
    pick2 = jnp.where(cid == pos2, 1.0, 0.0).astype(BF16)
    fetch(i, slot, True)
    y = loc_scr[slot]
    ff = p1 * _dot(pick1, y) + p2 * _dot(pick2, y)
    out = x_ref[...] + mod_ref[0, 0, 5:6, :] * ff
    if final_norm:
        ms = jnp.mean(out * out, axis=-1, keepdims=True)
        out = out * lax.rsqrt(ms + EPS) * fw_ref[...]
    o_ref[...] = out


def _moe_combine(plan, x, mod_l, gates, sel1, sel2, ys, final_w=None):
    tiles_per_batch = SEQ // TM_PROJ
    tile = lambda i, *_: (i, 0)
    final_norm = final_w is not None
    in_specs = [pl.BlockSpec((TM_PROJ, D_MODEL), tile),
                pl.BlockSpec((1, 1, 6, D_MODEL), lambda i, *_: (i // tiles_per_batch, 0, 0, 0)),
                pl.BlockSpec((TM_PROJ, LANES), tile),
                pl.BlockSpec((TM_PROJ, LANES), tile),
                pl.BlockSpec((TM_PROJ, LANES), tile)]
    args = [x, mod_l, gates, sel1, sel2]
    if final_norm:
        in_specs.append(pl.BlockSpec((1, D_MODEL), lambda i, *_: (0, 0)))
        args.append(final_w)
    return pl.pallas_call(
        functools.partial(_moe_combine_kernel, final_norm=final_norm),
        out_shape=jax.ShapeDtypeStruct((TOKENS, D_MODEL), F32),
        grid_spec=pltpu.PrefetchScalarGridSpec(
            num_scalar_prefetch=3,
            grid=(TOKENS // TM_PROJ,),
            in_specs=in_specs + [pl.BlockSpec(memory_space=pl.ANY)],
            out_specs=pl.BlockSpec((TM_PROJ, D_MODEL), tile),
            scratch_shapes=[pltpu.VMEM((2, MOE_LOC_ROWS, D_MODEL), BF16), pltpu.SemaphoreType.DMA((2,))],
        ),
        compiler_params=_params("arbitrary"),
        name="moe_combine_norm" if final_norm else "moe_combine",
    )(*plan["seg"], *args, ys)


def _final_norm_kernel(x_ref, w_ref, o_ref):
    x = x_ref[...]
    ms = jnp.mean(x * x, axis=-1, keepdims=True)
    o_ref[...] = x * lax.rsqrt(ms + EPS) * w_ref[...]


def _final_norm(x, w):
    return pl.pallas_call(
        _final_norm_kernel,
        out_shape=jax.ShapeDtypeStruct((TOKENS, D_MODEL), F32),
        grid=(TOKENS // TM_FFN,),
        in_specs=[pl.BlockSpec((TM_FFN, D_MODEL), lambda i: (i, 0)),
                  pl.BlockSpec((1, D_MODEL), lambda i: (0, 0))],
        out_specs=pl.BlockSpec((TM_FFN, D_MODEL), lambda i: (i, 0)),
        compiler_params=_params("arbitrary"),
        name="final_norm",
    )(x, w)


def _pad_cols(a, width):
    return jnp.pad(a, [(0, 0)] * (a.ndim - 1) + [(0, width - a.shape[-1])])


def _split_w_in(w_in):
    ssd_start = G_A + GLA_LOWRANK
    return w_in[:, :, :GLA_COLS].astype(BF16), _pad_cols(w_in[:, :, ssd_start:], SSD_COLS).astype(BF16)


def kernel(x, c, ada_w, ada_b, norm1_w, w_in, gla_a_w, gla_a_b, gla_norm_w, conv_w, conv_b, dt_bias, a_log,
           d_skip, ssd_norm_w, w_out, norm2_w, ffn_w_gate, ffn_w_up, ffn_w_down, router_w, moe_w_gate,
           moe_w_up, moe_w_down, final_norm_w):
    xt = x.reshape(TOKENS, D_MODEL)
    c_pad = jnp.pad(c, ((0, 8 - BATCH), (0, 0)))
    mod = _adaln(c_pad, ada_w, ada_b)[:, :BATCH].reshape(DEPTH, BATCH, 6, D_MODEL)

    w_gla, w_ssd = _split_w_in(w_in)
    w_out_bf = w_out.astype(BF16)
    a_w_p = jnp.pad(gla_a_w, ((0, 0), (0, LANES - GLA_LOWRANK), (0, 0)))
    dtb_p = _pad_cols(dt_bias, LANES)
    alog_p = _pad_cols(a_log, LANES)
    dsk_p = jnp.repeat(d_skip, SSD_HEADDIM, axis=-1)
    rw_p = _pad_cols(router_w, LANES)
    lane_head = jnp.arange(LANES)[:, None]
    e64 = (lane_head == jnp.arange(SSD_WIDTH)[None, :] // SSD_HEADDIM).astype(BF16)
    e128 = (lane_head == jnp.arange(SSD_HEADS * LANES)[None, :] // LANES).astype(BF16)

    for l in range(DEPTH):
        mod_l = mod[l].reshape(BATCH, 1, 6, D_MODEL)
        pg, ps = _inproj(xt, mod_l, norm1_w[l][None], w_gla, w_ssd, l)
        og = _gla(pg, a_w_p[l], gla_a_b[l][None], gla_norm_w[l][None])
        oy = _ssd(ps, conv_w[l], conv_b[l][None], dtb_p[l][None], alog_p[l][None], dsk_p[l][None],
                  ssd_norm_w[l][None], e64, e128)
        i = l // 2
        if l % 2 == 0:
            xt, h2 = _outproj(xt, og, oy, w_out_bf, l, mod_l, norm2_w[l][None], None)
            xt = _ffn(h2, xt, mod_l, ffn_w_gate, ffn_w_up, ffn_w_down, i)
        else:
            xt, h2, gates, sel1, sel2, cnt = _outproj(xt, og, oy, w_out_bf, l, mod_l, norm2_w[l][None], rw_p[i])
            plan = _moe_plan(cnt[:, 0, :N_EXPERTS])
            xs = _moe_sort(plan, h2, sel1, sel2)
            ys = _moe_ffn(plan, xs, moe_w_gate, moe_w_up, moe_w_down, i)
            last = l == DEPTH - 1
            xt = _moe_combine(plan, xt, mod_l, gates, sel1, sel2, ys, final_norm_w[None] if last else None)
    if DEPTH % 2:
        xt = _final_norm(xt, final_norm_w[None])
    return xt.reshape(BATCH, SEQ, D_MODEL)
```

```python
import functools

import jax
import jax.numpy as jnp
from jax import lax
from jax.experimental import pallas as pl
from jax.experimental.pallas import tpu as pltpu

D_MODEL = 1024
BATCH = 2
SEQ = 8192
DEPTH = 4
TOKENS = BATCH * SEQ

GLA_HEADS = 4
GLA_DK = 64
GLA_DV = 128
GLA_QK = GLA_HEADS * GLA_DK
GLA_WIDTH = GLA_HEADS * GLA_DV
GLA_LOWRANK = 16
GLA_TAU = 16.0
GLA_CHUNK = 64

SSD_HEADS = 8
SSD_HEADDIM = 64
SSD_WIDTH = SSD_HEADS * SSD_HEADDIM
SSD_GROUPS = 2
SSD_STATE = 64
SSD_BC = SSD_GROUPS * SSD_STATE
SSD_CONV = 4
SSD_CONV_DIM = SSD_WIDTH + 2 * SSD_BC
SSD_CHUNK = 128

D_FF = 3584
N_EXPERTS = 8
EPS = 1e-6

LANES = 128
GLA_COLS = 2 * GLA_QK + 2 * GLA_WIDTH + LANES
SSD_COLS = 2 * SSD_WIDTH + 2 * SSD_BC + LANES
G_Q, G_K, G_V, G_G, G_A = 0, GLA_QK, 2 * GLA_QK, 2 * GLA_QK + GLA_WIDTH, 2 * GLA_QK + 2 * GLA_WIDTH
S_Z, S_X, S_DT = 0, SSD_WIDTH, SSD_WIDTH + SSD_CONV_DIM

TM_PROJ = 512
TB_SCAN = 512
TM_FFN = 1024
TF_FFN = 512
GLA_GROUP = 4
SSD_HALO = 8
N_FF_CHUNKS = D_FF // TF_FFN
VMEM_LIMIT = 56 * 1024 * 1024

TOP_K = 2
SEG_ALIGN = 16
SEG_PIECES = tuple(TM_PROJ >> s for s in range(6))
MOE_TILE = 1024
MOE_SUB = 512
TF_MOE = 896
N_FF_MOE = D_FF // TF_MOE
N_TILES = TOKENS // TM_PROJ
MOE_LOC_ROWS = -(-(TOP_K * TM_PROJ + N_EXPERTS * (SEG_ALIGN - 1)) // LANES) * LANES
MOE_MAX_TILES = (TOP_K * TOKENS + N_TILES * N_EXPERTS * (SEG_ALIGN - 1) + N_EXPERTS * (MOE_TILE - 1)) // MOE_TILE

F32 = jnp.float32
BF16 = jnp.bfloat16


def _dot(a, b):
    return jnp.dot(a, b, preferred_element_type=F32)


def _dot_nt(a, b):
    return lax.dot_general(a, b, (((1,), (1,)), ((), ())), preferred_element_type=F32)


def _dot_tn(a, b):
    return lax.dot_general(a, b, (((0,), (0,)), ((), ())), preferred_element_type=F32)


def _split(a):
    hi = a.astype(BF16)
    lo = (a - hi.astype(F32)).astype(BF16)
    return hi, lo


def _dot3(a, b):
    a_hi, a_lo = _split(a)
    b_hi, b_lo = _split(b)
    return _dot(a_hi, b_hi) + _dot(a_lo, b_hi) + _dot(a_hi, b_lo)


def _dot_exact_rhs(a, b_bf16):
    a_hi, a_lo = _split(a)
    return _dot(a_hi, b_bf16) + _dot(a_lo, b_bf16)


def _dot_exact_lhs(a_bf16, b):
    b_hi, b_lo = _split(b)
    return _dot(a_bf16, b_hi) + _dot(a_bf16, b_lo)


def _silu(x):
    return x * (0.5 * jnp.tanh(0.5 * x) + 0.5)


def _softplus(x):
    return jnp.maximum(x, 0.0) + jnp.log1p(jnp.exp(-jnp.abs(x)))


def _norm_mod(x, w, scale, shift):
    ms = jnp.mean(x * x, axis=-1, keepdims=True)
    return (x * lax.rsqrt(ms + EPS) * w) * (1.0 + scale) + shift


def _params(*sem):
    return pltpu.CompilerParams(dimension_semantics=sem, vmem_limit_bytes=VMEM_LIMIT)


def _adaln_kernel(c_ref, w_ref, b_ref, o_ref):
    s = _silu(c_ref[...])
    o_ref[0] = _dot3(s, w_ref[0]) + b_ref[0]


def _adaln(c_pad, ada_w, ada_b):
    n_col = 6 * D_MODEL // D_MODEL
    return pl.pallas_call(
        _adaln_kernel,
        out_shape=jax.ShapeDtypeStruct((DEPTH, 8, 6 * D_MODEL), F32),
        grid=(DEPTH, n_col),
        in_specs=[
            pl.BlockSpec((8, D_MODEL), lambda l, j: (0, 0)),
            pl.BlockSpec((1, D_MODEL, D_MODEL), lambda l, j: (l, 0, j)),
            pl.BlockSpec((1, 1, D_MODEL), lambda l, j: (l, 0, j)),
        ],
        out_specs=pl.BlockSpec((1, 8, D_MODEL), lambda l, j: (l, 0, j)),
        compiler_params=_params("arbitrary", "arbitrary"),
        name="adaln",
    )(c_pad, ada_w, ada_b.reshape(DEPTH, 1, 6 * D_MODEL))


def _inproj_kernel(x_ref, mod_ref, nw_ref, wg_ref, ws_ref, og_ref, os_ref):
    mod = mod_ref[0, 0]
    half = TM_PROJ // 2
    for r0 in (0, half):
        rows = slice(r0, r0 + half)
        h = _norm_mod(x_ref[rows, :], nw_ref[...], mod[1:2], mod[0:1]).astype(BF16)
        for w_ref, o_ref in ((wg_ref, og_ref), (ws_ref, os_ref)):
            n_cols = o_ref.shape[1]
            for c0 in range(0, n_cols, 512):
                c1 = min(c0 + 512, n_cols)
                o_ref[rows, c0:c1] = _dot(h, w_ref[0, :, c0:c1])


def _inproj(x, mod_l, norm_w, w_gla, w_ssd, layer):
    tiles_per_batch = SEQ // TM_PROJ
    return pl.pallas_call(
        _inproj_kernel,
        out_shape=(jax.ShapeDtypeStruct((TOKENS, GLA_COLS), F32),
                   jax.ShapeDtypeStruct((TOKENS, SSD_COLS), F32)),
        grid=(TOKENS // TM_PROJ,),
        in_specs=[
            pl.BlockSpec((TM_PROJ, D_MODEL), lambda i: (i, 0)),
            pl.BlockSpec((1, 1, 6, D_MODEL), lambda i: (i // tiles_per_batch, 0, 0, 0)),
            pl.BlockSpec((1, D_MODEL), lambda i: (0, 0)),
            pl.BlockSpec((1, D_MODEL, GLA_COLS), lambda i: (layer, 0, 0)),
            pl.BlockSpec((1, D_MODEL, SSD_COLS), lambda i: (layer, 0, 0)),
        ],
        out_specs=(pl.BlockSpec((TM_PROJ, GLA_COLS), lambda i: (i, 0)),
                   pl.BlockSpec((TM_PROJ, SSD_COLS), lambda i: (i, 0))),
        compiler_params=_params("arbitrary"),
        name="inproj",
    )(x, mod_l, norm_w, w_gla, w_ssd)


def _gla_kernel(pg_ref, aw_ref, ab_ref, nw_ref, o_ref, la_scr, st_scr):
    @pl.when(pl.program_id(1) == 0)
    def _():
        st_scr[...] = jnp.zeros_like(st_scr)

    pre = _dot3(pg_ref[:, G_A:G_A + LANES], aw_ref[...]) + ab_ref[...]
    la_scr[...] = -_softplus(-pre) * (1.0 / GLA_TAU)

    c, gr = GLA_CHUNK, GLA_GROUP * GLA_CHUNK
    row = lax.broadcasted_iota(jnp.int32, (gr, gr), 0)
    col = lax.broadcasted_iota(jnp.int32, (gr, gr), 1)
    causal = jnp.logical_and(row >= col, jnp.bitwise_xor(row, col) < c)
    tril = jnp.where(causal, 1.0, 0.0).astype(BF16)
    causal2 = jnp.concatenate([causal, causal], axis=0)
    low_q = lax.broadcasted_iota(jnp.int32, (gr, LANES), 1) < GLA_DK
    low_s = lax.broadcasted_iota(jnp.int32, (GLA_DV, LANES), 1) < GLA_DK
    nw = nw_ref[...]

    for gi in range(TB_SCAN // gr):
        rows = slice(gi * gr, (gi + 1) * gr)
        b_all = _dot_exact_lhs(tril, la_scr[rows, :])
        for p in range(GLA_HEADS // 2):
            b = b_all[:, p * LANES:(p + 1) * LANES]
            b_last = [b[(ci + 1) * c - 1:(ci + 1) * c, :] for ci in range(GLA_GROUP)]
            b_last_rows = jnp.concatenate([jnp.broadcast_to(bl, (c, LANES)) for bl in b_last], axis=0)
            q = pg_ref[rows, G_Q + p * LANES:G_Q + (p + 1) * LANES] * (GLA_DK ** -0.5)
            k = pg_ref[rows, G_K + p * LANES:G_K + (p + 1) * LANES]
            qe = q * jnp.exp(b)
            ke = (k * jnp.exp(-b)).astype(BF16)
            k_tail = (k * jnp.exp(b_last_rows - b)).astype(BF16)
            v_pair = pg_ref[rows, G_V + 2 * p * GLA_DV:G_V + 2 * (p + 1) * GLA_DV].astype(BF16)
            qm = jnp.concatenate([jnp.where(low_q, qe, 0.0), jnp.where(low_q, 0.0, qe)], axis=0).astype(BF16)
            att = jnp.where(causal2, _dot_nt(qm, ke), 0.0).astype(BF16)
            o_intra = [_dot(att[j * gr:(j + 1) * gr], v_pair[:, j * GLA_DV:(j + 1) * GLA_DV]) for j in range(2)]
            contrib = [_dot_tn(v_pair[ci * c:(ci + 1) * c], k_tail[ci * c:(ci + 1) * c]) for ci in range(GLA_GROUP)]
            st = st_scr[p]
            o_inter = []
            for ci in range(GLA_GROUP):
                q_ci = jnp.concatenate([qm[ci * c:(ci + 1) * c], qm[gr + ci * c:gr + (ci + 1) * c]], axis=0)
                o_inter.append(_dot_nt(q_ci, st.astype(BF16)))
                st = st * jnp.exp(b_last[ci]) + jnp.where(low_s, contrib[ci][:GLA_DV], contrib[ci][GLA_DV:])
            st_scr[p] = st
            for j in range(2):
                h = 2 * p + j
                o = o_intra[j] + jnp.concatenate([oi[j * c:(j + 1) * c] for oi in o_inter], axis=0)
                ms = jnp.mean(o * o, axis=-1, keepdims=True)
                g = pg_ref[rows, G_G + h * GLA_DV:G_G + (h + 1) * GLA_DV]
                o = (o * lax.rsqrt(ms + EPS) * nw) * _silu(g)
                o_ref[rows, h * GLA_DV:(h + 1) * GLA_DV] = o.astype(BF16)


def _gla(pg, a_w, a_b, norm_w):
    nblk = SEQ // TB_SCAN
    return pl.pallas_call(
        _gla_kernel,
        out_shape=jax.ShapeDtypeStruct((TOKENS, GLA_WIDTH), BF16),
        grid=(BATCH, nblk),
        in_specs=[
            pl.BlockSpec((TB_SCAN, GLA_COLS), lambda b, i: (b * nblk + i, 0)),
            pl.BlockSpec((LANES, GLA_QK), lambda b, i: (0, 0)),
            pl.BlockSpec((1, GLA_QK), lambda b, i: (0, 0)),
            pl.BlockSpec((1, GLA_DV), lambda b, i: (0, 0)),
        ],
        out_specs=pl.BlockSpec((TB_SCAN, GLA_WIDTH), lambda b, i: (b * nblk + i, 0)),
        scratch_shapes=[pltpu.VMEM((TB_SCAN, GLA_QK), F32),
                        pltpu.VMEM((GLA_HEADS // 2, GLA_DV, LANES), F32)],
        compiler_params=_params("arbitrary", "arbitrary"),
        name="gla",
    )(pg, a_w, a_b, norm_w)


def _ssd_kernel(ps_ref, cw_ref, cb_ref, dtb_ref, alog_ref, dsk_ref, nw_ref, e64_ref, e128_ref,
                o_ref, xbc_scr, st_scr):
    first = pl.program_id(1) == 0
    halo = SSD_HALO
    c = SSD_CHUNK

    @pl.when(first)
    def _():
        xbc_scr[0:halo, :] = jnp.zeros((halo, SSD_CONV_DIM), F32)
        st_scr[...] = jnp.zeros_like(st_scr)

    @pl.when(jnp.logical_not(first))
    def _():
        xbc_scr[0:halo, :] = xbc_scr[TB_SCAN:TB_SCAN + halo, :]

    xbc_scr[halo:halo + TB_SCAN, :] = ps_ref[:, S_X:S_X + SSD_CONV_DIM]

    row = lax.broadcasted_iota(jnp.int32, (c, c), 0)
    col = lax.broadcasted_iota(jnp.int32, (c, c), 1)
    causal = row >= col
    tril = jnp.where(causal, 1.0, 0.0).astype(BF16)
    head_lane = col < SSD_HEADS
    low_half = col < SSD_HEADDIM
    st_row = lax.broadcasted_iota(jnp.int32, (c, SSD_WIDTH), 0)
    st_col = lax.broadcasted_iota(jnp.int32, (c, SSD_WIDTH), 1)
    blockdiag = (st_row < SSD_STATE) == (st_col < SSD_WIDTH // SSD_GROUPS)
    a_neg = -jnp.exp(alog_ref[...])
    heads_per_group = SSD_HEADS // SSD_GROUPS

    for ci in range(TB_SCAN // c):
        rows = slice(ci * c, (ci + 1) * c)
        win = xbc_scr[ci * c:(ci + 1) * c + halo, :]
        conv = cw_ref[0:1, :] * win
        for k in range(1, SSD_CONV):
            conv = pltpu.roll(conv, 1, axis=0) + cw_ref[k:k + 1, :] * win
        act = _silu(conv[halo:, :] + cb_ref[...])
        dt = jnp.where(head_lane, _softplus(ps_ref[rows, S_DT:S_DT + LANES] + dtb_ref[...]), 0.0)
        cum = _dot_exact_lhs(tril, dt * a_neg)
        cum_t = cum.T
        cum64 = _dot_exact_rhs(cum, e64_ref[...])
        dt64 = _dot_exact_rhs(dt, e64_ref[...])
        cum_col = _dot_exact_rhs(cum, e128_ref[...])
        cl64 = cum64[c - 1:c, :]
        xs = act[:, 0:SSD_WIDTH]
        bm = act[:, SSD_WIDTH:SSD_WIDTH + SSD_BC].astype(BF16)
        cm = act[:, SSD_WIDTH + SSD_BC:SSD_CONV_DIM]
        xdt = xs * dt64
        xdt_bf = xdt.astype(BF16)
        y_parts = []
        for g in range(SSD_GROUPS):
            cm_g = jnp.where(low_half if g == 0 else jnp.logical_not(low_half), cm, 0.0).astype(BF16)
            scores = _dot_nt(cm_g, bm)
            for pp in range(heads_per_group // 2):
                p = g * (heads_per_group // 2) + pp
                ys = []
                for j in range(2):
                    h = 2 * p + j
                    seg = cum_col[:, h * LANES:(h + 1) * LANES] - cum_t[h:h + 1, :]
                    decay = jnp.exp(jnp.where(causal, seg, -jnp.inf))
                    ys.append(_dot((scores * decay).astype(BF16), xdt_bf[:, p * LANES:(p + 1) * LANES]))
                y_parts.append(jnp.where(low_half, ys[0], ys[1]))
        y = jnp.concatenate(y_parts, axis=1)
        st = st_scr[...]
        y = y + _dot(cm.astype(BF16), st.astype(BF16)) * jnp.exp(cum64)
        contrib = _dot_tn(bm, (xdt * jnp.exp(cl64 - cum64)).astype(BF16))
        st_scr[...] = st * jnp.exp(cl64) + jnp.where(blockdiag, contrib, 0.0)
        y = y + dsk_ref[...] * xs
        y = y * _silu(ps_ref[rows, S_Z:S_Z + SSD_WIDTH])
        ms = jnp.mean(y * y, axis=-1, keepdims=True)
        o_ref[rows, :] = (y * lax.rsqrt(ms + EPS) * nw_ref[...]).astype(BF16)


def _ssd(ps, conv_w, conv_b, dt_bias, a_log, d_skip, norm_w, e64, e128):
    nblk = SEQ // TB_SCAN
    const = lambda b, i: (0, 0)
    return pl.pallas_call(
        _ssd_kernel,
        out_shape=jax.ShapeDtypeStruct((TOKENS, SSD_WIDTH), BF16),
        grid=(BATCH, nblk),
        in_specs=[
            pl.BlockSpec((TB_SCAN, SSD_COLS), lambda b, i: (b * nblk + i, 0)),
            pl.BlockSpec((SSD_CONV, SSD_CONV_DIM), const),
            pl.BlockSpec((1, SSD_CONV_DIM), const),
            pl.BlockSpec((1, LANES), const),
            pl.BlockSpec((1, LANES), const),
            pl.BlockSpec((1, SSD_WIDTH), const),
            pl.BlockSpec((1, SSD_WIDTH), const),
            pl.BlockSpec((LANES, SSD_WIDTH), const),
            pl.BlockSpec((LANES, SSD_HEADS * LANES), const),
        ],
        out_specs=pl.BlockSpec((TB_SCAN, SSD_WIDTH), lambda b, i: (b * nblk + i, 0)),
        scratch_shapes=[pltpu.VMEM((TB_SCAN + SSD_HALO, SSD_CONV_DIM), F32),
                        pltpu.VMEM((SSD_BC, SSD_WIDTH), F32)],
        compiler_params=_params("arbitrary", "arbitrary"),
        name="ssd",
    )(ps, conv_w, conv_b, dt_bias, a_log, d_skip, norm_w, e64, e128)


def _outproj_kernel(x_ref, og_ref, oy_ref, wo_ref, mod_ref, nw_ref, *rest, with_router):
    if with_router:
        rw_ref, xo_ref, h_ref, gates_ref, sel1_ref, sel2_ref, cnt_ref = rest
    else:
        xo_ref, h_ref = rest
    mod = mod_ref[0, 0]
    half = TM_PROJ // 2
    counts = jnp.zeros((1, LANES), F32)
    for r0 in (0, half):
        rows = slice(r0, r0 + half)
        mix = (_dot(og_ref[rows, :], wo_ref[0, 0:GLA_WIDTH, :])
               + _dot(oy_ref[rows, :], wo_ref[0, GLA_WIDTH:, :]))
        x_new = x_ref[rows, :] + mod[2:3] * mix
        xo_ref[rows, :] = x_new
        h = _norm_mod(x_new, nw_ref[...], mod[4:5], mod[3:4])
        h_ref[rows, :] = h.astype(BF16)
        if with_router:
            logits = _dot3(h, rw_ref[...])
            lane = lax.broadcasted_iota(jnp.int32, logits.shape, 1)
            lg = jnp.where(lane < N_EXPERTS, logits, -jnp.inf)
            v1 = jnp.max(lg, axis=-1, keepdims=True)
            i1 = jnp.min(jnp.where(lg == v1, lane, LANES), axis=-1, keepdims=True)
            lg2 = jnp.where(lane == i1, -jnp.inf, lg)
            v2 = jnp.max(lg2, axis=-1, keepdims=True)
            i2 = jnp.min(jnp.where(lg2 == v2, lane, LANES), axis=-1, keepdims=True)
            e2 = jnp.exp(v2 - v1)
            p1 = 1.0 / (1.0 + e2)
            p2 = e2 / (1.0 + e2)
            sel1 = jnp.where(lane == i1, 1.0, 0.0)
            sel2 = jnp.where(lane == i2, 1.0, 0.0)
            gates_ref[rows, :] = sel1 * p1 + sel2 * p2
            sel1_ref[rows, :] = sel1
            sel2_ref[rows, :] = sel2
            counts = counts + jnp.sum(sel1 + sel2, axis=0, keepdims=True)
    if with_router:
        cnt_ref[0] = jnp.broadcast_to(counts.astype(jnp.int32), (8, LANES))


def _outproj(x, og, oy, w_out, layer, mod_l, norm_w, router_w):
    tiles_per_batch = SEQ // TM_PROJ
    with_router = router_w is not None
    tile = lambda i: (i, 0)
    const = lambda i: (0, 0)
    in_specs = [
        pl.BlockSpec((TM_PROJ, D_MODEL), tile),
        pl.BlockSpec((TM_PROJ, GLA_WIDTH), tile),
        pl.BlockSpec((TM_PROJ, SSD_WIDTH), tile),
        pl.BlockSpec((1, D_MODEL, D_MODEL), lambda i: (layer, 0, 0)),
        pl.BlockSpec((1, 1, 6, D_MODEL), lambda i: (i // tiles_per_batch, 0, 0, 0)),
        pl.BlockSpec((1, D_MODEL), const),
    ]
    out_shape = [jax.ShapeDtypeStruct((TOKENS, D_MODEL), F32),
                 jax.ShapeDtypeStruct((TOKENS, D_MODEL), BF16)]
    out_specs = [pl.BlockSpec((TM_PROJ, D_MODEL), tile), pl.BlockSpec((TM_PROJ, D_MODEL), tile)]
    args = [x, og, oy, w_out, mod_l, norm_w]
    if with_router:
        in_specs.append(pl.BlockSpec((D_MODEL, LANES), const))
        for _ in range(3):
            out_shape.append(jax.ShapeDtypeStruct((TOKENS, LANES), F32))
            out_specs.append(pl.BlockSpec((TM_PROJ, LANES), tile))
        out_shape.append(jax.ShapeDtypeStruct((TOKENS // TM_PROJ, 8, LANES), jnp.int32))
        out_specs.append(pl.BlockSpec((1, 8, LANES), lambda i: (i, 0, 0)))
        args.append(router_w)
    return pl.pallas_call(
        functools.partial(_outproj_kernel, with_router=with_router),
        out_shape=tuple(out_shape),
        grid=(TOKENS // TM_PROJ,),
        in_specs=in_specs,
        out_specs=tuple(out_specs),
        compiler_params=_params("arbitrary"),
        name="outproj_router" if with_router else "outproj",
    )(*args)


def _swiglu_chunk(h, wg, wu, wd):
    a = (_silu(_dot(h, wg.astype(BF16))) * _dot(h, wu.astype(BF16))).astype(BF16)
    return _dot(a, wd.astype(BF16))


def _ffn_kernel(h_ref, x_ref, mod_ref, wg_ref, wu_ref, wd_ref, o_ref):
    @pl.when(pl.program_id(1) == 0)
    def _():
        o_ref[...] = x_ref[...]

    o_ref[...] += mod_ref[0, 0, 5:6, :] * _swiglu_chunk(h_ref[...], wg_ref[0], wu_ref[0], wd_ref[0])


def _ffn(h, x, mod_l, w_gate, w_up, w_down, layer_idx):
    tiles_per_batch = SEQ // TM_FFN
    tile = lambda i, j: (i, 0)
    return pl.pallas_call(
        _ffn_kernel,
        out_shape=jax.ShapeDtypeStruct((TOKENS, D_MODEL), F32),
        grid=(TOKENS // TM_FFN, N_FF_CHUNKS),
        in_specs=[pl.BlockSpec((TM_FFN, D_MODEL), tile),
                  pl.BlockSpec((TM_FFN, D_MODEL), tile),
                  pl.BlockSpec((1, 1, 6, D_MODEL), lambda i, j: (i // tiles_per_batch, 0, 0, 0)),
                  pl.BlockSpec((1, D_MODEL, TF_FFN), lambda i, j: (layer_idx, 0, j)),
                  pl.BlockSpec((1, D_MODEL, TF_FFN), lambda i, j: (layer_idx, 0, j)),
                  pl.BlockSpec((1, TF_FFN, D_MODEL), lambda i, j: (layer_idx, j, 0))],
        out_specs=pl.BlockSpec((TM_FFN, D_MODEL), tile),
        compiler_params=_params("arbitrary", "arbitrary"),
        name="dense_ffn",
    )(h, x, mod_l, w_gate, w_up, w_down)


def _moe_plan(cnt):
    seg_len = (cnt + SEG_ALIGN - 1) // SEG_ALIGN * SEG_ALIGN
    loc_off = jnp.cumsum(seg_len, axis=1) - seg_len
    n_rows = seg_len.sum(axis=0)
    region = (n_rows + MOE_TILE - 1) // MOE_TILE * MOE_TILE
    base = jnp.cumsum(region) - region
    seg_start = base[None, :] + jnp.cumsum(seg_len, axis=0) - seg_len
    tiles_e = region // MOE_TILE
    tile_end = jnp.cumsum(tiles_e)
    n_act = tile_end[-1]
    r = jnp.arange(MOE_MAX_TILES, dtype=jnp.int32)
    r_act = jnp.minimum(r, n_act - 1)
    tile_exp = jnp.sum(r_act[:, None] >= tile_end[None, :], axis=1).astype(jnp.int32)
    rows_left = n_rows[tile_exp] - (r_act - (tile_end - tiles_e)[tile_exp]) * MOE_TILE
    n_sub = jnp.clip((rows_left + MOE_SUB - 1) // MOE_SUB, 0, MOE_TILE // MOE_SUB)
    n_sub = jnp.where(r < n_act, n_sub, 0)
    i32 = lambda a: a.reshape(-1).astype(jnp.int32)
    return dict(seg=(i32(seg_start), i32(loc_off), i32(seg_len)),
                fill=(i32(base + n_rows), i32(region - n_rows), i32(n_act)),
                tiles=(tile_exp, i32(r_act), i32(n_sub)))


def _segment_dma(src, dst, src_off, dst_off, length, sem, wait, advance_src=True):
    off = jnp.int32(0)
    for size in SEG_PIECES:
        take = (length & size) != 0
        s0 = pl.multiple_of(src_off + off, SEG_ALIGN) if advance_src else src_off
        d0 = pl.multiple_of(dst_off + off, SEG_ALIGN)

        @pl.when(take)
        def _():
            cp = pltpu.make_async_copy(src.at[pl.ds(s0, size)], dst.at[pl.ds(d0, size)], sem)
            if wait:
                cp.wait()
            else:
                cp.start()

        off = off + jnp.where(take, size, 0)


def _local_rows(i, lo_ref, sel1, sel2):
    tm = sel1.shape[0]
    tr = lax.broadcasted_iota(jnp.int32, (tm, tm), 0)
    tc = lax.broadcasted_iota(jnp.int32, (tm, tm), 1)
    earlier = jnp.where(tr > tc, 1.0, 0.0).astype(BF16)
    lane = lax.broadcasted_iota(jnp.int32, (1, LANES), 1)
    loc = jnp.zeros((1, LANES), F32)
    for e in range(N_EXPERTS):
        loc = jnp.where(lane == e, lo_ref[i * N_EXPERTS + e].astype(F32), loc)
    pos = _dot(earlier, (sel1 + sel2).astype(BF16)) + loc
    return jnp.sum(sel1 * pos, axis=-1, keepdims=True), jnp.sum(sel2 * pos, axis=-1, keepdims=True)


def _moe_sort_kernel(ss_ref, lo_ref, ln_ref, fs_ref, fl_ref, na_ref, h_ref, sel1_ref, sel2_ref, xs_ref,
                     loc_scr, zero_scr, sem, fill_sem):
    i = pl.program_id(0)
    last = pl.num_programs(0) - 1
    slot = lax.rem(i, 2)
    pos1, pos2 = _local_rows(i, lo_ref, sel1_ref[...], sel2_ref[...])
    cid = lax.broadcasted_iota(jnp.int32, (TM_PROJ, MOE_LOC_ROWS), 1).astype(F32)
    pick = jnp.where(cid == pos1, 1.0, jnp.where(cid == pos2, 1.0, 0.0)).astype(BF16)
    loc_scr[slot] = _dot_tn(pick, h_ref[...]).astype(BF16)

    def copies(tile, buf, wait):
        for e in range(N_EXPERTS):
            k = tile * N_EXPERTS + e
            _segment_dma(loc_scr.at[buf], xs_ref, lo_ref[k], ss_ref[k], ln_ref[k], sem.at[buf], wait)

    copies(i, slot, False)

    @pl.when(i > 0)
    def _():
        copies(i - 1, 1 - slot, True)

    @pl.when(i == last)
    def _():
        copies(i, slot, True)
        zero_scr[...] = jnp.zeros_like(zero_scr)
        n_act = na_ref[0]

        def fill_tiles(wait):
            def body(r, carry):
                for part in range(MOE_TILE // MOE_SUB):
                    d0 = pl.multiple_of(r * MOE_TILE + part * MOE_SUB, MOE_SUB)
                    cp = pltpu.make_async_copy(zero_scr, xs_ref.at[pl.ds(d0, MOE_SUB)], fill_sem)
                    if wait:
                        cp.wait()
                    else:
                        cp.start()
                return carry
            lax.fori_loop(n_act, MOE_MAX_TILES, body, 0)

        for wait in (False, True):
            for e in range(N_EXPERTS):
                _segment_dma(zero_scr, xs_ref, 0, fs_ref[e], fl_ref[e], fill_sem, wait, advance_src=False)
            fill_tiles(wait)


def _moe_sort(plan, h, sel1, sel2):
    tile = lambda i, *_: (i, 0)
    return pl.pallas_call(
        _moe_sort_kernel,
        out_shape=jax.ShapeDtypeStruct((MOE_MAX_TILES * MOE_TILE, D_MODEL), BF16),
        grid_spec=pltpu.PrefetchScalarGridSpec(
            num_scalar_prefetch=6,
            grid=(TOKENS // TM_PROJ,),
            in_specs=[pl.BlockSpec((TM_PROJ, D_MODEL), tile),
                      pl.BlockSpec((TM_PROJ, LANES), tile),
                      pl.BlockSpec((TM_PROJ, LANES), tile)],
            out_specs=pl.BlockSpec(memory_space=pl.ANY),
            scratch_shapes=[pltpu.VMEM((2, MOE_LOC_ROWS, D_MODEL), BF16),
                            pltpu.VMEM((MOE_SUB, D_MODEL), BF16),
                            pltpu.SemaphoreType.DMA((2,)),
                            pltpu.SemaphoreType.DMA],
        ),
        compiler_params=_params("arbitrary"),
        name="moe_sort",
    )(*plan["seg"], *plan["fill"], h, sel1, sel2)


def _moe_ffn_kernel(te_ref, ra_ref, ns_ref, xs_ref, wg_ref, wu_ref, wd_ref, o_ref, acc_ref):
    del te_ref, ra_ref
    r, j = pl.program_id(0), pl.program_id(1)
    n_sub = ns_ref[r]
    n_parts = MOE_TILE // MOE_SUB
    for used in range(1, n_parts + 1):
        rows = slice(0, used * MOE_SUB)

        @pl.when(n_sub == used)
        def _():
            @pl.when(j == 0)
            def _():
                acc_ref[rows, :] = jnp.zeros((used * MOE_SUB, D_MODEL), F32)

            acc_ref[rows, :] += _swiglu_chunk(xs_ref[rows, :], wg_ref[0, 0], wu_ref[0, 0], wd_ref[0, 0])

            @pl.when(j == N_FF_MOE - 1)
            def _():
                o_ref[rows, :] = acc_ref[rows, :].astype(BF16)

    for s in range(n_parts):
        @pl.when(jnp.logical_and(s >= n_sub, j == N_FF_MOE - 1))
        def _():
            o_ref[s * MOE_SUB:(s + 1) * MOE_SUB, :] = jnp.zeros((MOE_SUB, D_MODEL), BF16)


def _moe_ffn(plan, xs, w_gate, w_up, w_down, layer_idx):
    last = N_FF_MOE - 1
    rows = lambda r, j, te, ra, ns: (ra[r], 0)
    chunk = lambda r, j, ns: jnp.where(ns[r] > 0, j, last)
    return pl.pallas_call(
        _moe_ffn_kernel,
        out_shape=jax.ShapeDtypeStruct(xs.shape, BF16),
        grid_spec=pltpu.PrefetchScalarGridSpec(
            num_scalar_prefetch=3,
            grid=(MOE_MAX_TILES, N_FF_MOE),
            in_specs=[pl.BlockSpec((MOE_TILE, D_MODEL), rows),
                      pl.BlockSpec((1, 1, D_MODEL, TF_MOE),
                                   lambda r, j, te, ra, ns: (layer_idx, te[r], 0, chunk(r, j, ns))),
                      pl.BlockSpec((1, 1, D_MODEL, TF_MOE),
                                   lambda r, j, te, ra, ns: (layer_idx, te[r], 0, chunk(r, j, ns))),
                      pl.BlockSpec((1, 1, TF_MOE, D_MODEL),
                                   lambda r, j, te, ra, ns: (layer_idx, te[r], chunk(r, j, ns), 0))],
            out_specs=pl.BlockSpec((MOE_TILE, D_MODEL), lambda r, j, *_: (r, 0)),
            scratch_shapes=[pltpu.VMEM((MOE_TILE, D_MODEL), F32)],
        ),
        compiler_params=_params("arbitrary", "arbitrary"),
        name="moe_ffn",
    )(*plan["tiles"], xs, w_gate, w_up, w_down)


def _moe_combine_kernel(ss_ref, lo_ref, ln_ref, x_ref, mod_ref, gates_ref, sel1_ref, sel2_ref, *rest,
                        final_norm):
    if final_norm:
        fw_ref, ys_ref, o_ref, loc_scr, sem = rest
    else:
        ys_ref, o_ref, loc_scr, sem = rest
    i = pl.program_id(0)
    tm = TM_PROJ
    slot = lax.rem(i, 2)

    def fetch(tile, buf, wait):
        for e in range(N_EXPERTS):
            k = tile * N_EXPERTS + e
            _segment_dma(ys_ref, loc_scr.at[buf], ss_ref[k], lo_ref[k], ln_ref[k], sem.at[buf], wait)

    @pl.when(i == 0)
    def _():
        loc_scr[...] = jnp.zeros_like(loc_scr)
        fetch(0, 0, False)

    @pl.when(i + 1 < pl.num_programs(0))
    def _():
        fetch(i + 1, 1 - slot, False)

    sel1, sel2, gates = sel1_ref[...], sel2_ref[...], gates_ref[...]
    pos1, pos2 = _local_rows(i, lo_ref, sel1, sel2)
    p1 = jnp.sum(sel1 * gates, axis=-1, keepdims=True)
    p2 = jnp.sum(sel2 * gates, axis=-1, keepdims=True)
    cid = lax.broadcasted_iota(jnp.int32, (tm, MOE_LOC_ROWS), 1).astype(F32)
    pick1 = jnp.where(cid == pos1, 1.0, 0.0).astype(BF16)
    pick2 = jnp.where(cid == pos2, 1.0, 0.0).astype(BF16)
    fetch(i, slot, True)
    y = loc_scr[slot]
    ff = p1 * _dot(pick1, y) + p2 * _dot(pick2, y)
    out = x_ref[...] + mod_ref[0, 0, 5:6, :] * ff
    if final_norm:
        ms = jnp.mean(out * out, axis=-1, keepdims=True)
        out = out * lax.rsqrt(ms + EPS) * fw_ref[...]
    o_ref[...] = out


def _moe_combine(plan, x, mod_l, gates, sel1, sel2, ys, final_w=None):
    tiles_per_batch = SEQ // TM_PROJ
    tile = lambda i, *_: (i, 0)
    final_norm = final_w is not None
    in_specs = [pl.BlockSpec((TM_PROJ, D_MODEL), tile),
                pl.BlockSpec((1, 1, 6, D_MODEL), lambda i, *_: (i // tiles_per_batch, 0, 0, 0)),
                pl.BlockSpec((TM_PROJ, LANES), tile),
                pl.BlockSpec((TM_PROJ, LANES), tile),
                pl.BlockSpec((TM_PROJ, LANES), tile)]
    args = [x, mod_l, gates, sel1, sel2]
    if final_norm:
        in_specs.append(pl.BlockSpec((1, D_MODEL), lambda i, *_: (0, 0)))
        args.append(final_w)
    return pl.pallas_call(
        functools.partial(_moe_combine_kernel, final_norm=final_norm),
        out_shape=jax.ShapeDtypeStruct((TOKENS, D_MODEL), F32),
        grid_spec=pltpu.PrefetchScalarGridSpec(
            num_scalar_prefetch=3,
            grid=(TOKENS // TM_PROJ,),
            in_specs=in_specs + [pl.BlockSpec(memory_space=pl.ANY)],
            out_specs=pl.BlockSpec((TM_PROJ, D_MODEL), tile),
            scratch_shapes=[pltpu.VMEM((2, MOE_LOC_ROWS, D_MODEL), BF16), pltpu.SemaphoreType.DMA((2,))],
        ),
        compiler_params=_params("arbitrary"),
        name="moe_combine_norm" if final_norm else "moe_combine",
    )(*plan["seg"], *args, ys)


def _final_norm_kernel(x_ref, w_ref, o_ref):
    x = x_ref[...]
    ms = jnp.mean(x * x, axis=-1, keepdims=True)
    o_ref[...] = x * lax.rsqrt(ms + EPS) * w_ref[...]


def _final_norm(x, w):
    return pl.pallas_call(
        _final_norm_kernel,
        out_shape=jax.ShapeDtypeStruct((TOKENS, D_MODEL), F32),
        grid=(TOKENS // TM_FFN,),
        in_specs=[pl.BlockSpec((TM_FFN, D_MODEL), lambda i: (i, 0)),
                  pl.BlockSpec((1, D_MODEL), lambda i: (0, 0))],
        out_specs=pl.BlockSpec((TM_FFN, D_MODEL), lambda i: (i, 0)),
        compiler_params=_params("arbitrary"),
        name="final_norm",
    )(x, w)


def _pad_cols(a, width):
    return jnp.pad(a, [(0, 0)] * (a.ndim - 1) + [(0, width - a.shape[-1])])


def _split_w_in(w_in):
    ssd_start = G_A + GLA_LOWRANK
    return w_in[:, :, :GLA_COLS].astype(BF16), _pad_cols(w_in[:, :, ssd_start:], SSD_COLS).astype(BF16)


def kernel(x, c, ada_w, ada_b, norm1_w, w_in, gla_a_w, gla_a_b, gla_norm_w, conv_w, conv_b, dt_bias, a_log,
           d_skip, ssd_norm_w, w_out, norm2_w, ffn_w_gate, ffn_w_up, ffn_w_down, router_w, moe_w_gate,
           moe_w_up, moe_w_down, final_norm_w):
    xt = x.reshape(TOKENS, D_MODEL)
    c_pad = jnp.pad(c, ((0, 8 - BATCH), (0, 0)))
    mod = _adaln(c_pad, ada_w, ada_b)[:, :BATCH].reshape(DEPTH, BATCH, 6, D_MODEL)

    w_gla, w_ssd = _split_w_in(w_in)
    w_out_bf = w_out.astype(BF16)
    a_w_p = jnp.pad(gla_a_w, ((0, 0), (0, LANES - GLA_LOWRANK), (0, 0)))
    dtb_p = _pad_cols(dt_bias, LANES)
    alog_p = _pad_cols(a_log, LANES)
    dsk_p = jnp.repeat(d_skip, SSD_HEADDIM, axis=-1)
    rw_p = _pad_cols(router_w, LANES)
    lane_head = jnp.arange(LANES)[:, None]
    e64 = (lane_head == jnp.arange(SSD_WIDTH)[None, :] // SSD_HEADDIM).astype(BF16)
    e128 = (lane_head == jnp.arange(SSD_HEADS * LANES)[None, :] // LANES).astype(BF16)

    for l in range(DEPTH):
        mod_l = mod[l].reshape(BATCH, 1, 6, D_MODEL)
        pg, ps = _inproj(xt, mod_l, norm1_w[l][None], w_gla, w_ssd, l)
        og = _gla(pg, a_w_p[l], gla_a_b[l][None], gla_norm_w[l][None])
        oy = _ssd(ps, conv_w[l], conv_b[l][None], dtb_p[l][None], alog_p[l][None], dsk_p[l][None],
                  ssd_norm_w[l][None], e64, e128)
        i = l // 2
        if l % 2 == 0:
            xt, h2 = _outproj(xt, og, oy, w_out_bf, l, mod_l, norm2_w[l][None], None)
            xt = _ffn(h2, xt, mod_l, ffn_w_gate, ffn_w_up, ffn_w_down, i)
        else:
            xt, h2, gates, sel1, sel2, cnt = _outproj(xt, og, oy, w_out_bf, l, mod_l, norm2_w[l][None], rw_p[i])
            plan = _moe_plan(cnt[:, 0, :N_EXPERTS])
            xs = _moe_sort(plan, h2, sel1, sel2)
            ys = _moe_ffn(plan, xs, moe_w_gate, moe_w_up, moe_w_down, i)
            last = l == DEPTH - 1
            xt = _moe_combine(plan, xt, mod_l, gates, sel1, sel2, ys, final_norm_w[None] if last else None)
    if DEPTH % 2:
        xt = _final_norm(xt, final_norm_w[None])
    return xt.reshape(BATCH, SEQ, D_MODEL)
```

```python
import functools

import jax
import jax.numpy as jnp
from jax import lax
from jax.experimental import pallas as pl
from jax.experimental.pallas import tpu as pltpu

D_MODEL = 1024
BATCH = 2
SEQ = 8192
DEPTH = 4
TOKENS = BATCH * SEQ

GLA_HEADS = 4
GLA_DK = 64
GLA_DV = 128
GLA_QK = GLA_HEADS * GLA_DK
GLA_WIDTH = GLA_HEADS * GLA_DV
GLA_LOWRANK = 16
GLA_TAU = 16.0
GLA_CHUNK = 64

SSD_HEADS = 8
SSD_HEADDIM = 64
SSD_WIDTH = SSD_HEADS * SSD_HEADDIM
SSD_GROUPS = 2
SSD_STATE = 64
SSD_BC = SSD_GROUPS * SSD_STATE
SSD_CONV = 4
SSD_CONV_DIM = SSD_WIDTH + 2 * SSD_BC
SSD_CHUNK = 128

D_FF = 3584
N_EXPERTS = 8
EPS = 1e-6

LANES = 128
GLA_COLS = 2 * GLA_QK + 2 * GLA_WIDTH + LANES
SSD_COLS = 2 * SSD_WIDTH + 2 * SSD_BC + LANES
G_Q, G_K, G_V, G_G, G_A = 0, GLA_QK, 2 * GLA_QK, 2 * GLA_QK + GLA_WIDTH, 2 * GLA_QK + 2 * GLA_WIDTH
S_Z, S_X, S_DT = 0, SSD_WIDTH, SSD_WIDTH + SSD_CONV_DIM

TM_PROJ = 512
TB_SCAN = 512
TM_FFN = 1024
TF_FFN = 512
GLA_GROUP = 4
SSD_HALO = 8
N_FF_CHUNKS = D_FF // TF_FFN
VMEM_LIMIT = 56 * 1024 * 1024

TOP_K = 2
SEG_ALIGN = 16
SEG_PIECES = tuple(TM_PROJ >> s for s in range(6))
MOE_TILE = 1024
MOE_SUB = 256
FILL_ROWS = SEG_PIECES[0]
TF_MOE = TF_FFN
N_FF_MOE = D_FF // TF_MOE
N_TILES = TOKENS // TM_PROJ
MOE_LOC_ROWS = -(-(TOP_K * TM_PROJ + N_EXPERTS * (SEG_ALIGN - 1)) // LANES) * LANES
MOE_MAX_TILES = (TOP_K * TOKENS + N_TILES * N_EXPERTS * (SEG_ALIGN - 1) + N_EXPERTS * (MOE_TILE - 1)) // MOE_TILE

F32 = jnp.float32
BF16 = jnp.bfloat16


def _dot(a, b):
    return jnp.dot(a, b, preferred_element_type=F32)


def _dot_nt(a, b):
    return lax.dot_general(a, b, (((1,), (1,)), ((), ())), preferred_element_type=F32)


def _dot_tn(a, b):
    return lax.dot_general(a, b, (((0,), (0,)), ((), ())), preferred_element_type=F32)


def _split(a):
    hi = a.astype(BF16)
    lo = (a - hi.astype(F32)).astype(BF16)
    return hi, lo


def _dot3(a, b):
    a_hi, a_lo = _split(a)
    b_hi, b_lo = _split(b)
    return _dot(a_hi, b_hi) + _dot(a_lo, b_hi) + _dot(a_hi, b_lo)


def _dot_exact_rhs(a, b_bf16):
    a_hi, a_lo = _split(a)
    return _dot(a_hi, b_bf16) + _dot(a_lo, b_bf16)


def _dot_exact_lhs(a_bf16, b):
    b_hi, b_lo = _split(b)
    return _dot(a_bf16, b_hi) + _dot(a_bf16, b_lo)


def _silu(x):
    return x * (0.5 * jnp.tanh(0.5 * x) + 0.5)


def _softplus(x):
    return jnp.maximum(x, 0.0) + jnp.log1p(jnp.exp(-jnp.abs(x)))


def _norm_mod(x, w, scale, shift):
    ms = jnp.mean(x * x, axis=-1, keepdims=True)
    return (x * lax.rsqrt(ms + EPS) * w) * (1.0 + scale) + shift


def _params(*sem):
    return pltpu.CompilerParams(dimension_semantics=sem, vmem_limit_bytes=VMEM_LIMIT)


def _adaln_kernel(c_ref, w_ref, b_ref, o_ref):
    s = _silu(c_ref[...])
    o_ref[0] = _dot3(s, w_ref[0]) + b_ref[0]


def _adaln(c_pad, ada_w, ada_b):
    n_col = 6 * D_MODEL // D_MODEL
    return pl.pallas_call(
        _adaln_kernel,
        out_shape=jax.ShapeDtypeStruct((DEPTH, 8, 6 * D_MODEL), F32),
        grid=(DEPTH, n_col),
        in_specs=[
            pl.BlockSpec((8, D_MODEL), lambda l, j: (0, 0)),
            pl.BlockSpec((1, D_MODEL, D_MODEL), lambda l, j: (l, 0, j)),
            pl.BlockSpec((1, 1, D_MODEL), lambda l, j: (l, 0, j)),
        ],
        out_specs=pl.BlockSpec((1, 8, D_MODEL), lambda l, j: (l, 0, j)),
        compiler_params=_params("arbitrary", "arbitrary"),
        name="adaln",
    )(c_pad, ada_w, ada_b.reshape(DEPTH, 1, 6 * D_MODEL))


def _inproj_kernel(x_ref, mod_ref, nw_ref, wg_ref, ws_ref, og_ref, os_ref):
    mod = mod_ref[0, 0]
    half = TM_PROJ // 2
    for r0 in (0, half):
        rows = slice(r0, r0 + half)
        h = _norm_mod(x_ref[rows, :], nw_ref[...], mod[1:2], mod[0:1]).astype(BF16)
        for w_ref, o_ref in ((wg_ref, og_ref), (ws_ref, os_ref)):
            n_cols = o_ref.shape[1]
            for c0 in range(0, n_cols, 512):
                c1 = min(c0 + 512, n_cols)
                o_ref[rows, c0:c1] = _dot(h, w_ref[0, :, c0:c1])


def _inproj(x, mod_l, norm_w, w_gla, w_ssd, layer):
    tiles_per_batch = SEQ // TM_PROJ
    return pl.pallas_call(
        _inproj_kernel,
        out_shape=(jax.ShapeDtypeStruct((TOKENS, GLA_COLS), F32),
                   jax.ShapeDtypeStruct((TOKENS, SSD_COLS), F32)),
        grid=(TOKENS // TM_PROJ,),
        in_specs=[
            pl.BlockSpec((TM_PROJ, D_MODEL), lambda i: (i, 0)),
            pl.BlockSpec((1, 1, 6, D_MODEL), lambda i: (i // tiles_per_batch, 0, 0, 0)),
            pl.BlockSpec((1, D_MODEL), lambda i: (0, 0)),
            pl.BlockSpec((1, D_MODEL, GLA_COLS), lambda i: (layer, 0, 0)),
            pl.BlockSpec((1, D_MODEL, SSD_COLS), lambda i: (layer, 0, 0)),
        ],
        out_specs=(pl.BlockSpec((TM_PROJ, GLA_COLS), lambda i: (i, 0)),
                   pl.BlockSpec((TM_PROJ, SSD_COLS), lambda i: (i, 0))),
        compiler_params=_params("arbitrary"),
        name="inproj",
    )(x, mod_l, norm_w, w_gla, w_ssd)


def _gla_kernel(pg_ref, aw_ref, ab_ref, nw_ref, o_ref, la_scr, st_scr):
    @pl.when(pl.program_id(1) == 0)
    def _():
        st_scr[...] = jnp.zeros_like(st_scr)

    pre = _dot3(pg_ref[:, G_A:G_A + LANES], aw_ref[...]) + ab_ref[...]
    la_scr[...] = -_softplus(-pre) * (1.0 / GLA_TAU)

    c, gr = GLA_CHUNK, GLA_GROUP * GLA_CHUNK
    row = lax.broadcasted_iota(jnp.int32, (gr, gr), 0)
    col = lax.broadcasted_iota(jnp.int32, (gr, gr), 1)
    causal = jnp.logical_and(row >= col, jnp.bitwise_xor(row, col) < c)
    tril = jnp.where(causal, 1.0, 0.0).astype(BF16)
    causal2 = jnp.concatenate([causal, causal], axis=0)
    low_q = lax.broadcasted_iota(jnp.int32, (gr, LANES), 1) < GLA_DK
    low_s = lax.broadcasted_iota(jnp.int32, (GLA_DV, LANES), 1) < GLA_DK
    nw = nw_ref[...]

    for gi in range(TB_SCAN // gr):
        rows = slice(gi * gr, (gi + 1) * gr)
        b_all = _dot_exact_lhs(tril, la_scr[rows, :])
        for p in range(GLA_HEADS // 2):
            b = b_all[:, p * LANES:(p + 1) * LANES]
            b_last = [b[(ci + 1) * c - 1:(ci + 1) * c, :] for ci in range(GLA_GROUP)]
            b_last_rows = jnp.concatenate([jnp.broadcast_to(bl, (c, LANES)) for bl in b_last], axis=0)
            q = pg_ref[rows, G_Q + p * LANES:G_Q + (p + 1) * LANES] * (GLA_DK ** -0.5)
            k = pg_ref[rows, G_K + p * LANES:G_K + (p + 1) * LANES]
            qe = q * jnp.exp(b)
            ke = (k * jnp.exp(-b)).astype(BF16)
            k_tail = (k * jnp.exp(b_last_rows - b)).astype(BF16)
            v_pair = pg_ref[rows, G_V + 2 * p * GLA_DV:G_V + 2 * (p + 1) * GLA_DV].astype(BF16)
            qm = jnp.concatenate([jnp.where(low_q, qe, 0.0), jnp.where(low_q, 0.0, qe)], axis=0).astype(BF16)
            att = jnp.where(causal2, _dot_nt(qm, ke), 0.0).astype(BF16)
            o_intra = [_dot(att[j * gr:(j + 1) * gr], v_pair[:, j * GLA_DV:(j + 1) * GLA_DV]) for j in range(2)]
            contrib = [_dot_tn(v_pair[ci * c:(ci + 1) * c], k_tail[ci * c:(ci + 1) * c]) for ci in range(GLA_GROUP)]
            st = st_scr[p]
            o_inter = []
            for ci in range(GLA_GROUP):
                q_ci = jnp.concatenate([qm[ci * c:(ci + 1) * c], qm[gr + ci * c:gr + (ci + 1) * c]], axis=0)
                o_inter.append(_dot_nt(q_ci, st.astype(BF16)))
                st = st * jnp.exp(b_last[ci]) + jnp.where(low_s, contrib[ci][:GLA_DV], contrib[ci][GLA_DV:])
            st_scr[p] = st
            for j in range(2):
                h = 2 * p + j
                o = o_intra[j] + jnp.concatenate([oi[j * c:(j + 1) * c] for oi in o_inter], axis=0)
                ms = jnp.mean(o * o, axis=-1, keepdims=True)
                g = pg_ref[rows, G_G + h * GLA_DV:G_G + (h + 1) * GLA_DV]
                o = (o * lax.rsqrt(ms + EPS) * nw) * _silu(g)
                o_ref[rows, h * GLA_DV:(h + 1) * GLA_DV] = o.astype(BF16)


def _gla(pg, a_w, a_b, norm_w):
    nblk = SEQ // TB_SCAN
    return pl.pallas_call(
        _gla_kernel,
        out_shape=jax.ShapeDtypeStruct((TOKENS, GLA_WIDTH), BF16),
        grid=(BATCH, nblk),
        in_specs=[
            pl.BlockSpec((TB_SCAN, GLA_COLS), lambda b, i: (b * nblk + i, 0)),
            pl.BlockSpec((LANES, GLA_QK), lambda b, i: (0, 0)),
            pl.BlockSpec((1, GLA_QK), lambda b, i: (0, 0)),
            pl.BlockSpec((1, GLA_DV), lambda b, i: (0, 0)),
        ],
        out_specs=pl.BlockSpec((TB_SCAN, GLA_WIDTH), lambda b, i: (b * nblk + i, 0)),
        scratch_shapes=[pltpu.VMEM((TB_SCAN, GLA_QK), F32),
                        pltpu.VMEM((GLA_HEADS // 2, GLA_DV, LANES), F32)],
        compiler_params=_params("arbitrary", "arbitrary"),
        name="gla",
    )(pg, a_w, a_b, norm_w)


def _ssd_kernel(ps_ref, cw_ref, cb_ref, dtb_ref, alog_ref, dsk_ref, nw_ref, e64_ref, e128_ref,
                o_ref, xbc_scr, st_scr):
    first = pl.program_id(1) == 0
    halo = SSD_HALO
    c = SSD_CHUNK

    @pl.when(first)
    def _():
        xbc_scr[0:halo, :] = jnp.zeros((halo, SSD_CONV_DIM), F32)
        st_scr[...] = jnp.zeros_like(st_scr)

    @pl.when(jnp.logical_not(first))
    def _():
        xbc_scr[0:halo, :] = xbc_scr[TB_SCAN:TB_SCAN + halo, :]

    xbc_scr[halo:halo + TB_SCAN, :] = ps_ref[:, S_X:S_X + SSD_CONV_DIM]

    row = lax.broadcasted_iota(jnp.int32, (c, c), 0)
    col = lax.broadcasted_iota(jnp.int32, (c, c), 1)
    causal = row >= col
    tril = jnp.where(causal, 1.0, 0.0).astype(BF16)
    head_lane = col < SSD_HEADS
    low_half = col < SSD_HEADDIM
    st_row = lax.broadcasted_iota(jnp.int32, (c, SSD_WIDTH), 0)
    st_col = lax.broadcasted_iota(jnp.int32, (c, SSD_WIDTH), 1)
    blockdiag = (st_row < SSD_STATE) == (st_col < SSD_WIDTH // SSD_GROUPS)
    a_neg = -jnp.exp(alog_ref[...])
    heads_per_group = SSD_HEADS // SSD_GROUPS

    for ci in range(TB_SCAN // c):
        rows = slice(ci * c, (ci + 1) * c)
        win = xbc_scr[ci * c:(ci + 1) * c + halo, :]
        conv = cw_ref[0:1, :] * win
        for k in range(1, SSD_CONV):
            conv = pltpu.roll(conv, 1, axis=0) + cw_ref[k:k + 1, :] * win
        act = _silu(conv[halo:, :] + cb_ref[...])
        dt = jnp.where(head_lane, _softplus(ps_ref[rows, S_DT:S_DT + LANES] + dtb_ref[...]), 0.0)
        cum = _dot_exact_lhs(tril, dt * a_neg)
        cum_t = cum.T
        cum64 = _dot_exact_rhs(cum, e64_ref[...])
        dt64 = _dot_exact_rhs(dt, e64_ref[...])
        cum_col = _dot_exact_rhs(cum, e128_ref[...])
        cl64 = cum64[c - 1:c, :]
        xs = act[:, 0:SSD_WIDTH]
        bm = act[:, SSD_WIDTH:SSD_WIDTH + SSD_BC].astype(BF16)
        cm = act[:, SSD_WIDTH + SSD_BC:SSD_CONV_DIM]
        xdt = xs * dt64
        xdt_bf = xdt.astype(BF16)
        y_parts = []
        for g in range(SSD_GROUPS):
            cm_g = jnp.where(low_half if g == 0 else jnp.logical_not(low_half), cm, 0.0).astype(BF16)
            scores = _dot_nt(cm_g, bm)
            for pp in range(heads_per_group // 2):
                p = g * (heads_per_group // 2) + pp
                ys = []
                for j in range(2):
                    h = 2 * p + j
                    seg = cum_col[:, h * LANES:(h + 1) * LANES] - cum_t[h:h + 1, :]
                    decay = jnp.exp(jnp.where(causal, seg, -jnp.inf))
                    ys.append(_dot((scores * decay).astype(BF16), xdt_bf[:, p * LANES:(p + 1) * LANES]))
                y_parts.append(jnp.where(low_half, ys[0], ys[1]))
        y = jnp.concatenate(y_parts, axis=1)
        st = st_scr[...]
        y = y + _dot(cm.astype(BF16), st.astype(BF16)) * jnp.exp(cum64)
        contrib = _dot_tn(bm, (xdt * jnp.exp(cl64 - cum64)).astype(BF16))
        st_scr[...] = st * jnp.exp(cl64) + jnp.where(blockdiag, contrib, 0.0)
        y = y + dsk_ref[...] * xs
        y = y * _silu(ps_ref[rows, S_Z:S_Z + SSD_WIDTH])
        ms = jnp.mean(y * y, axis=-1, keepdims=True)
        o_ref[rows, :] = (y * lax.rsqrt(ms + EPS) * nw_ref[...]).astype(BF16)


def _ssd(ps, conv_w, conv_b, dt_bias, a_log, d_skip, norm_w, e64, e128):
    nblk = SEQ // TB_SCAN
    const = lambda b, i: (0, 0)
    return pl.pallas_call(
        _ssd_kernel,
        out_shape=jax.ShapeDtypeStruct((TOKENS, SSD_WIDTH), BF16),
        grid=(BATCH, nblk),
        in_specs=[
            pl.BlockSpec((TB_SCAN, SSD_COLS), lambda b, i: (b * nblk + i, 0)),
            pl.BlockSpec((SSD_CONV, SSD_CONV_DIM), const),
            pl.BlockSpec((1, SSD_CONV_DIM), const),
            pl.BlockSpec((1, LANES), const),
            pl.BlockSpec((1, LANES), const),
            pl.BlockSpec((1, SSD_WIDTH), const),
            pl.BlockSpec((1, SSD_WIDTH), const),
            pl.BlockSpec((LANES, SSD_WIDTH), const),
            pl.BlockSpec((LANES, SSD_HEADS * LANES), const),
        ],
        out_specs=pl.BlockSpec((TB_SCAN, SSD_WIDTH), lambda b, i: (b * nblk + i, 0)),
        scratch_shapes=[pltpu.VMEM((TB_SCAN + SSD_HALO, SSD_CONV_DIM), F32),
                        pltpu.VMEM((SSD_BC, SSD_WIDTH), F32)],
        compiler_params=_params("arbitrary", "arbitrary"),
        name="ssd",
    )(ps, conv_w, conv_b, dt_bias, a_log, d_skip, norm_w, e64, e128)


def _outproj_kernel(x_ref, og_ref, oy_ref, wo_ref, mod_ref, nw_ref, *rest, with_router):
    if with_router:
        rw_ref, xo_ref, h_ref, gates_ref, sel1_ref, sel2_ref, cnt_ref = rest
    else:
        xo_ref, h_ref = rest
    mod = mod_ref[0, 0]
    half = TM_PROJ // 2
    counts = jnp.zeros((1, LANES), F32)
    for r0 in (0, half):
        rows = slice(r0, r0 + half)
        mix = (_dot(og_ref[rows, :], wo_ref[0, 0:GLA_WIDTH, :])
               + _dot(oy_ref[rows, :], wo_ref[0, GLA_WIDTH:, :]))
        x_new = x_ref[rows, :] + mod[2:3] * mix
        xo_ref[rows, :] = x_new
        h = _norm_mod(x_new, nw_ref[...], mod[4:5], mod[3:4])
        h_ref[rows, :] = h.astype(BF16)
        if with_router:
            logits = _dot3(h, rw_ref[...])
            lane = lax.broadcasted_iota(jnp.int32, logits.shape, 1)
            lg = jnp.where(lane < N_EXPERTS, logits, -jnp.inf)
            v1 = jnp.max(lg, axis=-1, keepdims=True)
            i1 = jnp.min(jnp.where(lg == v1, lane, LANES), axis=-1, keepdims=True)
            lg2 = jnp.where(lane == i1, -jnp.inf, lg)
            v2 = jnp.max(lg2, axis=-1, keepdims=True)
            i2 = jnp.min(jnp.where(lg2 == v2, lane, LANES), axis=-1, keepdims=True)
            e2 = jnp.exp(v2 - v1)
            p1 = 1.0 / (1.0 + e2)
            p2 = e2 / (1.0 + e2)
            sel1 = jnp.where(lane == i1, 1.0, 0.0)
            sel2 = jnp.where(lane == i2, 1.0, 0.0)
            gates_ref[rows, :] = sel1 * p1 + sel2 * p2
            sel1_ref[rows, :] = sel1
            sel2_ref[rows, :] = sel2
            counts = counts + jnp.sum(sel1 + sel2, axis=0, keepdims=True)
    if with_router:
        cnt_ref[0] = jnp.broadcast_to(counts.astype(jnp.int32), (8, LANES))


def _outproj(x, og, oy, w_out, layer, mod_l, norm_w, router_w):
    tiles_per_batch = SEQ // TM_PROJ
    with_router = router_w is not None
    tile = lambda i: (i, 0)
    const = lambda i: (0, 0)
    in_specs = [
        pl.BlockSpec((TM_PROJ, D_MODEL), tile),
        pl.BlockSpec((TM_PROJ, GLA_WIDTH), tile),
        pl.BlockSpec((TM_PROJ, SSD_WIDTH), tile),
        pl.BlockSpec((1, D_MODEL, D_MODEL), lambda i: (layer, 0, 0)),
        pl.BlockSpec((1, 1, 6, D_MODEL), lambda i: (i // tiles_per_batch, 0, 0, 0)),
        pl.BlockSpec((1, D_MODEL), const),
    ]
    out_shape = [jax.ShapeDtypeStruct((TOKENS, D_MODEL), F32),
                 jax.ShapeDtypeStruct((TOKENS, D_MODEL), BF16)]
    out_specs = [pl.BlockSpec((TM_PROJ, D_MODEL), tile), pl.BlockSpec((TM_PROJ, D_MODEL), tile)]
    args = [x, og, oy, w_out, mod_l, norm_w]
    if with_router:
        in_specs.append(pl.BlockSpec((D_MODEL, LANES), const))
        for _ in range(3):
            out_shape.append(jax.ShapeDtypeStruct((TOKENS, LANES), F32))
            out_specs.append(pl.BlockSpec((TM_PROJ, LANES), tile))
        out_shape.append(jax.ShapeDtypeStruct((TOKENS // TM_PROJ, 8, LANES), jnp.int32))
        out_specs.append(pl.BlockSpec((1, 8, LANES), lambda i: (i, 0, 0)))
        args.append(router_w)
    return pl.pallas_call(
        functools.partial(_outproj_kernel, with_router=with_router),
        out_shape=tuple(out_shape),
        grid=(TOKENS // TM_PROJ,),
        in_specs=in_specs,
        out_specs=tuple(out_specs),
        compiler_params=_params("arbitrary"),
        name="outproj_router" if with_router else "outproj",
    )(*args)


def _swiglu_chunk(h, wg, wu, wd):
    a = (_silu(_dot(h, wg.astype(BF16))) * _dot(h, wu.astype(BF16))).astype(BF16)
    return _dot(a, wd.astype(BF16))


def _ffn_kernel(h_ref, x_ref, mod_ref, wg_ref, wu_ref, wd_ref, o_ref):
    @pl.when(pl.program_id(1) == 0)
    def _():
        o_ref[...] = x_ref[...]

    o_ref[...] += mod_ref[0, 0, 5:6, :] * _swiglu_chunk(h_ref[...], wg_ref[0], wu_ref[0], wd_ref[0])


def _ffn(h, x, mod_l, w_gate, w_up, w_down, layer_idx):
    tiles_per_batch = SEQ // TM_FFN
    tile = lambda i, j: (i, 0)
    return pl.pallas_call(
        _ffn_kernel,
        out_shape=jax.ShapeDtypeStruct((TOKENS, D_MODEL), F32),
        grid=(TOKENS // TM_FFN, N_FF_CHUNKS),
        in_specs=[pl.BlockSpec((TM_FFN, D_MODEL), tile),
                  pl.BlockSpec((TM_FFN, D_MODEL), tile),
                  pl.BlockSpec((1, 1, 6, D_MODEL), lambda i, j: (i // tiles_per_batch, 0, 0, 0)),
                  pl.BlockSpec((1, D_MODEL, TF_FFN), lambda i, j: (layer_idx, 0, j)),
                  pl.BlockSpec((1, D_MODEL, TF_FFN), lambda i, j: (layer_idx, 0, j)),
                  pl.BlockSpec((1, TF_FFN, D_MODEL), lambda i, j: (layer_idx, j, 0))],
        out_specs=pl.BlockSpec((TM_FFN, D_MODEL), tile),
        compiler_params=_params("arbitrary", "arbitrary"),
        name="dense_ffn",
    )(h, x, mod_l, w_gate, w_up, w_down)


def _moe_plan(cnt):
    seg_len = (cnt + SEG_ALIGN - 1) // SEG_ALIGN * SEG_ALIGN
    loc_off = jnp.cumsum(seg_len, axis=1) - seg_len
    n_rows = seg_len.sum(axis=0)
    region = (n_rows + MOE_TILE - 1) // MOE_TILE * MOE_TILE
    base = jnp.cumsum(region) - region
    seg_start = base[None, :] + jnp.cumsum(seg_len, axis=0) - seg_len
    tiles_e = region // MOE_TILE
    tile_end = jnp.cumsum(tiles_e)
    n_act = tile_end[-1]
    r = jnp.arange(MOE_MAX_TILES, dtype=jnp.int32)
    r_act = jnp.minimum(r, n_act - 1)
    tile_exp = jnp.sum(r_act[:, None] >= tile_end[None, :], axis=1).astype(jnp.int32)
    rows_left = n_rows[tile_exp] - (r_act - (tile_end - tiles_e)[tile_exp]) * MOE_TILE
    n_sub = jnp.clip((rows_left + MOE_SUB - 1) // MOE_SUB, 0, MOE_TILE // MOE_SUB)
    n_sub = jnp.where(r < n_act, n_sub, 0)
    i32 = lambda a: a.reshape(-1).astype(jnp.int32)
    return dict(seg=(i32(seg_start), i32(loc_off), i32(seg_len)),
                fill=(i32(base + n_rows), i32(region - n_rows), i32(n_act)),
                tiles=(tile_exp, i32(r_act), i32(n_sub)))


def _segment_dma(src, dst, src_off, dst_off, length, sem, wait, advance_src=True):
    off = jnp.int32(0)
    for size in SEG_PIECES:
        take = (length & size) != 0
        s0 = pl.multiple_of(src_off + off, SEG_ALIGN) if advance_src else src_off
        d0 = pl.multiple_of(dst_off + off, SEG_ALIGN)

        @pl.when(take)
        def _():
            cp = pltpu.make_async_copy(src.at[pl.ds(s0, size)], dst.at[pl.ds(d0, size)], sem)
            if wait:
                cp.wait()
            else:
                cp.start()

        off = off + jnp.where(take, size, 0)


def _local_rows(i, lo_ref, sel1, sel2):
    tm = sel1.shape[0]
    tr = lax.broadcasted_iota(jnp.int32, (tm, tm), 0)
    tc = lax.broadcasted_iota(jnp.int32, (tm, tm), 1)
    earlier = jnp.where(tr > tc, 1.0, 0.0).astype(BF16)
    lane = lax.broadcasted_iota(jnp.int32, (1, LANES), 1)
    loc = jnp.zeros((1, LANES), F32)
    for e in range(N_EXPERTS):
        loc = jnp.where(lane == e, lo_ref[i * N_EXPERTS + e].astype(F32), loc)
    pos = _dot(earlier, (sel1 + sel2).astype(BF16)) + loc
    return jnp.sum(sel1 * pos, axis=-1, keepdims=True), jnp.sum(sel2 * pos, axis=-1, keepdims=True)


def _moe_sort_kernel(ss_ref, lo_ref, ln_ref, fs_ref, fl_ref, na_ref, h_ref, sel1_ref, sel2_ref, xs_ref,
                     loc_scr, zero_scr, sem, fill_sem):
    i = pl.program_id(0)
    last = pl.num_programs(0) - 1
    slot = lax.rem(i, 2)
    pos1, pos2 = _local_rows(i, lo_ref, sel1_ref[...], sel2_ref[...])
    cid = lax.broadcasted_iota(jnp.int32, (TM_PROJ, MOE_LOC_ROWS), 1).astype(F32)
    pick = jnp.where(cid == pos1, 1.0, jnp.where(cid == pos2, 1.0, 0.0)).astype(BF16)
    loc_scr[slot] = _dot_tn(pick, h_ref[...]).astype(BF16)

    def copies(tile, buf, wait):
        for e in range(N_EXPERTS):
            k = tile * N_EXPERTS + e
            _segment_dma(loc_scr.at[buf], xs_ref, lo_ref[k], ss_ref[k], ln_ref[k], sem.at[buf], wait)

    copies(i, slot, False)

    @pl.when(i > 0)
    def _():
        copies(i - 1, 1 - slot, True)

    @pl.when(i == last)
    def _():
        copies(i, slot, True)
        zero_scr[...] = jnp.zeros_like(zero_scr)
        n_act = na_ref[0]

        def fill_tiles(wait):
            def body(r, carry):
                for part in range(MOE_TILE // FILL_ROWS):
                    d0 = pl.multiple_of(r * MOE_TILE + part * FILL_ROWS, FILL_ROWS)
                    cp = pltpu.make_async_copy(zero_scr, xs_ref.at[pl.ds(d0, FILL_ROWS)], fill_sem)
                    if wait:
                        cp.wait()
                    else:
                        cp.start()
                return carry
            lax.fori_loop(n_act, MOE_MAX_TILES, body, 0)

        for wait in (False, True):
            for e in range(N_EXPERTS):
                _segment_dma(zero_scr, xs_ref, 0, fs_ref[e], fl_ref[e], fill_sem, wait, advance_src=False)
            fill_tiles(wait)


def _moe_sort(plan, h, sel1, sel2):
    tile = lambda i, *_: (i, 0)
    return pl.pallas_call(
        _moe_sort_kernel,
        out_shape=jax.ShapeDtypeStruct((MOE_MAX_TILES * MOE_TILE, D_MODEL), BF16),
        grid_spec=pltpu.PrefetchScalarGridSpec(
            num_scalar_prefetch=6,
            grid=(TOKENS // TM_PROJ,),
            in_specs=[pl.BlockSpec((TM_PROJ, D_MODEL), tile),
                      pl.BlockSpec((TM_PROJ, LANES), tile),
                      pl.BlockSpec((TM_PROJ, LANES), tile)],
            out_specs=pl.BlockSpec(memory_space=pl.ANY),
            scratch_shapes=[pltpu.VMEM((2, MOE_LOC_ROWS, D_MODEL), BF16),
                            pltpu.VMEM((FILL_ROWS, D_MODEL), BF16),
                            pltpu.SemaphoreType.DMA((2,)),
                            pltpu.SemaphoreType.DMA],
        ),
        compiler_params=_params("arbitrary"),
        name="moe_sort",
    )(*plan["seg"], *plan["fill"], h, sel1, sel2)


def _moe_ffn_kernel(te_ref, ra_ref, ns_ref, xs_ref, wg_ref, wu_ref, wd_ref, o_ref, acc_ref):
    del te_ref, ra_ref
    r, j = pl.program_id(0), pl.program_id(1)
    n_sub = ns_ref[r]
    n_parts = MOE_TILE // MOE_SUB
    for used in range(1, n_parts + 1):
        rows = slice(0, used * MOE_SUB)

        @pl.when(n_sub == used)
        def _():
            @pl.when(j == 0)
            def _():
                acc_ref[rows, :] = jnp.zeros((used * MOE_SUB, D_MODEL), F32)

            acc_ref[rows, :] += _swiglu_chunk(xs_ref[rows, :], wg_ref[0, 0], wu_ref[0, 0], wd_ref[0, 0])

            @pl.when(j == N_FF_MOE - 1)
            def _():
                o_ref[rows, :] = acc_ref[rows, :].astype(BF16)

    for s in range(n_parts):
        @pl.when(jnp.logical_and(s >= n_sub, j == N_FF_MOE - 1))
        def _():
            o_ref[s * MOE_SUB:(s + 1) * MOE_SUB, :] = jnp.zeros((MOE_SUB, D_MODEL), BF16)


def _moe_ffn(plan, xs, w_gate, w_up, w_down, layer_idx):
    last = N_FF_MOE - 1
    rows = lambda r, j, te, ra, ns: (ra[r], 0)
    chunk = lambda r, j, ns: jnp.where(ns[r] > 0, j, last)
    return pl.pallas_call(
        _moe_ffn_kernel,
        out_shape=jax.ShapeDtypeStruct(xs.shape, BF16),
        grid_spec=pltpu.PrefetchScalarGridSpec(
            num_scalar_prefetch=3,
            grid=(MOE_MAX_TILES, N_FF_MOE),
            in_specs=[pl.BlockSpec((MOE_TILE, D_MODEL), rows),
                      pl.BlockSpec((1, 1, D_MODEL, TF_MOE),
                                   lambda r, j, te, ra, ns: (layer_idx, te[r], 0, chunk(r, j, ns))),
                      pl.BlockSpec((1, 1, D_MODEL, TF_MOE),
                                   lambda r, j, te, ra, ns: (layer_idx, te[r], 0, chunk(r, j, ns))),
                      pl.BlockSpec((1, 1, TF_MOE, D_MODEL),
                                   lambda r, j, te, ra, ns: (layer_idx, te[r], chunk(r, j, ns), 0))],
            out_specs=pl.BlockSpec((MOE_TILE, D_MODEL), lambda r, j, *_: (r, 0)),
            scratch_shapes=[pltpu.VMEM((MOE_TILE, D_MODEL), F32)],
        ),
        compiler_params=_params("arbitrary", "arbitrary"),
        name="moe_ffn",
    )(*plan["tiles"], xs, w_gate, w_up, w_down)


def _moe_combine_kernel(ss_ref, lo_ref, ln_ref, x_ref, mod_ref, gates_ref, sel1_ref, sel2_ref, *rest,
                        final_norm):
    if final_norm:
        fw_ref, ys_ref, o_ref, loc_scr, sem = rest
    else:
        ys_ref, o_ref, loc_scr, sem = rest
    i = pl.program_id(0)
    tm = TM_PROJ
    slot = lax.rem(i, 2)

    def fetch(tile, buf, wait):
        for e in range(N_EXPERTS):
            k = tile * N_EXPERTS + e
            _segment_dma(ys_ref, loc_scr.at[buf], ss_ref[k], lo_ref[k], ln_ref[k], sem.at[buf], wait)

    @pl.when(i == 0)
    def _():
        loc_scr[...] = jnp.zeros_like(loc_scr)
        fetch(0, 0, False)

    @pl.when(i + 1 < pl.num_programs(0))
    def _():
        fetch(i + 1, 1 - slot, False)

    sel1, sel2, gates = sel1_ref[...], sel2_ref[...], gates_ref[...]
    pos1, pos2 = _local_rows(i, lo_ref, sel1, sel2)
    p1 = jnp.sum(sel1 * gates, axis=-1, keepdims=True)
    p2 = jnp.sum(sel2 * gates, axis=-1, keepdims=True)
    cid = lax.broadcasted_iota(jnp.int32, (tm, MOE_LOC_ROWS), 1).astype(F32)
    pick1 = jnp.where(cid == pos1, 1.0, 0.0).astype(BF16)
    pick2 = jnp.where(cid == pos2, 1.0, 0.0).astype(BF16)
    fetch(i, slot, True)
    y = loc_scr[slot]
    ff = p1 * _dot(pick1, y) + p2 * _dot(pick2, y)
    out = x_ref[...] + mod_ref[0, 0, 5:6, :] * ff
    if final_norm:
        ms = jnp.mean(out * out, axis=-1, keepdims=True)
        out = out * lax.rsqrt(ms + EPS) * fw_ref[...]
    o_ref[...] = out


def _moe_combine(plan, x, mod_l, gates, sel1, sel2, ys, final_w=None):
    tiles_per_batch = SEQ // TM_PROJ
    tile = lambda i, *_: (i, 0)
    final_norm = final_w is not None
    in_specs = [pl.BlockSpec((TM_PROJ, D_MODEL), tile),
                pl.BlockSpec((1, 1, 6, D_MODEL), lambda i, *_: (i // tiles_per_batch, 0, 0, 0)),
                pl.BlockSpec((TM_PROJ, LANES), tile),
                pl.BlockSpec((TM_PROJ, LANES), tile),
                pl.BlockSpec((TM_PROJ, LANES), tile)]
    args = [x, mod_l, gates, sel1, sel2]
    if final_norm:
        in_specs.append(pl.BlockSpec((1, D_MODEL), lambda i, *_: (0, 0)))
        args.append(final_w)
    return pl.pallas_call(
        functools.partial(_moe_combine_kernel, final_norm=final_norm),
        out_shape=jax.ShapeDtypeStruct((TOKENS, D_MODEL), F32),
        grid_spec=pltpu.PrefetchScalarGridSpec(
            num_scalar_prefetch=3,
            grid=(TOKENS // TM_PROJ,),
            in_specs=in_specs + [pl.BlockSpec(memory_space=pl.ANY)],
            out_specs=pl.BlockSpec((TM_PROJ, D_MODEL), tile),
            scratch_shapes=[pltpu.VMEM((2, MOE_LOC_ROWS, D_MODEL), BF16), pltpu.SemaphoreType.DMA((2,))],
        ),
        compiler_params=_params("arbitrary"),
        name="moe_combine_norm" if final_norm else "moe_combine",
    )(*plan["seg"], *args, ys)


def _final_norm_kernel(x_ref, w_ref, o_ref):
    x = x_ref[...]
    ms = jnp.mean(x * x, axis=-1, keepdims=True)
    o_ref[...] = x * lax.rsqrt(ms + EPS) * w_ref[...]


def _final_norm(x, w):
    return pl.pallas_call(
        _final_norm_kernel,
        out_shape=jax.ShapeDtypeStruct((TOKENS, D_MODEL), F32),
        grid=(TOKENS // TM_FFN,),
        in_specs=[pl.BlockSpec((TM_FFN, D_MODEL), lambda i: (i, 0)),
                  pl.BlockSpec((1, D_MODEL), lambda i: (0, 0))],
        out_specs=pl.BlockSpec((TM_FFN, D_MODEL), lambda i: (i, 0)),
        compiler_params=_params("arbitrary"),
        name="final_norm",
    )(x, w)


def _pad_cols(a, width):
    return jnp.pad(a, [(0, 0)] * (a.ndim - 1) + [(0, width - a.shape[-1])])


def _split_w_in(w_in):
    ssd_start = G_A + GLA_LOWRANK
    return w_in[:, :, :GLA_COLS].astype(BF16), _pad_cols(w_in[:, :, ssd_start:], SSD_COLS).astype(BF16)


def kernel(x, c, ada_w, ada_b, norm1_w, w_in, gla_a_w, gla_a_b, gla_norm_w, conv_w, conv_b, dt_bias, a_log,
           d_skip, ssd_norm_w, w_out, norm2_w, ffn_w_gate, ffn_w_up, ffn_w_down, router_w, moe_w_gate,
           moe_w_up, moe_w_down, final_norm_w):
    xt = x.reshape(TOKENS, D_MODEL)
    c_pad = jnp.pad(c, ((0, 8 - BATCH), (0, 0)))
    mod = _adaln(c_pad, ada_w, ada_b)[:, :BATCH].reshape(DEPTH, BATCH, 6, D_MODEL)

    w_gla, w_ssd = _split_w_in(w_in)
    w_out_bf = w_out.astype(BF16)
    a_w_p = jnp.pad(gla_a_w, ((0, 0), (0, LANES - GLA_LOWRANK), (0, 0)))
    dtb_p = _pad_cols(dt_bias, LANES)
    alog_p = _pad_cols(a_log, LANES)
    dsk_p = jnp.repeat(d_skip, SSD_HEADDIM, axis=-1)
    rw_p = _pad_cols(router_w, LANES)
    lane_head = jnp.arange(LANES)[:, None]
    e64 = (lane_head == jnp.arange(SSD_WIDTH)[None, :] // SSD_HEADDIM).astype(BF16)
    e128 = (lane_head == jnp.arange(SSD_HEADS * LANES)[None, :] // LANES).astype(BF16)

    for l in range(DEPTH):
        mod_l = mod[l].reshape(BATCH, 1, 6, D_MODEL)
        pg, ps = _inproj(xt, mod_l, norm1_w[l][None], w_gla, w_ssd, l)
        og = _gla(pg, a_w_p[l], gla_a_b[l][None], gla_norm_w[l][None])
        oy = _ssd(ps, conv_w[l], conv_b[l][None], dtb_p[l][None], alog_p[l][None], dsk_p[l][None],
                  ssd_norm_w[l][None], e64, e128)
        i = l // 2
        if l % 2 == 0:
            xt, h2 = _outproj(xt, og, oy, w_out_bf, l, mod_l, norm2_w[l][None], None)
            xt = _ffn(h2, xt, mod_l, ffn_w_gate, ffn_w_up, ffn_w_down, i)
        else:
            xt, h2, gates, sel1, sel2, cnt = _outproj(xt, og, oy, w_out_bf, l, mod_l, norm2_w[l][None], rw_p[i])
            plan = _moe_plan(cnt[:, 0, :N_EXPERTS])
            xs = _moe_sort(plan, h2, sel1, sel2)
            ys = _moe_ffn(plan, xs, moe_w_gate, moe_w_up, moe_w_down, i)
            last = l == DEPTH - 1
            xt = _moe_combine(plan, xt, mod_l, gates, sel1, sel2, ys, final_norm_w[None] if last else None)
    if DEPTH % 2:
        xt = _final_norm(xt, final_norm_w[None])
    return xt.reshape(BATCH, SEQ, D_MODEL)
```

```python
import functools

import jax
import jax.numpy as jnp
from jax import lax
from jax.experimental import pallas as pl
from jax.experimental.pallas import tpu as pltpu

D_MODEL = 1024
BATCH = 2
SEQ = 8192
DEPTH = 4
TOKENS = BATCH * SEQ

GLA_HEADS = 4
GLA_DK = 64
GLA_DV = 128
GLA_QK = GLA_HEADS * GLA_DK
GLA_WIDTH = GLA_HEADS * GLA_DV
GLA_LOWRANK = 16
GLA_TAU = 16.0
GLA_CHUNK = 64

SSD_HEADS = 8
SSD_HEADDIM = 64
SSD_WIDTH = SSD_HEADS * SSD_HEADDIM
SSD_GROUPS = 2
SSD_STATE = 64
SSD_BC = SSD_GROUPS * SSD_STATE
SSD_CONV = 4
SSD_CONV_DIM = SSD_WIDTH + 2 * SSD_BC
SSD_CHUNK = 128

D_FF = 3584
N_EXPERTS = 8
EPS = 1e-6

LANES = 128
GLA_COLS = 2 * GLA_QK + 2 * GLA_WIDTH + LANES
SSD_COLS = 2 * SSD_WIDTH + 2 * SSD_BC + LANES
G_Q, G_K, G_V, G_G, G_A = 0, GLA_QK, 2 * GLA_QK, 2 * GLA_QK + GLA_WIDTH, 2 * GLA_QK + 2 * GLA_WIDTH
S_Z, S_X, S_DT = 0, SSD_WIDTH, SSD_WIDTH + SSD_CONV_DIM

TM_IN = 1024
TM_PROJ = 512
TB_SCAN = 512
TM_FFN = 1024
TF_FFN = 512
GLA_GROUP = 4
SSD_HALO = 8
N_FF_CHUNKS = D_FF // TF_FFN
VMEM_LIMIT = 56 * 1024 * 1024

TOP_K = 2
SEG_ALIGN = 16
SEG_PIECES = tuple(TM_PROJ >> s for s in range(6))
MOE_TILE = 1024
MOE_SUB = 256
FILL_ROWS = SEG_PIECES[0]
TF_MOE = TF_FFN
N_FF_MOE = D_FF // TF_MOE
N_TILES = TOKENS // TM_PROJ
MOE_LOC_ROWS = -(-(TOP_K * TM_PROJ + N_EXPERTS * (SEG_ALIGN - 1)) // LANES) * LANES
MOE_MAX_TILES = (TOP_K * TOKENS + N_TILES * N_EXPERTS * (SEG_ALIGN - 1) + N_EXPERTS * (MOE_TILE - 1)) // MOE_TILE

F32 = jnp.float32
BF16 = jnp.bfloat16


def _dot(a, b):
    return jnp.dot(a, b, preferred_element_type=F32)


def _dot_nt(a, b):
    return lax.dot_general(a, b, (((1,), (1,)), ((), ())), preferred_element_type=F32)


def _dot_tn(a, b):
    return lax.dot_general(a, b, (((0,), (0,)), ((), ())), preferred_element_type=F32)


def _split(a):
    hi = a.astype(BF16)
    lo = (a - hi.astype(F32)).astype(BF16)
    return hi, lo


def _dot3(a, b):
    a_hi, a_lo = _split(a)
    b_hi, b_lo = _split(b)
    return _dot(a_hi, b_hi) + _dot(a_lo, b_hi) + _dot(a_hi, b_lo)


def _dot_exact_rhs(a, b_bf16):
    a_hi, a_lo = _split(a)
    return _dot(a_hi, b_bf16) + _dot(a_lo, b_bf16)


def _dot_exact_lhs(a_bf16, b):
    b_hi, b_lo = _split(b)
    return _dot(a_bf16, b_hi) + _dot(a_bf16, b_lo)


def _silu(x):
    return x * (0.5 * jnp.tanh(0.5 * x) + 0.5)


def _softplus(x):
    return jnp.maximum(x, 0.0) + jnp.log1p(jnp.exp(-jnp.abs(x)))


def _norm_mod(x, w, scale, shift):
    ms = jnp.mean(x * x, axis=-1, keepdims=True)
    return (x * lax.rsqrt(ms + EPS) * w) * (1.0 + scale) + shift


def _params(*sem):
    return pltpu.CompilerParams(dimension_semantics=sem, vmem_limit_bytes=VMEM_LIMIT)


def _adaln_kernel(c_ref, w_ref, b_ref, o_ref):
    s = _silu(c_ref[...])
    o_ref[0] = _dot3(s, w_ref[0]) + b_ref[0]


def _adaln(c_pad, ada_w, ada_b):
    n_col = 6 * D_MODEL // D_MODEL
    return pl.pallas_call(
        _adaln_kernel,
        out_shape=jax.ShapeDtypeStruct((DEPTH, 8, 6 * D_MODEL), F32),
        grid=(DEPTH, n_col),
        in_specs=[
            pl.BlockSpec((8, D_MODEL), lambda l, j: (0, 0)),
            pl.BlockSpec((1, D_MODEL, D_MODEL), lambda l, j: (l, 0, j)),
            pl.BlockSpec((1, 1, D_MODEL), lambda l, j: (l, 0, j)),
        ],
        out_specs=pl.BlockSpec((1, 8, D_MODEL), lambda l, j: (l, 0, j)),
        compiler_params=_params("arbitrary", "arbitrary"),
        name="adaln",
    )(c_pad, ada_w, ada_b.reshape(DEPTH, 1, 6 * D_MODEL))


def _inproj_kernel(x_ref, mod_ref, nw_ref, wg_ref, ws_ref, og_ref, os_ref):
    mod = mod_ref[0, 0]
    half = TM_IN // 2
    for r0 in (0, half):
        rows = slice(r0, r0 + half)
        h = _norm_mod(x_ref[rows, :], nw_ref[...], mod[1:2], mod[0:1]).astype(BF16)
        for w_ref, o_ref in ((wg_ref, og_ref), (ws_ref, os_ref)):
            n_cols = o_ref.shape[1]
            for c0 in range(0, n_cols, 512):
                c1 = min(c0 + 512, n_cols)
                o_ref[rows, c0:c1] = _dot(h, w_ref[0, :, c0:c1])


def _inproj(x, mod_l, norm_w, w_gla, w_ssd, layer):
    tiles_per_batch = SEQ // TM_IN
    return pl.pallas_call(
        _inproj_kernel,
        out_shape=(jax.ShapeDtypeStruct((TOKENS, GLA_COLS), F32),
                   jax.ShapeDtypeStruct((TOKENS, SSD_COLS), F32)),
        grid=(TOKENS // TM_IN,),
        in_specs=[
            pl.BlockSpec((TM_IN, D_MODEL), lambda i: (i, 0)),
            pl.BlockSpec((1, 1, 6, D_MODEL), lambda i: (i // tiles_per_batch, 0, 0, 0)),
            pl.BlockSpec((1, D_MODEL), lambda i: (0, 0)),
            pl.BlockSpec((1, D_MODEL, GLA_COLS), lambda i: (layer, 0, 0)),
            pl.BlockSpec((1, D_MODEL, SSD_COLS), lambda i: (layer, 0, 0)),
        ],
        out_specs=(pl.BlockSpec((TM_IN, GLA_COLS), lambda i: (i, 0)),
                   pl.BlockSpec((TM_IN, SSD_COLS), lambda i: (i, 0))),
        compiler_params=_params("arbitrary"),
        name="inproj",
    )(x, mod_l, norm_w, w_gla, w_ssd)


def _gla_kernel(pg_ref, aw_ref, ab_ref, nw_ref, o_ref, la_scr, st_scr):
    @pl.when(pl.program_id(1) == 0)
    def _():
        st_scr[...] = jnp.zeros_like(st_scr)

    pre = _dot3(pg_ref[:, G_A:G_A + LANES], aw_ref[...]) + ab_ref[...]
    la_scr[...] = -_softplus(-pre) * (1.0 / GLA_TAU)

    c, gr = GLA_CHUNK, GLA_GROUP * GLA_CHUNK
    row = lax.broadcasted_iota(jnp.int32, (gr, gr), 0)
    col = lax.broadcasted_iota(jnp.int32, (gr, gr), 1)
    causal = jnp.logical_and(row >= col, jnp.bitwise_xor(row, col) < c)
    tril = jnp.where(causal, 1.0, 0.0).astype(BF16)
    causal2 = jnp.concatenate([causal, causal], axis=0)
    low_q = lax.broadcasted_iota(jnp.int32, (gr, LANES), 1) < GLA_DK
    low_s = lax.broadcasted_iota(jnp.int32, (GLA_DV, LANES), 1) < GLA_DK
    nw = nw_ref[...]

    n_groups = TB_SCAN // gr
    pairs = range(GLA_HEADS // 2)
    units = [(gi, p) for gi in range(n_groups) for p in pairs]
    rows_of = lambda gi: slice(gi * gr, (gi + 1) * gr)
    b_all = [_dot_exact_lhs(tril, la_scr[rows_of(gi), :]) for gi in range(n_groups)]
    b_last, qm, ke, k_tail, v_pair = {}, {}, {}, {}, {}
    for u in units:
        gi, p = u
        b = b_all[gi][:, p * LANES:(p + 1) * LANES]
        b_last[u] = [b[(ci + 1) * c - 1:(ci + 1) * c, :] for ci in range(GLA_GROUP)]
        b_last_rows = jnp.concatenate([jnp.broadcast_to(bl, (c, LANES)) for bl in b_last[u]], axis=0)
        q = pg_ref[rows_of(gi), G_Q + p * LANES:G_Q + (p + 1) * LANES] * (GLA_DK ** -0.5)
        k = pg_ref[rows_of(gi), G_K + p * LANES:G_K + (p + 1) * LANES]
        qe = q * jnp.exp(b)
        ke[u] = (k * jnp.exp(-b)).astype(BF16)
        k_tail[u] = (k * jnp.exp(b_last_rows - b)).astype(BF16)
        v_pair[u] = pg_ref[rows_of(gi), G_V + 2 * p * GLA_DV:G_V + 2 * (p + 1) * GLA_DV].astype(BF16)
        qm[u] = jnp.concatenate([jnp.where(low_q, qe, 0.0), jnp.where(low_q, 0.0, qe)], axis=0).astype(BF16)
    att = {u: jnp.where(causal2, _dot_nt(qm[u], ke[u]), 0.0).astype(BF16) for u in units}
    contrib = {u: [_dot_tn(v_pair[u][ci * c:(ci + 1) * c], k_tail[u][ci * c:(ci + 1) * c])
                   for ci in range(GLA_GROUP)] for u in units}
    o_intra = {u: [_dot(att[u][j * gr:(j + 1) * gr], v_pair[u][:, j * GLA_DV:(j + 1) * GLA_DV]) for j in range(2)]
               for u in units}
    o_inter = {u: [] for u in units}
    for p in pairs:
        st = st_scr[p]
        for gi in range(n_groups):
            u = (gi, p)
            for ci in range(GLA_GROUP):
                q_ci = jnp.concatenate([qm[u][ci * c:(ci + 1) * c], qm[u][gr + ci * c:gr + (ci + 1) * c]], axis=0)
                o_inter[u].append(_dot_nt(q_ci, st.astype(BF16)))
                st = (st * jnp.exp(b_last[u][ci])
                      + jnp.where(low_s, contrib[u][ci][:GLA_DV], contrib[u][ci][GLA_DV:]))
        st_scr[p] = st
    for u in units:
        gi, p = u
        for j in range(2):
            h = 2 * p + j
            o = o_intra[u][j] + jnp.concatenate([oi[j * c:(j + 1) * c] for oi in o_inter[u]], axis=0)
            ms = jnp.mean(o * o, axis=-1, keepdims=True)
            g = pg_ref[rows_of(gi), G_G + h * GLA_DV:G_G + (h + 1) * GLA_DV]
            o = (o * lax.rsqrt(ms + EPS) * nw) * _silu(g)
            o_ref[rows_of(gi), h * GLA_DV:(h + 1) * GLA_DV] = o.astype(BF16)


def _gla(pg, a_w, a_b, norm_w):
    nblk = SEQ // TB_SCAN
    return pl.pallas_call(
        _gla_kernel,
        out_shape=jax.ShapeDtypeStruct((TOKENS, GLA_WIDTH), BF16),
        grid=(BATCH, nblk),
        in_specs=[
            pl.BlockSpec((TB_SCAN, GLA_COLS), lambda b, i: (b * nblk + i, 0)),
            pl.BlockSpec((LANES, GLA_QK), lambda b, i: (0, 0)),
            pl.BlockSpec((1, GLA_QK), lambda b, i: (0, 0)),
            pl.BlockSpec((1, GLA_DV), lambda b, i: (0, 0)),
        ],
        out_specs=pl.BlockSpec((TB_SCAN, GLA_WIDTH), lambda b, i: (b * nblk + i, 0)),
        scratch_shapes=[pltpu.VMEM((TB_SCAN, GLA_QK), F32),
                        pltpu.VMEM((GLA_HEADS // 2, GLA_DV, LANES), F32)],
        compiler_params=_params("arbitrary", "arbitrary"),
        name="gla",
    )(pg, a_w, a_b, norm_w)


def _ssd_kernel(ps_ref, cw_ref, cb_ref, dtb_ref, alog_ref, dsk_ref, nw_ref, e64_ref, e128_ref,
                o_ref, xbc_scr, st_scr):
    first = pl.program_id(1) == 0
    halo = SSD_HALO
    c = SSD_CHUNK

    @pl.when(first)
    def _():
        xbc_scr[0:halo, :] = jnp.zeros((halo, SSD_CONV_DIM), F32)
        st_scr[...] = jnp.zeros_like(st_scr)

    @pl.when(jnp.logical_not(first))
    def _():
        xbc_scr[0:halo, :] = xbc_scr[TB_SCAN:TB_SCAN + halo, :]

    xbc_scr[halo:halo + TB_SCAN, :] = ps_ref[:, S_X:S_X + SSD_CONV_DIM]

    row = lax.broadcasted_iota(jnp.int32, (c, c), 0)
    col = lax.broadcasted_iota(jnp.int32, (c, c), 1)
    causal = row >= col
    tril = jnp.where(causal, 1.0, 0.0).astype(BF16)
    head_lane = col < SSD_HEADS
    low_half = col < SSD_HEADDIM
    st_row = lax.broadcasted_iota(jnp.int32, (c, SSD_WIDTH), 0)
    st_col = lax.broadcasted_iota(jnp.int32, (c, SSD_WIDTH), 1)
    blockdiag = (st_row < SSD_STATE) == (st_col < SSD_WIDTH // SSD_GROUPS)
    a_neg = -jnp.exp(alog_ref[...])
    heads_per_group = SSD_HEADS // SSD_GROUPS

    for ci in range(TB_SCAN // c):
        rows = slice(ci * c, (ci + 1) * c)
        win = xbc_scr[ci * c:(ci + 1) * c + halo, :]
        conv = cw_ref[0:1, :] * win
        for k in range(1, SSD_CONV):
            conv = pltpu.roll(conv, 1, axis=0) + cw_ref[k:k + 1, :] * win
        act = _silu(conv[halo:, :] + cb_ref[...])
        dt = jnp.where(head_lane, _softplus(ps_ref[rows, S_DT:S_DT + LANES] + dtb_ref[...]), 0.0)
        cum = _dot_exact_lhs(tril, dt * a_neg)
        cum_t = cum.T
        cum64 = _dot_exact_rhs(cum, e64_ref[...])
        dt64 = _dot_exact_rhs(dt, e64_ref[...])
        cum_col = _dot_exact_rhs(cum, e128_ref[...])
        cl64 = cum64[c - 1:c, :]
        xs = act[:, 0:SSD_WIDTH]
        bm = act[:, SSD_WIDTH:SSD_WIDTH + SSD_BC].astype(BF16)
        cm = act[:, SSD_WIDTH + SSD_BC:SSD_CONV_DIM]
        xdt = xs * dt64
        xdt_bf = xdt.astype(BF16)
        y_parts = []
        for g in range(SSD_GROUPS):
            cm_g = jnp.where(low_half if g == 0 else jnp.logical_not(low_half), cm, 0.0).astype(BF16)
            scores = _dot_nt(cm_g, bm)
            for pp in range(heads_per_group // 2):
                p = g * (heads_per_group // 2) + pp
                ys = []
                for j in range(2):
                    h = 2 * p + j
                    seg = cum_col[:, h * LANES:(h + 1) * LANES] - cum_t[h:h + 1, :]
                    decay = jnp.exp(jnp.where(causal, seg, -jnp.inf))
                    ys.append(_dot((scores * decay).astype(BF16), xdt_bf[:, p * LANES:(p + 1) * LANES]))
                y_parts.append(jnp.where(low_half, ys[0], ys[1]))
        y = jnp.concatenate(y_parts, axis=1)
        st = st_scr[...]
        y = y + _dot(cm.astype(BF16), st.astype(BF16)) * jnp.exp(cum64)
        contrib = _dot_tn(bm, (xdt * jnp.exp(cl64 - cum64)).astype(BF16))
        st_scr[...] = st * jnp.exp(cl64) + jnp.where(blockdiag, contrib, 0.0)
        y = y + dsk_ref[...] * xs
        y = y * _silu(ps_ref[rows, S_Z:S_Z + SSD_WIDTH])
        ms = jnp.mean(y * y, axis=-1, keepdims=True)
        o_ref[rows, :] = (y * lax.rsqrt(ms + EPS) * nw_ref[...]).astype(BF16)


def _ssd(ps, conv_w, conv_b, dt_bias, a_log, d_skip, norm_w, e64, e128):
    nblk = SEQ // TB_SCAN
    const = lambda b, i: (0, 0)
    return pl.pallas_call(
        _ssd_kernel,
        out_shape=jax.ShapeDtypeStruct((TOKENS, SSD_WIDTH), BF16),
        grid=(BATCH, nblk),
        in_specs=[
            pl.BlockSpec((TB_SCAN, SSD_COLS), lambda b, i: (b * nblk + i, 0)),
            pl.BlockSpec((SSD_CONV, SSD_CONV_DIM), const),
            pl.BlockSpec((1, SSD_CONV_DIM), const),
            pl.BlockSpec((1, LANES), const),
            pl.BlockSpec((1, LANES), const),
            pl.BlockSpec((1, SSD_WIDTH), const),
            pl.BlockSpec((1, SSD_WIDTH), const),
            pl.BlockSpec((LANES, SSD_WIDTH), const),
            pl.BlockSpec((LANES, SSD_HEADS * LANES), const),
        ],
        out_specs=pl.BlockSpec((TB_SCAN, SSD_WIDTH), lambda b, i: (b * nblk + i, 0)),
        scratch_shapes=[pltpu.VMEM((TB_SCAN + SSD_HALO, SSD_CONV_DIM), F32),
                        pltpu.VMEM((SSD_BC, SSD_WIDTH), F32)],
        compiler_params=_params("arbitrary", "arbitrary"),
        name="ssd",
    )(ps, conv_w, conv_b, dt_bias, a_log, d_skip, norm_w, e64, e128)


def _outproj_kernel(x_ref, og_ref, oy_ref, wo_ref, mod_ref, nw_ref, *rest, with_router):
    if with_router:
        rw_ref, xo_ref, h_ref, gates_ref, sel1_ref, sel2_ref, cnt_ref = rest
    else:
        xo_ref, h_ref = rest
    mod = mod_ref[0, 0]
    half = TM_PROJ // 2
    counts = jnp.zeros((1, LANES), F32)
    for r0 in (0, half):
        rows = slice(r0, r0 + half)
        mix = (_dot(og_ref[rows, :], wo_ref[0, 0:GLA_WIDTH, :])
               + _dot(oy_ref[rows, :], wo_ref[0, GLA_WIDTH:, :]))
        x_new = x_ref[rows, :] + mod[2:3] * mix
        xo_ref[rows, :] = x_new
        h = _norm_mod(x_new, nw_ref[...], mod[4:5], mod[3:4])
        h_ref[rows, :] = h.astype(BF16)
        if with_router:
            logits = _dot3(h, rw_ref[...])
            lane = lax.broadcasted_iota(jnp.int32, logits.shape, 1)
            lg = jnp.where(lane < N_EXPERTS, logits, -jnp.inf)
            v1 = jnp.max(lg, axis=-1, keepdims=True)
            i1 = jnp.min(jnp.where(lg == v1, lane, LANES), axis=-1, keepdims=True)
            lg2 = jnp.where(lane == i1, -jnp.inf, lg)
            v2 = jnp.max(lg2, axis=-1, keepdims=True)
            i2 = jnp.min(jnp.where(lg2 == v2, lane, LANES), axis=-1, keepdims=True)
            e2 = jnp.exp(v2 - v1)
            p1 = 1.0 / (1.0 + e2)
            p2 = e2 / (1.0 + e2)
            sel1 = jnp.where(lane == i1, 1.0, 0.0)
            sel2 = jnp.where(lane == i2, 1.0, 0.0)
            gates_ref[rows, :] = sel1 * p1 + sel2 * p2
            sel1_ref[rows, :] = sel1
            sel2_ref[rows, :] = sel2
            counts = counts + jnp.sum(sel1 + sel2, axis=0, keepdims=True)
    if with_router:
        cnt_ref[0] = jnp.broadcast_to(counts.astype(jnp.int32), (8, LANES))


def _outproj(x, og, oy, w_out, layer, mod_l, norm_w, router_w):
    tiles_per_batch = SEQ // TM_PROJ
    with_router = router_w is not None
    tile = lambda i: (i, 0)
    const = lambda i: (0, 0)
    in_specs = [
        pl.BlockSpec((TM_PROJ, D_MODEL), tile),
        pl.BlockSpec((TM_PROJ, GLA_WIDTH), tile),
        pl.BlockSpec((TM_PROJ, SSD_WIDTH), tile),
        pl.BlockSpec((1, D_MODEL, D_MODEL), lambda i: (layer, 0, 0)),
        pl.BlockSpec((1, 1, 6, D_MODEL), lambda i: (i // tiles_per_batch, 0, 0, 0)),
        pl.BlockSpec((1, D_MODEL), const),
    ]
    out_shape = [jax.ShapeDtypeStruct((TOKENS, D_MODEL), F32),
                 jax.ShapeDtypeStruct((TOKENS, D_MODEL), BF16)]
    out_specs = [pl.BlockSpec((TM_PROJ, D_MODEL), tile), pl.BlockSpec((TM_PROJ, D_MODEL), tile)]
    args = [x, og, oy, w_out, mod_l, norm_w]
    if with_router:
        in_specs.append(pl.BlockSpec((D_MODEL, LANES), const))
        for _ in range(3):
            out_shape.append(jax.ShapeDtypeStruct((TOKENS, LANES), F32))
            out_specs.append(pl.BlockSpec((TM_PROJ, LANES), tile))
        out_shape.append(jax.ShapeDtypeStruct((TOKENS // TM_PROJ, 8, LANES), jnp.int32))
        out_specs.append(pl.BlockSpec((1, 8, LANES), lambda i: (i, 0, 0)))
        args.append(router_w)
    return pl.pallas_call(
        functools.partial(_outproj_kernel, with_router=with_router),
        out_shape=tuple(out_shape),
        grid=(TOKENS // TM_PROJ,),
        in_specs=in_specs,
        out_specs=tuple(out_specs),
        compiler_params=_params("arbitrary"),
        name="outproj_router" if with_router else "outproj",
    )(*args)


def _swiglu_chunk(h, wg, wu, wd):
    a = (_silu(_dot(h, wg.astype(BF16))) * _dot(h, wu.astype(BF16))).astype(BF16)
    return _dot(a, wd.astype(BF16))


def _ffn_kernel(h_ref, x_ref, mod_ref, wg_ref, wu_ref, wd_ref, o_ref):
    @pl.when(pl.program_id(1) == 0)
    def _():
        o_ref[...] = x_ref[...]

    o_ref[...] += mod_ref[0, 0, 5:6, :] * _swiglu_chunk(h_ref[...], wg_ref[0], wu_ref[0], wd_ref[0])


def _ffn(h, x, mod_l, w_gate, w_up, w_down, layer_idx):
    tiles_per_batch = SEQ // TM_FFN
    tile = lambda i, j: (i, 0)
    return pl.pallas_call(
        _ffn_kernel,
        out_shape=jax.ShapeDtypeStruct((TOKENS, D_MODEL), F32),
        grid=(TOKENS // TM_FFN, N_FF_CHUNKS),
        in_specs=[pl.BlockSpec((TM_FFN, D_MODEL), tile),
                  pl.BlockSpec((TM_FFN, D_MODEL), tile),
                  pl.BlockSpec((1, 1, 6, D_MODEL), lambda i, j: (i // tiles_per_batch, 0, 0, 0)),
                  pl.BlockSpec((1, D_MODEL, TF_FFN), lambda i, j: (layer_idx, 0, j)),
                  pl.BlockSpec((1, D_MODEL, TF_FFN), lambda i, j: (layer_idx, 0, j)),
                  pl.BlockSpec((1, TF_FFN, D_MODEL), lambda i, j: (layer_idx, j, 0))],
        out_specs=pl.BlockSpec((TM_FFN, D_MODEL), tile),
        compiler_params=_params("arbitrary", "arbitrary"),
        name="dense_ffn",
    )(h, x, mod_l, w_gate, w_up, w_down)


def _moe_plan(cnt):
    seg_len = (cnt + SEG_ALIGN - 1) // SEG_ALIGN * SEG_ALIGN
    loc_off = jnp.cumsum(seg_len, axis=1) - seg_len
    n_rows = seg_len.sum(axis=0)
    region = (n_rows + MOE_TILE - 1) // MOE_TILE * MOE_TILE
    base = jnp.cumsum(region) - region
    seg_start = base[None, :] + jnp.cumsum(seg_len, axis=0) - seg_len
    tiles_e = region // MOE_TILE
    tile_end = jnp.cumsum(tiles_e)
    n_act = tile_end[-1]
    r = jnp.arange(MOE_MAX_TILES, dtype=jnp.int32)
    r_act = jnp.minimum(r, n_act - 1)
    tile_exp = jnp.sum(r_act[:, None] >= tile_end[None, :], axis=1).astype(jnp.int32)
    rows_left = n_rows[tile_exp] - (r_act - (tile_end - tiles_e)[tile_exp]) * MOE_TILE
    n_sub = jnp.clip((rows_left + MOE_SUB - 1) // MOE_SUB, 0, MOE_TILE // MOE_SUB)
    n_sub = jnp.where(r < n_act, n_sub, 0)
    i32 = lambda a: a.reshape(-1).astype(jnp.int32)
    return dict(seg=(i32(seg_start), i32(loc_off), i32(seg_len)),
                fill=(i32(base + n_rows), i32(region - n_rows), i32(n_act)),
                tiles=(tile_exp, i32(r_act), i32(n_sub)))


def _segment_dma(src, dst, src_off, dst_off, length, sem, wait, advance_src=True):
    off = jnp.int32(0)
    for size in SEG_PIECES:
        take = (length & size) != 0
        s0 = pl.multiple_of(src_off + off, SEG_ALIGN) if advance_src else src_off
        d0 = pl.multiple_of(dst_off + off, SEG_ALIGN)

        @pl.when(take)
        def _():
            cp = pltpu.make_async_copy(src.at[pl.ds(s0, size)], dst.at[pl.ds(d0, size)], sem)
            if wait:
                cp.wait()
            else:
                cp.start()

        off = off + jnp.where(take, size, 0)


def _local_rows(i, lo_ref, sel1, sel2):
    tm = sel1.shape[0]
    tr = lax.broadcasted_iota(jnp.int32, (tm, tm), 0)
    tc = lax.broadcasted_iota(jnp.int32, (tm, tm), 1)
    earlier = jnp.where(tr > tc, 1.0, 0.0).astype(BF16)
    lane = lax.broadcasted_iota(jnp.int32, (1, LANES), 1)
    loc = jnp.zeros((1, LANES), F32)
    for e in range(N_EXPERTS):
        loc = jnp.where(lane == e, lo_ref[i * N_EXPERTS + e].astype(F32), loc)
    pos = _dot(earlier, (sel1 + sel2).astype(BF16)) + loc
    return jnp.sum(sel1 * pos, axis=-1, keepdims=True), jnp.sum(sel2 * pos, axis=-1, keepdims=True)


def _moe_sort_kernel(ss_ref, lo_ref, ln_ref, fs_ref, fl_ref, na_ref, h_ref, sel1_ref, sel2_ref, xs_ref,
                     loc_scr, zero_scr, sem, fill_sem):
    i = pl.program_id(0)
    last = pl.num_programs(0) - 1
    slot = lax.rem(i, 2)
    pos1, pos2 = _local_rows(i, lo_ref, sel1_ref[...], sel2_ref[...])
    cid = lax.broadcasted_iota(jnp.int32, (TM_PROJ, MOE_LOC_ROWS), 1).astype(F32)
    pick = jnp.where(cid == pos1, 1.0, jnp.where(cid == pos2, 1.0, 0.0)).astype(BF16)
    loc_scr[slot] = _dot_tn(pick, h_ref[...]).astype(BF16)

    def copies(tile, buf, wait):
        for e in range(N_EXPERTS):
            k = tile * N_EXPERTS + e
            _segment_dma(loc_scr.at[buf], xs_ref, lo_ref[k], ss_ref[k], ln_ref[k], sem.at[buf], wait)

    copies(i, slot, False)

    @pl.when(i > 0)
    def _():
        copies(i - 1, 1 - slot, True)

    @pl.when(i == last)
    def _():
        copies(i, slot, True)
        zero_scr[...] = jnp.zeros_like(zero_scr)
        n_act = na_ref[0]

        def fill_tiles(wait):
            def body(r, carry):
                for part in range(MOE_TILE // FILL_ROWS):
                    d0 = pl.multiple_of(r * MOE_TILE + part * FILL_ROWS, FILL_ROWS)
                    cp = pltpu.make_async_copy(zero_scr, xs_ref.at[pl.ds(d0, FILL_ROWS)], fill_sem)
                    if wait:
                        cp.wait()
                    else:
                        cp.start()
                return carry
            lax.fori_loop(n_act, MOE_MAX_TILES, body, 0)

        for wait in (False, True):
            for e in range(N_EXPERTS):
                _segment_dma(zero_scr, xs_ref, 0, fs_ref[e], fl_ref[e], fill_sem, wait, advance_src=False)
            fill_tiles(wait)


def _moe_sort(plan, h, sel1, sel2):
    tile = lambda i, *_: (i, 0)
    return pl.pallas_call(
        _moe_sort_kernel,
        out_shape=jax.ShapeDtypeStruct((MOE_MAX_TILES * MOE_TILE, D_MODEL), BF16),
        grid_spec=pltpu.PrefetchScalarGridSpec(
            num_scalar_prefetch=6,
            grid=(TOKENS // TM_PROJ,),
            in_specs=[pl.BlockSpec((TM_PROJ, D_MODEL), tile),
                      pl.BlockSpec((TM_PROJ, LANES), tile),
                      pl.BlockSpec((TM_PROJ, LANES), tile)],
            out_specs=pl.BlockSpec(memory_space=pl.ANY),
            scratch_shapes=[pltpu.VMEM((2, MOE_LOC_ROWS, D_MODEL), BF16),
                            pltpu.VMEM((FILL_ROWS, D_MODEL), BF16),
                            pltpu.SemaphoreType.DMA((2,)),
                            pltpu.SemaphoreType.DMA],
        ),
        compiler_params=_params("arbitrary"),
        name="moe_sort",
    )(*plan["seg"], *plan["fill"], h, sel1, sel2)


def _moe_ffn_kernel(te_ref, ra_ref, ns_ref, xs_ref, wg_ref, wu_ref, wd_ref, o_ref, acc_ref):
    del te_ref, ra_ref
    r, j = pl.program_id(0), pl.program_id(1)
    n_sub = ns_ref[r]
    n_parts = MOE_TILE // MOE_SUB
    for used in range(1, n_parts + 1):
        rows = slice(0, used * MOE_SUB)

        @pl.when(n_sub == used)
        def _():
            @pl.when(j == 0)
            def _():
                acc_ref[rows, :] = jnp.zeros((used * MOE_SUB, D_MODEL), F32)

            acc_ref[rows, :] += _swiglu_chunk(xs_ref[rows, :], wg_ref[0, 0], wu_ref[0, 0], wd_ref[0, 0])

            @pl.when(j == N_FF_MOE - 1)
            def _():
                o_ref[rows, :] = acc_ref[rows, :].astype(BF16)

    for s in range(n_parts):
        @pl.when(jnp.logical_and(s >= n_sub, j == N_FF_MOE - 1))
        def _():
            o_ref[s * MOE_SUB:(s + 1) * MOE_SUB, :] = jnp.zeros((MOE_SUB, D_MODEL), BF16)


def _moe_ffn(plan, xs, w_gate, w_up, w_down, layer_idx):
    last = N_FF_MOE - 1
    rows = lambda r, j, te, ra, ns: (ra[r], 0)
    chunk = lambda r, j, ns: jnp.where(ns[r] > 0, j, last)
    return pl.pallas_call(
        _moe_ffn_kernel,
        out_shape=jax.ShapeDtypeStruct(xs.shape, BF16),
        grid_spec=pltpu.PrefetchScalarGridSpec(
            num_scalar_prefetch=3,
            grid=(MOE_MAX_TILES, N_FF_MOE),
            in_specs=[pl.BlockSpec((MOE_TILE, D_MODEL), rows),
                      pl.BlockSpec((1, 1, D_MODEL, TF_MOE),
                                   lambda r, j, te, ra, ns: (layer_idx, te[r], 0, chunk(r, j, ns))),
                      pl.BlockSpec((1, 1, D_MODEL, TF_MOE),
                                   lambda r, j, te, ra, ns: (layer_idx, te[r], 0, chunk(r, j, ns))),
                      pl.BlockSpec((1, 1, TF_MOE, D_MODEL),
                                   lambda r, j, te, ra, ns: (layer_idx, te[r], chunk(r, j, ns), 0))],
            out_specs=pl.BlockSpec((MOE_TILE, D_MODEL), lambda r, j, *_: (r, 0)),
            scratch_shapes=[pltpu.VMEM((MOE_TILE, D_MODEL), F32)],
        ),
        compiler_params=_params("arbitrary", "arbitrary"),
        name="moe_ffn",
    )(*plan["tiles"], xs, w_gate, w_up, w_down)


def _moe_combine_kernel(ss_ref, lo_ref, ln_ref, x_ref, mod_ref, gates_ref, sel1_ref, sel2_ref, *rest,
                        final_norm):
    if final_norm:
        fw_ref, ys_ref, o_ref, loc_scr, sem = rest
    else:
        ys_ref, o_ref, loc_scr, sem = rest
    i = pl.program_id(0)
    tm = TM_PROJ
    slot = lax.rem(i, 2)

    def fetch(tile, buf, wait):
        for e in range(N_EXPERTS):
            k = tile * N_EXPERTS + e
            _segment_dma(ys_ref, loc_scr.at[buf], ss_ref[k], lo_ref[k], ln_ref[k], sem.at[buf], wait)

    @pl.when(i == 0)
    def _():
        loc_scr[...] = jnp.zeros_like(loc_scr)
        fetch(0, 0, False)

    @pl.when(i + 1 < pl.num_programs(0))
    def _():
        fetch(i + 1, 1 - slot, False)

    sel1, sel2, gates = sel1_ref[...], sel2_ref[...], gates_ref[...]
    pos1, pos2 = _local_rows(i, lo_ref, sel1, sel2)
    p1 = jnp.sum(sel1 * gates, axis=-1, keepdims=True)
    p2 = jnp.sum(sel2 * gates, axis=-1, keepdims=True)
    cid = lax.broadcasted_iota(jnp.int32, (tm, MOE_LOC_ROWS), 1).astype(F32)
    pick1 = jnp.where(cid == pos1, 1.0, 0.0).astype(BF16)
    pick2 = jnp.where(cid == pos2, 1.0, 0.0).astype(BF16)
    fetch(i, slot, True)
    y = loc_scr[slot]
    ff = p1 * _dot(pick1, y) + p2 * _dot(pick2, y)
    out = x_ref[...] + mod_ref[0, 0, 5:6, :] * ff
    if final_norm:
        ms = jnp.mean(out * out, axis=-1, keepdims=True)
        out = out * lax.rsqrt(ms + EPS) * fw_ref[...]
    o_ref[...] = out


def _moe_combine(plan, x, mod_l, gates, sel1, sel2, ys, final_w=None):
    tiles_per_batch = SEQ // TM_PROJ
    tile = lambda i, *_: (i, 0)
    final_norm = final_w is not None
    in_specs = [pl.BlockSpec((TM_PROJ, D_MODEL), tile),
                pl.BlockSpec((1, 1, 6, D_MODEL), lambda i, *_: (i // tiles_per_batch, 0, 0, 0)),
                pl.BlockSpec((TM_PROJ, LANES), tile),
                pl.BlockSpec((TM_PROJ, LANES), tile),
                pl.BlockSpec((TM_PROJ, LANES), tile)]
    args = [x, mod_l, gates, sel1, sel2]
    if final_norm:
        in_specs.append(pl.BlockSpec((1, D_MODEL), lambda i, *_: (0, 0)))
        args.append(final_w)
    return pl.pallas_call(
        functools.partial(_moe_combine_kernel, final_norm=final_norm),
        out_shape=jax.ShapeDtypeStruct((TOKENS, D_MODEL), F32),
        grid_spec=pltpu.PrefetchScalarGridSpec(
            num_scalar_prefetch=3,
            grid=(TOKENS // TM_PROJ,),
            in_specs=in_specs + [pl.BlockSpec(memory_space=pl.ANY)],
            out_specs=pl.BlockSpec((TM_PROJ, D_MODEL), tile),
            scratch_shapes=[pltpu.VMEM((2, MOE_LOC_ROWS, D_MODEL), BF16), pltpu.SemaphoreType.DMA((2,))],
        ),
        compiler_params=_params("arbitrary"),
        name="moe_combine_norm" if final_norm else "moe_combine",
    )(*plan["seg"], *args, ys)


def _final_norm_kernel(x_ref, w_ref, o_ref):
    x = x_ref[...]
    ms = jnp.mean(x * x, axis=-1, keepdims=True)
    o_ref[...] = x * lax.rsqrt(ms + EPS) * w_ref[...]


def _final_norm(x, w):
    return pl.pallas_call(
        _final_norm_kernel,
        out_shape=jax.ShapeDtypeStruct((TOKENS, D_MODEL), F32),
        grid=(TOKENS // TM_FFN,),
        in_specs=[pl.BlockSpec((TM_FFN, D_MODEL), lambda i: (i, 0)),
                  pl.BlockSpec((1, D_MODEL), lambda i: (0, 0))],
        out_specs=pl.BlockSpec((TM_FFN, D_MODEL), lambda i: (i, 0)),
        compiler_params=_params("arbitrary"),
        name="final_norm",
    )(x, w)


def _pad_cols(a, width):
    return jnp.pad(a, [(0, 0)] * (a.ndim - 1) + [(0, width - a.shape[-1])])


def _split_w_in(w_in):
    ssd_start = G_A + GLA_LOWRANK
    return w_in[:, :, :GLA_COLS].astype(BF16), _pad_cols(w_in[:, :, ssd_start:], SSD_COLS).astype(BF16)


def kernel(x, c, ada_w, ada_b, norm1_w, w_in, gla_a_w, gla_a_b, gla_norm_w, conv_w, conv_b, dt_bias, a_log,
           d_skip, ssd_norm_w, w_out, norm2_w, ffn_w_gate, ffn_w_up, ffn_w_down, router_w, moe_w_gate,
           moe_w_up, moe_w_down, final_norm_w):
    xt = x.reshape(TOKENS, D_MODEL)
    c_pad = jnp.pad(c, ((0, 8 - BATCH), (0, 0)))
    mod = _adaln(c_pad, ada_w, ada_b)[:, :BATCH].reshape(DEPTH, BATCH, 6, D_MODEL)

    w_gla, w_ssd = _split_w_in(w_in)
    w_out_bf = w_out.astype(BF16)
    a_w_p = jnp.pad(gla_a_w, ((0, 0), (0, LANES - GLA_LOWRANK), (0, 0)))
    dtb_p = _pad_cols(dt_bias, LANES)
    alog_p = _pad_cols(a_log, LANES)
    dsk_p = jnp.repeat(d_skip, SSD_HEADDIM, axis=-1)
    rw_p = _pad_cols(router_w, LANES)
    lane_head = jnp.arange(LANES)[:, None]
    e64 = (lane_head == jnp.arange(SSD_WIDTH)[None, :] // SSD_HEADDIM).astype(BF16)
    e128 = (lane_head == jnp.arange(SSD_HEADS * LANES)[None, :] // LANES).astype(BF16)

    for l in range(DEPTH):
        mod_l = mod[l].reshape(BATCH, 1, 6, D_MODEL)
        pg, ps = _inproj(xt, mod_l, norm1_w[l][None], w_gla, w_ssd, l)
        og = _gla(pg, a_w_p[l], gla_a_b[l][None], gla_norm_w[l][None])
        oy = _ssd(ps, conv_w[l], conv_b[l][None], dtb_p[l][None], alog_p[l][None], dsk_p[l][None],
                  ssd_norm_w[l][None], e64, e128)
        i = l // 2
        if l % 2 == 0:
            xt, h2 = _outproj(xt, og, oy, w_out_bf, l, mod_l, norm2_w[l][None], None)
            xt = _ffn(h2, xt, mod_l, ffn_w_gate, ffn_w_up, ffn_w_down, i)
        else:
            xt, h2, gates, sel1, sel2, cnt = _outproj(xt, og, oy, w_out_bf, l, mod_l, norm2_w[l][None], rw_p[i])
            plan = _moe_plan(cnt[:, 0, :N_EXPERTS])
            xs = _moe_sort(plan, h2, sel1, sel2)
            ys = _moe_ffn(plan, xs, moe_w_gate, moe_w_up, moe_w_down, i)
            last = l == DEPTH - 1
            xt = _moe_combine(plan, xt, mod_l, gates, sel1, sel2, ys, final_norm_w[None] if last else None)
    if DEPTH % 2:
        xt = _final_norm(xt, final_norm_w[None])
    return xt.reshape(BATCH, SEQ, D_MODEL)
```

```python
import functools

import jax
import jax.numpy as jnp
from jax import lax
from jax.experimental import pallas as pl
from jax.experimental.pallas import tpu as pltpu

D_MODEL = 1024
BATCH = 2
SEQ = 8192
DEPTH = 4
TOKENS = BATCH * SEQ

GLA_HEADS = 4
GLA_DK = 64
GLA_DV = 128
GLA_QK = GLA_HEADS * GLA_DK
GLA_WIDTH = GLA_HEADS * GLA_DV
GLA_LOWRANK = 16
GLA_TAU = 16.0
GLA_CHUNK = 64

SSD_HEADS = 8
SSD_HEADDIM = 64
SSD_WIDTH = SSD_HEADS * SSD_HEADDIM
SSD_GROUPS = 2
SSD_STATE = 64
SSD_BC = SSD_GROUPS * SSD_STATE
SSD_CONV = 4
SSD_CONV_DIM = SSD_WIDTH + 2 * SSD_BC
SSD_CHUNK = 128

D_FF = 3584
N_EXPERTS = 8
EPS = 1e-6

LANES = 128
GLA_COLS = 2 * GLA_QK + 2 * GLA_WIDTH + LANES
SSD_COLS = 2 * SSD_WIDTH + 2 * SSD_BC + LANES
G_Q, G_K, G_V, G_G, G_A = 0, GLA_QK, 2 * GLA_QK, 2 * GLA_QK + GLA_WIDTH, 2 * GLA_QK + 2 * GLA_WIDTH
S_Z, S_X, S_DT = 0, SSD_WIDTH, SSD_WIDTH + SSD_CONV_DIM

TM_IN = 1024
TM_PROJ = 512
TB_SCAN = 512
TM_FFN = 1024
TF_FFN = 512
GLA_GROUP = 4
SSD_HALO = 8
N_FF_CHUNKS = D_FF // TF_FFN
VMEM_LIMIT = 56 * 1024 * 1024

TOP_K = 2
SEG_ALIGN = 16
SEG_PIECES = tuple(TM_PROJ >> s for s in range(6))
MOE_TILE = 1024
MOE_SUB = 256
FILL_ROWS = SEG_PIECES[0]
TF_MOE = TF_FFN
N_FF_MOE = D_FF // TF_MOE
N_TILES = TOKENS // TM_PROJ
MOE_LOC_ROWS = -(-(TOP_K * TM_PROJ + N_EXPERTS * (SEG_ALIGN - 1)) // LANES) * LANES
MOE_MAX_TILES = (TOP_K * TOKENS + N_TILES * N_EXPERTS * (SEG_ALIGN - 1) + N_EXPERTS * (MOE_TILE - 1)) // MOE_TILE

F32 = jnp.float32
BF16 = jnp.bfloat16


def _dot(a, b):
    return jnp.dot(a, b, preferred_element_type=F32)


def _dot_nt(a, b):
    return lax.dot_general(a, b, (((1,), (1,)), ((), ())), preferred_element_type=F32)


def _dot_tn(a, b):
    return lax.dot_general(a, b, (((0,), (0,)), ((), ())), preferred_element_type=F32)


def _split(a):
    hi = a.astype(BF16)
    lo = (a - hi.astype(F32)).astype(BF16)
    return hi, lo


def _dot3(a, b):
    a_hi, a_lo = _split(a)
    b_hi, b_lo = _split(b)
    return _dot(a_hi, b_hi) + _dot(a_lo, b_hi) + _dot(a_hi, b_lo)


def _dot_exact_rhs(a, b_bf16):
    a_hi, a_lo = _split(a)
    return _dot(a_hi, b_bf16) + _dot(a_lo, b_bf16)


def _dot_exact_lhs(a_bf16, b):
    b_hi, b_lo = _split(b)
    return _dot(a_bf16, b_hi) + _dot(a_bf16, b_lo)


def _silu(x):
    return x * (0.5 * jnp.tanh(0.5 * x) + 0.5)


def _softplus(x):
    return jnp.maximum(x, 0.0) + jnp.log1p(jnp.exp(-jnp.abs(x)))


def _norm_mod(x, w, scale, shift):
    ms = jnp.mean(x * x, axis=-1, keepdims=True)
    return (x * lax.rsqrt(ms + EPS) * w) * (1.0 + scale) + shift


def _params(*sem):
    return pltpu.CompilerParams(dimension_semantics=sem, vmem_limit_bytes=VMEM_LIMIT)


def _adaln_kernel(c_ref, w_ref, b_ref, o_ref):
    s = _silu(c_ref[...])
    o_ref[0] = _dot3(s, w_ref[0]) + b_ref[0]


def _adaln(c_pad, ada_w, ada_b):
    n_col = 6 * D_MODEL // D_MODEL
    return pl.pallas_call(
        _adaln_kernel,
        out_shape=jax.ShapeDtypeStruct((DEPTH, 8, 6 * D_MODEL), F32),
        grid=(DEPTH, n_col),
        in_specs=[
            pl.BlockSpec((8, D_MODEL), lambda l, j: (0, 0)),
            pl.BlockSpec((1, D_MODEL, D_MODEL), lambda l, j: (l, 0, j)),
            pl.BlockSpec((1, 1, D_MODEL), lambda l, j: (l, 0, j)),
        ],
        out_specs=pl.BlockSpec((1, 8, D_MODEL), lambda l, j: (l, 0, j)),
        compiler_params=_params("arbitrary", "arbitrary"),
        name="adaln",
    )(c_pad, ada_w, ada_b.reshape(DEPTH, 1, 6 * D_MODEL))


def _inproj_kernel(x_ref, mod_ref, nw_ref, wg_ref, ws_ref, og_ref, os_ref):
    mod = mod_ref[0, 0]
    half = TM_IN // 2
    for r0 in (0, half):
        rows = slice(r0, r0 + half)
        h = _norm_mod(x_ref[rows, :], nw_ref[...], mod[1:2], mod[0:1]).astype(BF16)
        for w_ref, o_ref in ((wg_ref, og_ref), (ws_ref, os_ref)):
            n_cols = o_ref.shape[1]
            for c0 in range(0, n_cols, 512):
                c1 = min(c0 + 512, n_cols)
                o_ref[rows, c0:c1] = _dot(h, w_ref[0, :, c0:c1])


def _inproj(x, mod_l, norm_w, w_gla, w_ssd, layer):
    tiles_per_batch = SEQ // TM_IN
    return pl.pallas_call(
        _inproj_kernel,
        out_shape=(jax.ShapeDtypeStruct((TOKENS, GLA_COLS), F32),
                   jax.ShapeDtypeStruct((TOKENS, SSD_COLS), F32)),
        grid=(TOKENS // TM_IN,),
        in_specs=[
            pl.BlockSpec((TM_IN, D_MODEL), lambda i: (i, 0)),
            pl.BlockSpec((1, 1, 6, D_MODEL), lambda i: (i // tiles_per_batch, 0, 0, 0)),
            pl.BlockSpec((1, D_MODEL), lambda i: (0, 0)),
            pl.BlockSpec((1, D_MODEL, GLA_COLS), lambda i: (layer, 0, 0)),
            pl.BlockSpec((1, D_MODEL, SSD_COLS), lambda i: (layer, 0, 0)),
        ],
        out_specs=(pl.BlockSpec((TM_IN, GLA_COLS), lambda i: (i, 0)),
                   pl.BlockSpec((TM_IN, SSD_COLS), lambda i: (i, 0))),
        compiler_params=_params("arbitrary"),
        name="inproj",
    )(x, mod_l, norm_w, w_gla, w_ssd)


def _gla_kernel(pg_ref, aw_ref, ab_ref, nw_ref, o_ref, la_scr, st_scr):
    @pl.when(pl.program_id(1) == 0)
    def _():
        st_scr[...] = jnp.zeros_like(st_scr)

    pre = _dot3(pg_ref[:, G_A:G_A + LANES], aw_ref[...]) + ab_ref[...]
    la_scr[...] = -_softplus(-pre) * (1.0 / GLA_TAU)

    c, gr = GLA_CHUNK, GLA_GROUP * GLA_CHUNK
    row = lax.broadcasted_iota(jnp.int32, (gr, gr), 0)
    col = lax.broadcasted_iota(jnp.int32, (gr, gr), 1)
    causal = jnp.logical_and(row >= col, jnp.bitwise_xor(row, col) < c)
    tril = jnp.where(causal, 1.0, 0.0).astype(BF16)
    causal2 = jnp.concatenate([causal, causal], axis=0)
    low_q = lax.broadcasted_iota(jnp.int32, (gr, LANES), 1) < GLA_DK
    low_s = lax.broadcasted_iota(jnp.int32, (GLA_DV, LANES), 1) < GLA_DK
    nw = nw_ref[...]

    n_groups = TB_SCAN // gr
    pairs = range(GLA_HEADS // 2)
    units = [(gi, p) for gi in range(n_groups) for p in pairs]
    rows_of = lambda gi: slice(gi * gr, (gi + 1) * gr)
    b_all = [_dot_exact_lhs(tril, la_scr[rows_of(gi), :]) for gi in range(n_groups)]
    b_last, qm, ke, k_tail, v_pair = {}, {}, {}, {}, {}
    for u in units:
        gi, p = u
        b = b_all[gi][:, p * LANES:(p + 1) * LANES]
        b_last[u] = [b[(ci + 1) * c - 1:(ci + 1) * c, :] for ci in range(GLA_GROUP)]
        b_last_rows = jnp.concatenate([jnp.broadcast_to(bl, (c, LANES)) for bl in b_last[u]], axis=0)
        q = pg_ref[rows_of(gi), G_Q + p * LANES:G_Q + (p + 1) * LANES] * (GLA_DK ** -0.5)
        k = pg_ref[rows_of(gi), G_K + p * LANES:G_K + (p + 1) * LANES]
        qe = q * jnp.exp(b)
        ke[u] = (k * jnp.exp(-b)).astype(BF16)
        k_tail[u] = (k * jnp.exp(b_last_rows - b)).astype(BF16)
        v_pair[u] = pg_ref[rows_of(gi), G_V + 2 * p * GLA_DV:G_V + 2 * (p + 1) * GLA_DV].astype(BF16)
        qm[u] = jnp.concatenate([jnp.where(low_q, qe, 0.0), jnp.where(low_q, 0.0, qe)], axis=0).astype(BF16)
    att = {u: jnp.where(causal2, _dot_nt(qm[u], ke[u]), 0.0).astype(BF16) for u in units}
    contrib = {u: [_dot_tn(v_pair[u][ci * c:(ci + 1) * c], k_tail[u][ci * c:(ci + 1) * c])
                   for ci in range(GLA_GROUP)] for u in units}
    o_intra = {u: [_dot(att[u][j * gr:(j + 1) * gr], v_pair[u][:, j * GLA_DV:(j + 1) * GLA_DV]) for j in range(2)]
               for u in units}
    o_inter = {u: [] for u in units}
    for p in pairs:
        st = st_scr[p]
        for gi in range(n_groups):
            u = (gi, p)
            for ci in range(GLA_GROUP):
                q_ci = jnp.concatenate([qm[u][ci * c:(ci + 1) * c], qm[u][gr + ci * c:gr + (ci + 1) * c]], axis=0)
                o_inter[u].append(_dot_nt(q_ci, st.astype(BF16)))
                st = (st * jnp.exp(b_last[u][ci])
                      + jnp.where(low_s, contrib[u][ci][:GLA_DV], contrib[u][ci][GLA_DV:]))
        st_scr[p] = st
    for u in units:
        gi, p = u
        for j in range(2):
            h = 2 * p + j
            o = o_intra[u][j] + jnp.concatenate([oi[j * c:(j + 1) * c] for oi in o_inter[u]], axis=0)
            ms = jnp.mean(o * o, axis=-1, keepdims=True)
            g = pg_ref[rows_of(gi), G_G + h * GLA_DV:G_G + (h + 1) * GLA_DV]
            o = (o * lax.rsqrt(ms + EPS) * nw) * _silu(g)
            o_ref[rows_of(gi), h * GLA_DV:(h + 1) * GLA_DV] = o.astype(BF16)


def _gla(pg, a_w, a_b, norm_w):
    nblk = SEQ // TB_SCAN
    return pl.pallas_call(
        _gla_kernel,
        out_shape=jax.ShapeDtypeStruct((TOKENS, GLA_WIDTH), BF16),
        grid=(BATCH, nblk),
        in_specs=[
            pl.BlockSpec((TB_SCAN, GLA_COLS), lambda b, i: (b * nblk + i, 0)),
            pl.BlockSpec((LANES, GLA_QK), lambda b, i: (0, 0)),
            pl.BlockSpec((1, GLA_QK), lambda b, i: (0, 0)),
            pl.BlockSpec((1, GLA_DV), lambda b, i: (0, 0)),
        ],
        out_specs=pl.BlockSpec((TB_SCAN, GLA_WIDTH), lambda b, i: (b * nblk + i, 0)),
        scratch_shapes=[pltpu.VMEM((TB_SCAN, GLA_QK), F32),
                        pltpu.VMEM((GLA_HEADS // 2, GLA_DV, LANES), F32)],
        compiler_params=_params("arbitrary", "arbitrary"),
        name="gla",
    )(pg, a_w, a_b, norm_w)


def _ssd_kernel(ps_ref, cw_ref, cb_ref, dtb_ref, alog_ref, dsk_ref, nw_ref, e64_ref, e128_ref,
                o_ref, xbc_scr, st_scr):
    first = pl.program_id(1) == 0
    halo = SSD_HALO
    c = SSD_CHUNK

    @pl.when(first)
    def _():
        xbc_scr[0:halo, :] = jnp.zeros((halo, SSD_CONV_DIM), F32)
        st_scr[...] = jnp.zeros_like(st_scr)

    @pl.when(jnp.logical_not(first))
    def _():
        xbc_scr[0:halo, :] = xbc_scr[TB_SCAN:TB_SCAN + halo, :]

    xbc_scr[halo:halo + TB_SCAN, :] = ps_ref[:, S_X:S_X + SSD_CONV_DIM]

    row = lax.broadcasted_iota(jnp.int32, (c, c), 0)
    col = lax.broadcasted_iota(jnp.int32, (c, c), 1)
    causal = row >= col
    tril = jnp.where(causal, 1.0, 0.0).astype(BF16)
    head_lane = col < SSD_HEADS
    low_half = col < SSD_HEADDIM
    st_row = lax.broadcasted_iota(jnp.int32, (c, SSD_WIDTH), 0)
    st_col = lax.broadcasted_iota(jnp.int32, (c, SSD_WIDTH), 1)
    blockdiag = (st_row < SSD_STATE) == (st_col < SSD_WIDTH // SSD_GROUPS)
    a_neg = -jnp.exp(alog_ref[...])
    heads_per_group = SSD_HEADS // SSD_GROUPS

    for ci in range(TB_SCAN // c):
        rows = slice(ci * c, (ci + 1) * c)
        win = xbc_scr[ci * c:(ci + 1) * c + halo, :]
        conv = cw_ref[0:1, :] * win
        for k in range(1, SSD_CONV):
            conv = pltpu.roll(conv, 1, axis=0) + cw_ref[k:k + 1, :] * win
        act = _silu(conv[halo:, :] + cb_ref[...])
        dt = jnp.where(head_lane, _softplus(ps_ref[rows, S_DT:S_DT + LANES] + dtb_ref[...]), 0.0)
        cum = _dot_exact_lhs(tril, dt * a_neg)
        cum_t = cum.T
        cum64 = _dot_exact_rhs(cum, e64_ref[...])
        dt64 = _dot_exact_rhs(dt, e64_ref[...])
        cum_col = _dot_exact_rhs(cum, e128_ref[...])
        cl64 = cum64[c - 1:c, :]
        xs = act[:, 0:SSD_WIDTH]
        bm = act[:, SSD_WIDTH:SSD_WIDTH + SSD_BC].astype(BF16)
        cm = act[:, SSD_WIDTH + SSD_BC:SSD_CONV_DIM]
        xdt = xs * dt64
        xdt_bf = xdt.astype(BF16)
        y_parts = []
        for g in range(SSD_GROUPS):
            cm_g = jnp.where(low_half if g == 0 else jnp.logical_not(low_half), cm, 0.0).astype(BF16)
            scores = _dot_nt(cm_g, bm)
            for pp in range(heads_per_group // 2):
                p = g * (heads_per_group // 2) + pp
                ys = []
                for j in range(2):
                    h = 2 * p + j
                    seg = cum_col[:, h * LANES:(h + 1) * LANES] - cum_t[h:h + 1, :]
                    decay = jnp.exp(jnp.where(causal, seg, -jnp.inf))
                    ys.append(_dot((scores * decay).astype(BF16), xdt_bf[:, p * LANES:(p + 1) * LANES]))
                y_parts.append(jnp.where(low_half, ys[0], ys[1]))
        y = jnp.concatenate(y_parts, axis=1)
        st = st_scr[...]
        y = y + _dot(cm.astype(BF16), st.astype(BF16)) * jnp.exp(cum64)
        contrib = _dot_tn(bm, (xdt * jnp.exp(cl64 - cum64)).astype(BF16))
        st_scr[...] = st * jnp.exp(cl64) + jnp.where(blockdiag, contrib, 0.0)
        y = y + dsk_ref[...] * xs
        y = y * _silu(ps_ref[rows, S_Z:S_Z + SSD_WIDTH])
        ms = jnp.mean(y * y, axis=-1, keepdims=True)
        o_ref[rows, :] = (y * lax.rsqrt(ms + EPS) * nw_ref[...]).astype(BF16)


def _ssd(ps, conv_w, conv_b, dt_bias, a_log, d_skip, norm_w, e64, e128):
    nblk = SEQ // TB_SCAN
    const = lambda b, i: (0, 0)
    return pl.pallas_call(
        _ssd_kernel,
        out_shape=jax.ShapeDtypeStruct((TOKENS, SSD_WIDTH), BF16),
        grid=(BATCH, nblk),
        in_specs=[
            pl.BlockSpec((TB_SCAN, SSD_COLS), lambda b, i: (b * nblk + i, 0)),
            pl.BlockSpec((SSD_CONV, SSD_CONV_DIM), const),
            pl.BlockSpec((1, SSD_CONV_DIM), const),
            pl.BlockSpec((1, LANES), const),
            pl.BlockSpec((1, LANES), const),
            pl.BlockSpec((1, SSD_WIDTH), const),
            pl.BlockSpec((1, SSD_WIDTH), const),
            pl.BlockSpec((LANES, SSD_WIDTH), const),
            pl.BlockSpec((LANES, SSD_HEADS * LANES), const),
        ],
        out_specs=pl.BlockSpec((TB_SCAN, SSD_WIDTH), lambda b, i: (b * nblk + i, 0)),
        scratch_shapes=[pltpu.VMEM((TB_SCAN + SSD_HALO, SSD_CONV_DIM), F32),
                        pltpu.VMEM((SSD_BC, SSD_WIDTH), F32)],
        compiler_params=_params("arbitrary", "arbitrary"),
        name="ssd",
    )(ps, conv_w, conv_b, dt_bias, a_log, d_skip, norm_w, e64, e128)


def _outproj_kernel(x_ref, og_ref, oy_ref, wo_ref, mod_ref, nw_ref, *rest, with_router):
    if with_router:
        rw_ref, xo_ref, h_ref, gates_ref, sel1_ref, sel2_ref, cnt_ref = rest
    else:
        xo_ref, h_ref = rest
    mod = mod_ref[0, 0]
    half = TM_PROJ // 2
    counts = jnp.zeros((1, LANES), F32)
    for r0 in (0, half):
        rows = slice(r0, r0 + half)
        mix = (_dot(og_ref[rows, :], wo_ref[0, 0:GLA_WIDTH, :])
               + _dot(oy_ref[rows, :], wo_ref[0, GLA_WIDTH:, :]))
        x_new = x_ref[rows, :] + mod[2:3] * mix
        xo_ref[rows, :] = x_new
        h = _norm_mod(x_new, nw_ref[...], mod[4:5], mod[3:4])
        h_ref[rows, :] = h.astype(BF16)
        if with_router:
            logits = _dot3(h, rw_ref[...])
            lane = lax.broadcasted_iota(jnp.int32, logits.shape, 1)
            lg = jnp.where(lane < N_EXPERTS, logits, -jnp.inf)
            v1 = jnp.max(lg, axis=-1, keepdims=True)
            i1 = jnp.min(jnp.where(lg == v1, lane, LANES), axis=-1, keepdims=True)
            lg2 = jnp.where(lane == i1, -jnp.inf, lg)
            v2 = jnp.max(lg2, axis=-1, keepdims=True)
            i2 = jnp.min(jnp.where(lg2 == v2, lane, LANES), axis=-1, keepdims=True)
            e2 = jnp.exp(v2 - v1)
            p1 = 1.0 / (1.0 + e2)
            p2 = e2 / (1.0 + e2)
            sel1 = jnp.where(lane == i1, 1.0, 0.0)
            sel2 = jnp.where(lane == i2, 1.0, 0.0)
            gates_ref[rows, :] = sel1 * p1 + sel2 * p2
            sel1_ref[rows, :] = sel1
            sel2_ref[rows, :] = sel2
            counts = counts + jnp.sum(sel1 + sel2, axis=0, keepdims=True)
    if with_router:
        cnt_ref[0] = jnp.broadcast_to(counts.astype(jnp.int32), (8, LANES))


def _outproj(x, og, oy, w_out, layer, mod_l, norm_w, router_w):
    tiles_per_batch = SEQ // TM_PROJ
    with_router = router_w is not None
    tile = lambda i: (i, 0)
    const = lambda i: (0, 0)
    in_specs = [
        pl.BlockSpec((TM_PROJ, D_MODEL), tile),
        pl.BlockSpec((TM_PROJ, GLA_WIDTH), tile),
        pl.BlockSpec((TM_PROJ, SSD_WIDTH), tile),
        pl.BlockSpec((1, D_MODEL, D_MODEL), lambda i: (layer, 0, 0)),
        pl.BlockSpec((1, 1, 6, D_MODEL), lambda i: (i // tiles_per_batch, 0, 0, 0)),
        pl.BlockSpec((1, D_MODEL), const),
    ]
    out_shape = [jax.ShapeDtypeStruct((TOKENS, D_MODEL), F32),
                 jax.ShapeDtypeStruct((TOKENS, D_MODEL), BF16)]
    out_specs = [pl.BlockSpec((TM_PROJ, D_MODEL), tile), pl.BlockSpec((TM_PROJ, D_MODEL), tile)]
    args = [x, og, oy, w_out, mod_l, norm_w]
    if with_router:
        in_specs.append(pl.BlockSpec((D_MODEL, LANES), const))
        for _ in range(3):
            out_shape.append(jax.ShapeDtypeStruct((TOKENS, LANES), F32))
            out_specs.append(pl.BlockSpec((TM_PROJ, LANES), tile))
        out_shape.append(jax.ShapeDtypeStruct((TOKENS // TM_PROJ, 8, LANES), jnp.int32))
        out_specs.append(pl.BlockSpec((1, 8, LANES), lambda i: (i, 0, 0)))
        args.append(router_w)
    return pl.pallas_call(
        functools.partial(_outproj_kernel, with_router=with_router),
        out_shape=tuple(out_shape),
        grid=(TOKENS // TM_PROJ,),
        in_specs=in_specs,
        out_specs=tuple(out_specs),
        compiler_params=_params("arbitrary"),
        name="outproj_router" if with_router else "outproj",
    )(*args)


def _swiglu_chunk(h, wg, wu, wd):
    a = (_silu(_dot(h, wg.astype(BF16))) * _dot(h, wu.astype(BF16))).astype(BF16)
    return _dot(a, wd.astype(BF16))


def _ffn_kernel(x_ref, og_ref, oy_ref, wo_ref, mod_ref, nw_ref, wg_ref, wu_ref, wd_ref, o_ref, h_scr):
    mod = mod_ref[0, 0]

    @pl.when(pl.program_id(1) == 0)
    def _():
        half = TM_FFN // 2
        for r0 in (0, half):
            rows = slice(r0, r0 + half)
            mix = (_dot(og_ref[rows, :], wo_ref[0, 0:GLA_WIDTH, :])
                   + _dot(oy_ref[rows, :], wo_ref[0, GLA_WIDTH:, :]))
            x_new = x_ref[rows, :] + mod[2:3] * mix
            o_ref[rows, :] = x_new
            h_scr[rows, :] = _norm_mod(x_new, nw_ref[...], mod[4:5], mod[3:4]).astype(BF16)

    o_ref[...] += mod[5:6] * _swiglu_chunk(h_scr[...], wg_ref[0], wu_ref[0], wd_ref[0])


def _outproj_ffn(x, og, oy, w_out, layer, mod_l, norm_w, w_gate, w_up, w_down, layer_idx):
    tiles_per_batch = SEQ // TM_FFN
    tile = lambda i, j: (i, 0)
    return pl.pallas_call(
        _ffn_kernel,
        out_shape=jax.ShapeDtypeStruct((TOKENS, D_MODEL), F32),
        grid=(TOKENS // TM_FFN, N_FF_CHUNKS),
        in_specs=[pl.BlockSpec((TM_FFN, D_MODEL), tile),
                  pl.BlockSpec((TM_FFN, GLA_WIDTH), tile),
                  pl.BlockSpec((TM_FFN, SSD_WIDTH), tile),
                  pl.BlockSpec((1, D_MODEL, D_MODEL), lambda i, j: (layer, 0, 0)),
                  pl.BlockSpec((1, 1, 6, D_MODEL), lambda i, j: (i // tiles_per_batch, 0, 0, 0)),
                  pl.BlockSpec((1, D_MODEL), lambda i, j: (0, 0)),
                  pl.BlockSpec((1, D_MODEL, TF_FFN), lambda i, j: (layer_idx, 0, j)),
                  pl.BlockSpec((1, D_MODEL, TF_FFN), lambda i, j: (layer_idx, 0, j)),
                  pl.BlockSpec((1, TF_FFN, D_MODEL), lambda i, j: (layer_idx, j, 0))],
        out_specs=pl.BlockSpec((TM_FFN, D_MODEL), tile),
        scratch_shapes=[pltpu.VMEM((TM_FFN, D_MODEL), BF16)],
        compiler_params=_params("arbitrary", "arbitrary"),
        name="outproj_ffn",
    )(x, og, oy, w_out, mod_l, norm_w, w_gate, w_up, w_down)


def _moe_plan(cnt):
    seg_len = (cnt + SEG_ALIGN - 1) // SEG_ALIGN * SEG_ALIGN
    loc_off = jnp.cumsum(seg_len, axis=1) - seg_len
    n_rows = seg_len.sum(axis=0)
    region = (n_rows + MOE_TILE - 1) // MOE_TILE * MOE_TILE
    base = jnp.cumsum(region) - region
    seg_start = base[None, :] + jnp.cumsum(seg_len, axis=0) - seg_len
    tiles_e = region // MOE_TILE
    tile_end = jnp.cumsum(tiles_e)
    n_act = tile_end[-1]
    r = jnp.arange(MOE_MAX_TILES, dtype=jnp.int32)
    r_act = jnp.minimum(r, n_act - 1)
    tile_exp = jnp.sum(r_act[:, None] >= tile_end[None, :], axis=1).astype(jnp.int32)
    rows_left = n_rows[tile_exp] - (r_act - (tile_end - tiles_e)[tile_exp]) * MOE_TILE
    n_sub = jnp.clip((rows_left + MOE_SUB - 1) // MOE_SUB, 0, MOE_TILE // MOE_SUB)
    n_sub = jnp.where(r < n_act, n_sub, 0)
    i32 = lambda a: a.reshape(-1).astype(jnp.int32)
    return dict(seg=(i32(seg_start), i32(loc_off), i32(seg_len)),
                fill=(i32(base + n_rows), i32(region - n_rows), i32(n_act)),
                tiles=(tile_exp, i32(r_act), i32(n_sub)))


def _segment_dma(src, dst, src_off, dst_off, length, sem, wait, advance_src=True):
    off = jnp.int32(0)
    for size in SEG_PIECES:
        take = (length & size) != 0
        s0 = pl.multiple_of(src_off + off, SEG_ALIGN) if advance_src else src_off
        d0 = pl.multiple_of(dst_off + off, SEG_ALIGN)

        @pl.when(take)
        def _():
            cp = pltpu.make_async_copy(src.at[pl.ds(s0, size)], dst.at[pl.ds(d0, size)], sem)
            if wait:
                cp.wait()
            else:
                cp.start()

        off = off + jnp.where(take, size, 0)


def _local_rows(i, lo_ref, sel1, sel2):
    tm = sel1.shape[0]
    tr = lax.broadcasted_iota(jnp.int32, (tm, tm), 0)
    tc = lax.broadcasted_iota(jnp.int32, (tm, tm), 1)
    earlier = jnp.where(tr > tc, 1.0, 0.0).astype(BF16)
    lane = lax.broadcasted_iota(jnp.int32, (1, LANES), 1)
    loc = jnp.zeros((1, LANES), F32)
    for e in range(N_EXPERTS):
        loc = jnp.where(lane == e, lo_ref[i * N_EXPERTS + e].astype(F32), loc)
    pos = _dot(earlier, (sel1 + sel2).astype(BF16)) + loc
    return jnp.sum(sel1 * pos, axis=-1, keepdims=True), jnp.sum(sel2 * pos, axis=-1, keepdims=True)


def _moe_sort_kernel(ss_ref, lo_ref, ln_ref, fs_ref, fl_ref, na_ref, h_ref, sel1_ref, sel2_ref, xs_ref,
                     loc_scr, zero_scr, sem, fill_sem):
    i = pl.program_id(0)
    last = pl.num_programs(0) - 1
    slot = lax.rem(i, 2)
    pos1, pos2 = _local_rows(i, lo_ref, sel1_ref[...], sel2_ref[...])
    cid = lax.broadcasted_iota(jnp.int32, (TM_PROJ, MOE_LOC_ROWS), 1).astype(F32)
    pick = jnp.where(cid == pos1, 1.0, jnp.where(cid == pos2, 1.0, 0.0)).astype(BF16)
    loc_scr[slot] = _dot_tn(pick, h_ref[...]).astype(BF16)

    def copies(tile, buf, wait):
        for e in range(N_EXPERTS):
            k = tile * N_EXPERTS + e
            _segment_dma(loc_scr.at[buf], xs_ref, lo_ref[k], ss_ref[k], ln_ref[k], sem.at[buf], wait)

    copies(i, slot, False)

    @pl.when(i > 0)
    def _():
        copies(i - 1, 1 - slot, True)

    @pl.when(i == last)
    def _():
        copies(i, slot, True)
        zero_scr[...] = jnp.zeros_like(zero_scr)
        n_act = na_ref[0]

        def fill_tiles(wait):
            def body(r, carry):
                for part in range(MOE_TILE // FILL_ROWS):
                    d0 = pl.multiple_of(r * MOE_TILE + part * FILL_ROWS, FILL_ROWS)
                    cp = pltpu.make_async_copy(zero_scr, xs_ref.at[pl.ds(d0, FILL_ROWS)], fill_sem)
                    if wait:
                        cp.wait()
                    else:
                        cp.start()
                return carry
            lax.fori_loop(n_act, MOE_MAX_TILES, body, 0)

        for wait in (False, True):
            for e in range(N_EXPERTS):
                _segment_dma(zero_scr, xs_ref, 0, fs_ref[e], fl_ref[e], fill_sem, wait, advance_src=False)
            fill_tiles(wait)


def _moe_sort(plan, h, sel1, sel2):
    tile = lambda i, *_: (i, 0)
    return pl.pallas_call(
        _moe_sort_kernel,
        out_shape=jax.ShapeDtypeStruct((MOE_MAX_TILES * MOE_TILE, D_MODEL), BF16),
        grid_spec=pltpu.PrefetchScalarGridSpec(
            num_scalar_prefetch=6,
            grid=(TOKENS // TM_PROJ,),
            in_specs=[pl.BlockSpec((TM_PROJ, D_MODEL), tile),
                      pl.BlockSpec((TM_PROJ, LANES), tile),
                      pl.BlockSpec((TM_PROJ, LANES), tile)],
            out_specs=pl.BlockSpec(memory_space=pl.ANY),
            scratch_shapes=[pltpu.VMEM((2, MOE_LOC_ROWS, D_MODEL), BF16),
                            pltpu.VMEM((FILL_ROWS, D_MODEL), BF16),
                            pltpu.SemaphoreType.DMA((2,)),
                            pltpu.SemaphoreType.DMA],
        ),
        compiler_params=_params("arbitrary"),
        name="moe_sort",
    )(*plan["seg"], *plan["fill"], h, sel1, sel2)


def _moe_ffn_kernel(te_ref, ra_ref, ns_ref, xs_ref, wg_ref, wu_ref, wd_ref, o_ref, acc_ref):
    del te_ref, ra_ref
    r, j = pl.program_id(0), pl.program_id(1)
    n_sub = ns_ref[r]
    n_parts = MOE_TILE // MOE_SUB
    for used in range(1, n_parts + 1):
        rows = slice(0, used * MOE_SUB)

        @pl.when(n_sub == used)
        def _():
            @pl.when(j == 0)
            def _():
                acc_ref[rows, :] = jnp.zeros((used * MOE_SUB, D_MODEL), F32)

            acc_ref[rows, :] += _swiglu_chunk(xs_ref[rows, :], wg_ref[0, 0], wu_ref[0, 0], wd_ref[0, 0])

            @pl.when(j == N_FF_MOE - 1)
            def _():
                o_ref[rows, :] = acc_ref[rows, :].astype(BF16)

    for s in range(n_parts):
        @pl.when(jnp.logical_and(s >= n_sub, j == N_FF_MOE - 1))
        def _():
            o_ref[s * MOE_SUB:(s + 1) * MOE_SUB, :] = jnp.zeros((MOE_SUB, D_MODEL), BF16)


def _moe_ffn(plan, xs, w_gate, w_up, w_down, layer_idx):
    last = N_FF_MOE - 1
    rows = lambda r, j, te, ra, ns: (ra[r], 0)
    chunk = lambda r, j, ns: jnp.where(ns[r] > 0, j, last)
    return pl.pallas_call(
        _moe_ffn_kernel,
        out_shape=jax.ShapeDtypeStruct(xs.shape, BF16),
        grid_spec=pltpu.PrefetchScalarGridSpec(
            num_scalar_prefetch=3,
            grid=(MOE_MAX_TILES, N_FF_MOE),
            in_specs=[pl.BlockSpec((MOE_TILE, D_MODEL), rows),
                      pl.BlockSpec((1, 1, D_MODEL, TF_MOE),
                                   lambda r, j, te, ra, ns: (layer_idx, te[r], 0, chunk(r, j, ns))),
                      pl.BlockSpec((1, 1, D_MODEL, TF_MOE),
                                   lambda r, j, te, ra, ns: (layer_idx, te[r], 0, chunk(r, j, ns))),
                      pl.BlockSpec((1, 1, TF_MOE, D_MODEL),
                                   lambda r, j, te, ra, ns: (layer_idx, te[r], chunk(r, j, ns), 0))],
            out_specs=pl.BlockSpec((MOE_TILE, D_MODEL), lambda r, j, *_: (r, 0)),
            scratch_shapes=[pltpu.VMEM((MOE_TILE, D_MODEL), F32)],
        ),
        compiler_params=_params("arbitrary", "arbitrary"),
        name="moe_ffn",
    )(*plan["tiles"], xs, w_gate, w_up, w_down)


def _moe_combine_kernel(ss_ref, lo_ref, ln_ref, x_ref, mod_ref, gates_ref, sel1_ref, sel2_ref, *rest,
                        final_norm):
    if final_norm:
        fw_ref, ys_ref, o_ref, loc_scr, sem = rest
    else:
        ys_ref, o_ref, loc_scr, sem = rest
    i = pl.program_id(0)
    tm = TM_PROJ
    slot = lax.rem(i, 2)

    def fetch(tile, buf, wait):
        for e in range(N_EXPERTS):
            k = tile * N_EXPERTS + e
            _segment_dma(ys_ref, loc_scr.at[buf], ss_ref[k], lo_ref[k], ln_ref[k], sem.at[buf], wait)

    @pl.when(i == 0)
    def _():
        loc_scr[...] = jnp.zeros_like(loc_scr)
        fetch(0, 0, False)

    @pl.when(i + 1 < pl.num_programs(0))
    def _():
        fetch(i + 1, 1 - slot, False)

    sel1, sel2, gates = sel1_ref[...], sel2_ref[...], gates_ref[...]
    pos1, pos2 = _local_rows(i, lo_ref, sel1, sel2)
    p1 = jnp.sum(sel1 * gates, axis=-1, keepdims=True)
    p2 = jnp.sum(sel2 * gates, axis=-1, keepdims=True)
    cid = lax.broadcasted_iota(jnp.int32, (tm, MOE_LOC_ROWS), 1).astype(F32)
    pick1 = jnp.where(cid == pos1, 1.0, 0.0).astype(BF16)
    pick2 = jnp.where(cid == pos2, 1.0, 0.0).astype(BF16)
    fetch(i, slot, True)
    y = loc_scr[slot]
    ff = p1 * _dot(pick1, y) + p2 * _dot(pick2, y)
    out = x_ref[...] + mod_ref[0, 0, 5:6, :] * ff
    if final_norm:
        ms = jnp.mean(out * out, axis=-1, keepdims=True)
        out = out * lax.rsqrt(ms + EPS) * fw_ref[...]
    o_ref[...] = out


def _moe_combine(plan, x, mod_l, gates, sel1, sel2, ys, final_w=None):
    tiles_per_batch = SEQ // TM_PROJ
    tile = lambda i, *_: (i, 0)
    final_norm = final_w is not None
    in_specs = [pl.BlockSpec((TM_PROJ, D_MODEL), tile),
                pl.BlockSpec((1, 1, 6, D_MODEL), lambda i, *_: (i // tiles_per_batch, 0, 0, 0)),
                pl.BlockSpec((TM_PROJ, LANES), tile),
                pl.BlockSpec((TM_PROJ, LANES), tile),
                pl.BlockSpec((TM_PROJ, LANES), tile)]
    args = [x, mod_l, gates, sel1, sel2]
    if final_norm:
        in_specs.append(pl.BlockSpec((1, D_MODEL), lambda i, *_: (0, 0)))
        args.append(final_w)
    return pl.pallas_call(
        functools.partial(_moe_combine_kernel, final_norm=final_norm),
        out_shape=jax.ShapeDtypeStruct((TOKENS, D_MODEL), F32),
        grid_spec=pltpu.PrefetchScalarGridSpec(
            num_scalar_prefetch=3,
            grid=(TOKENS // TM_PROJ,),
            in_specs=in_specs + [pl.BlockSpec(memory_space=pl.ANY)],
            out_specs=pl.BlockSpec((TM_PROJ, D_MODEL), tile),
            scratch_shapes=[pltpu.VMEM((2, MOE_LOC_ROWS, D_MODEL), BF16), pltpu.SemaphoreType.DMA((2,))],
        ),
        compiler_params=_params("arbitrary"),
        name="moe_combine_norm" if final_norm else "moe_combine",
    )(*plan["seg"], *args, ys)


def _final_norm_kernel(x_ref, w_ref, o_ref):
    x = x_ref[...]
    ms = jnp.mean(x * x, axis=-1, keepdims=True)
    o_ref[...] = x * lax.rsqrt(ms + EPS) * w_ref[...]


def _final_norm(x, w):
    return pl.pallas_call(
        _final_norm_kernel,
        out_shape=jax.ShapeDtypeStruct((TOKENS, D_MODEL), F32),
        grid=(TOKENS // TM_FFN,),
        in_specs=[pl.BlockSpec((TM_FFN, D_MODEL), lambda i: (i, 0)),
                  pl.BlockSpec((1, D_MODEL), lambda i: (0, 0))],
        out_specs=pl.BlockSpec((TM_FFN, D_MODEL), lambda i: (i, 0)),
        compiler_params=_params("arbitrary"),
        name="final_norm",
    )(x, w)


def _pad_cols(a, width):
    return jnp.pad(a, [(0, 0)] * (a.ndim - 1) + [(0, width - a.shape[-1])])


def _split_w_in(w_in):
    ssd_start = G_A + GLA_LOWRANK
    return w_in[:, :, :GLA_COLS].astype(BF16), _pad_cols(w_in[:, :, ssd_start:], SSD_COLS).astype(BF16)


def kernel(x, c, ada_w, ada_b, norm1_w, w_in, gla_a_w, gla_a_b, gla_norm_w, conv_w, conv_b, dt_bias, a_log,
           d_skip, ssd_norm_w, w_out, norm2_w, ffn_w_gate, ffn_w_up, ffn_w_down, router_w, moe_w_gate,
           moe_w_up, moe_w_down, final_norm_w):
    xt = x.reshape(TOKENS, D_MODEL)
    c_pad = jnp.pad(c, ((0, 8 - BATCH), (0, 0)))
    mod = _adaln(c_pad, ada_w, ada_b)[:, :BATCH].reshape(DEPTH, BATCH, 6, D_MODEL)

    w_gla, w_ssd = _split_w_in(w_in)
    w_out_bf = w_out.astype(BF16)
    a_w_p = jnp.pad(gla_a_w, ((0, 0), (0, LANES - GLA_LOWRANK), (0, 0)))
    dtb_p = _pad_cols(dt_bias, LANES)
    alog_p = _pad_cols(a_log, LANES)
    dsk_p = jnp.repeat(d_skip, SSD_HEADDIM, axis=-1)
    rw_p = _pad_cols(router_w, LANES)
    lane_head = jnp.arange(LANES)[:, None]
    e64 = (lane_head == jnp.arange(SSD_WIDTH)[None, :] // SSD_HEADDIM).astype(BF16)
    e128 = (lane_head == jnp.arange(SSD_HEADS * LANES)[None, :] // LANES).astype(BF16)

    for l in range(DEPTH):
        mod_l = mod[l].reshape(BATCH, 1, 6, D_MODEL)
        pg, ps = _inproj(xt, mod_l, norm1_w[l][None], w_gla, w_ssd, l)
        og = _gla(pg, a_w_p[l], gla_a_b[l][None], gla_norm_w[l][None])
        oy = _ssd(ps, conv_w[l], conv_b[l][None], dtb_p[l][None], alog_p[l][None], dsk_p[l][None],
                  ssd_norm_w[l][None], e64, e128)
        i = l // 2
        if l % 2 == 0:
            xt = _outproj_ffn(xt, og, oy, w_out_bf, l, mod_l, norm2_w[l][None], ffn_w_gate, ffn_w_up, ffn_w_down, i)
        else:
            xt, h2, gates, sel1, sel2, cnt = _outproj(xt, og, oy, w_out_bf, l, mod_l, norm2_w[l][None], rw_p[i])
            plan = _moe_plan(cnt[:, 0, :N_EXPERTS])
            xs = _moe_sort(plan, h2, sel1, sel2)
            ys = _moe_ffn(plan, xs, moe_w_gate, moe_w_up, moe_w_down, i)
            last = l == DEPTH - 1
            xt = _moe_combine(plan, xt, mod_l, gates, sel1, sel2, ys, final_norm_w[None] if last else None)
    if DEPTH % 2:
        xt = _final_norm(xt, final_norm_w[None])
    return xt.reshape(BATCH, SEQ, D_MODEL)
```

```python
import functools

import jax
import jax.numpy as jnp
from jax import lax
from jax.experimental import pallas as pl
from jax.experimental.pallas import tpu as pltpu

D_MODEL = 1024
BATCH = 2
SEQ = 8192
DEPTH = 4
TOKENS = BATCH * SEQ

GLA_HEADS = 4
GLA_DK = 64
GLA_DV = 128
GLA_QK = GLA_HEADS * GLA_DK
GLA_WIDTH = GLA_HEADS * GLA_DV
GLA_LOWRANK = 16
GLA_TAU = 16.0
GLA_CHUNK = 64

SSD_HEADS = 8
SSD_HEADDIM = 64
SSD_WIDTH = SSD_HEADS * SSD_HEADDIM
SSD_GROUPS = 2
SSD_STATE = 64
SSD_BC = SSD_GROUPS * SSD_STATE
SSD_CONV = 4
SSD_CONV_DIM = SSD_WIDTH + 2 * SSD_BC
SSD_CHUNK = 128

D_FF = 3584
N_EXPERTS = 8
EPS = 1e-6

LANES = 128
GLA_COLS = 2 * GLA_QK + 2 * GLA_WIDTH + LANES
SSD_COLS = 2 * SSD_WIDTH + 2 * SSD_BC + LANES
G_Q, G_K, G_V, G_G, G_A = 0, GLA_QK, 2 * GLA_QK, 2 * GLA_QK + GLA_WIDTH, 2 * GLA_QK + 2 * GLA_WIDTH
S_Z, S_X, S_DT = 0, SSD_WIDTH, SSD_WIDTH + SSD_CONV_DIM

TM_IN = 1024
TM_PROJ = 512
TB_SCAN = 512
TM_FFN = 1024
TF_FFN = 512
GLA_GROUP = 2
SSD_HALO = 8
N_FF_CHUNKS = D_FF // TF_FFN
VMEM_LIMIT = 56 * 1024 * 1024

TOP_K = 2
SEG_ALIGN = 16
SEG_PIECES = tuple(TM_PROJ >> s for s in range(6))
MOE_TILE = 1024
MOE_SUB = 256
FILL_ROWS = SEG_PIECES[0]
TF_MOE = TF_FFN
N_FF_MOE = D_FF // TF_MOE
N_TILES = TOKENS // TM_PROJ
MOE_LOC_ROWS = -(-(TOP_K * TM_PROJ + N_EXPERTS * (SEG_ALIGN - 1)) // LANES) * LANES
MOE_MAX_TILES = (TOP_K * TOKENS + N_TILES * N_EXPERTS * (SEG_ALIGN - 1) + N_EXPERTS * (MOE_TILE - 1)) // MOE_TILE

F32 = jnp.float32
BF16 = jnp.bfloat16


def _dot(a, b):
    return jnp.dot(a, b, preferred_element_type=F32)


def _dot_nt(a, b):
    return lax.dot_general(a, b, (((1,), (1,)), ((), ())), preferred_element_type=F32)


def _dot_tn(a, b):
    return lax.dot_general(a, b, (((0,), (0,)), ((), ())), preferred_element_type=F32)


def _split(a):
    hi = a.astype(BF16)
    lo = (a - hi.astype(F32)).astype(BF16)
    return hi, lo


def _dot3(a, b):
    a_hi, a_lo = _split(a)
    b_hi, b_lo = _split(b)
    return _dot(a_hi, b_hi) + _dot(a_lo, b_hi) + _dot(a_hi, b_lo)


def _dot_exact_rhs(a, b_bf16):
    a_hi, a_lo = _split(a)
    return _dot(a_hi, b_bf16) + _dot(a_lo, b_bf16)


def _dot_exact_lhs(a_bf16, b):
    b_hi, b_lo = _split(b)
    return _dot(a_bf16, b_hi) + _dot(a_bf16, b_lo)


def _silu(x):
    return x * (0.5 * jnp.tanh(0.5 * x) + 0.5)


def _softplus(x):
    return jnp.maximum(x, 0.0) + jnp.log1p(jnp.exp(-jnp.abs(x)))


def _norm_mod(x, w, scale, shift):
    ms = jnp.mean(x * x, axis=-1, keepdims=True)
    return (x * lax.rsqrt(ms + EPS) * w) * (1.0 + scale) + shift


def _params(*sem):
    return pltpu.CompilerParams(dimension_semantics=sem, vmem_limit_bytes=VMEM_LIMIT)


def _adaln_kernel(c_ref, w_ref, b_ref, o_ref):
    s = _silu(c_ref[...])
    o_ref[0] = _dot3(s, w_ref[0]) + b_ref[0]


def _adaln(c_pad, ada_w, ada_b):
    n_col = 6 * D_MODEL // D_MODEL
    return pl.pallas_call(
        _adaln_kernel,
        out_shape=jax.ShapeDtypeStruct((DEPTH, 8, 6 * D_MODEL), F32),
        grid=(DEPTH, n_col),
        in_specs=[
            pl.BlockSpec((8, D_MODEL), lambda l, j: (0, 0)),
            pl.BlockSpec((1, D_MODEL, D_MODEL), lambda l, j: (l, 0, j)),
            pl.BlockSpec((1, 1, D_MODEL), lambda l, j: (l, 0, j)),
        ],
        out_specs=pl.BlockSpec((1, 8, D_MODEL), lambda l, j: (l, 0, j)),
        compiler_params=_params("arbitrary", "arbitrary"),
        name="adaln",
    )(c_pad, ada_w, ada_b.reshape(DEPTH, 1, 6 * D_MODEL))


def _inproj_kernel(x_ref, mod_ref, nw_ref, wg_ref, ws_ref, og_ref, os_ref):
    mod = mod_ref[0, 0]
    half = TM_IN // 2
    for r0 in (0, half):
        rows = slice(r0, r0 + half)
        h = _norm_mod(x_ref[rows, :], nw_ref[...], mod[1:2], mod[0:1]).astype(BF16)
        for w_ref, o_ref in ((wg_ref, og_ref), (ws_ref, os_ref)):
            n_cols = o_ref.shape[1]
            for c0 in range(0, n_cols, 512):
                c1 = min(c0 + 512, n_cols)
                o_ref[rows, c0:c1] = _dot(h, w_ref[0, :, c0:c1])


def _inproj(x, mod_l, norm_w, w_gla, w_ssd, layer):
    tiles_per_batch = SEQ // TM_IN
    return pl.pallas_call(
        _inproj_kernel,
        out_shape=(jax.ShapeDtypeStruct((TOKENS, GLA_COLS), F32),
                   jax.ShapeDtypeStruct((TOKENS, SSD_COLS), F32)),
        grid=(TOKENS // TM_IN,),
        in_specs=[
            pl.BlockSpec((TM_IN, D_MODEL), lambda i: (i, 0)),
            pl.BlockSpec((1, 1, 6, D_MODEL), lambda i: (i // tiles_per_batch, 0, 0, 0)),
            pl.BlockSpec((1, D_MODEL), lambda i: (0, 0)),
            pl.BlockSpec((1, D_MODEL, GLA_COLS), lambda i: (layer, 0, 0)),
            pl.BlockSpec((1, D_MODEL, SSD_COLS), lambda i: (layer, 0, 0)),
        ],
        out_specs=(pl.BlockSpec((TM_IN, GLA_COLS), lambda i: (i, 0)),
                   pl.BlockSpec((TM_IN, SSD_COLS), lambda i: (i, 0))),
        compiler_params=_params("arbitrary"),
        name="inproj",
    )(x, mod_l, norm_w, w_gla, w_ssd)


def _gla_kernel(pg_ref, aw_ref, ab_ref, nw_ref, o_ref, la_scr, st_scr):
    @pl.when(pl.program_id(1) == 0)
    def _():
        st_scr[...] = jnp.zeros_like(st_scr)

    pre = _dot3(pg_ref[:, G_A:G_A + LANES], aw_ref[...]) + ab_ref[...]
    la_scr[...] = -_softplus(-pre) * (1.0 / GLA_TAU)

    c, gr = GLA_CHUNK, GLA_GROUP * GLA_CHUNK
    row = lax.broadcasted_iota(jnp.int32, (gr, gr), 0)
    col = lax.broadcasted_iota(jnp.int32, (gr, gr), 1)
    causal = jnp.logical_and(row >= col, jnp.bitwise_xor(row, col) < c)
    tril = jnp.where(causal, 1.0, 0.0).astype(BF16)
    causal2 = jnp.concatenate([causal, causal], axis=0)
    low_q = lax.broadcasted_iota(jnp.int32, (gr, LANES), 1) < GLA_DK
    low_s = lax.broadcasted_iota(jnp.int32, (GLA_DV, LANES), 1) < GLA_DK
    nw = nw_ref[...]

    n_groups = TB_SCAN // gr
    pairs = range(GLA_HEADS // 2)
    units = [(gi, p) for gi in range(n_groups) for p in pairs]
    rows_of = lambda gi: slice(gi * gr, (gi + 1) * gr)
    b_all = [_dot_exact_lhs(tril, la_scr[rows_of(gi), :]) for gi in range(n_groups)]
    b_last, qm, ke, k_tail, v_pair = {}, {}, {}, {}, {}
    for u in units:
        gi, p = u
        b = b_all[gi][:, p * LANES:(p + 1) * LANES]
        b_last[u] = [b[(ci + 1) * c - 1:(ci + 1) * c, :] for ci in range(GLA_GROUP)]
        b_last_rows = jnp.concatenate([jnp.broadcast_to(bl, (c, LANES)) for bl in b_last[u]], axis=0)
        q = pg_ref[rows_of(gi), G_Q + p * LANES:G_Q + (p + 1) * LANES] * (GLA_DK ** -0.5)
        k = pg_ref[rows_of(gi), G_K + p * LANES:G_K + (p + 1) * LANES]
        qe = q * jnp.exp(b)
        ke[u] = (k * jnp.exp(-b)).astype(BF16)
        k_tail[u] = (k * jnp.exp(b_last_rows - b)).astype(BF16)
        v_pair[u] = pg_ref[rows_of(gi), G_V + 2 * p * GLA_DV:G_V + 2 * (p + 1) * GLA_DV].astype(BF16)
        qm[u] = jnp.concatenate([jnp.where(low_q, qe, 0.0), jnp.where(low_q, 0.0, qe)], axis=0).astype(BF16)
    att = {u: jnp.where(causal2, _dot_nt(qm[u], ke[u]), 0.0).astype(BF16) for u in units}
    contrib = {u: [_dot_tn(v_pair[u][ci * c:(ci + 1) * c], k_tail[u][ci * c:(ci + 1) * c])
                   for ci in range(GLA_GROUP)] for u in units}
    o_intra = {u: [_dot(att[u][j * gr:(j + 1) * gr], v_pair[u][:, j * GLA_DV:(j + 1) * GLA_DV]) for j in range(2)]
               for u in units}
    o_inter = {u: [] for u in units}
    for p in pairs:
        st = st_scr[p]
        for gi in range(n_groups):
            u = (gi, p)
            for ci in range(GLA_GROUP):
                q_ci = jnp.concatenate([qm[u][ci * c:(ci + 1) * c], qm[u][gr + ci * c:gr + (ci + 1) * c]], axis=0)
                o_inter[u].append(_dot_nt(q_ci, st.astype(BF16)))
                st = (st * jnp.exp(b_last[u][ci])
                      + jnp.where(low_s, contrib[u][ci][:GLA_DV], contrib[u][ci][GLA_DV:]))
        st_scr[p] = st
    for u in units:
        gi, p = u
        for j in range(2):
            h = 2 * p + j
            o = o_intra[u][j] + jnp.concatenate([oi[j * c:(j + 1) * c] for oi in o_inter[u]], axis=0)
            ms = jnp.mean(o * o, axis=-1, keepdims=True)
            g = pg_ref[rows_of(gi), G_G + h * GLA_DV:G_G + (h + 1) * GLA_DV]
            o = (o * lax.rsqrt(ms + EPS) * nw) * _silu(g)
            o_ref[rows_of(gi), h * GLA_DV:(h + 1) * GLA_DV] = o.astype(BF16)


def _gla(pg, a_w, a_b, norm_w):
    nblk = SEQ // TB_SCAN
    return pl.pallas_call(
        _gla_kernel,
        out_shape=jax.ShapeDtypeStruct((TOKENS, GLA_WIDTH), BF16),
        grid=(BATCH, nblk),
        in_specs=[
            pl.BlockSpec((TB_SCAN, GLA_COLS), lambda b, i: (b * nblk + i, 0)),
            pl.BlockSpec((LANES, GLA_QK), lambda b, i: (0, 0)),
            pl.BlockSpec((1, GLA_QK), lambda b, i: (0, 0)),
            pl.BlockSpec((1, GLA_DV), lambda b, i: (0, 0)),
        ],
        out_specs=pl.BlockSpec((TB_SCAN, GLA_WIDTH), lambda b, i: (b * nblk + i, 0)),
        scratch_shapes=[pltpu.VMEM((TB_SCAN, GLA_QK), F32),
                        pltpu.VMEM((GLA_HEADS // 2, GLA_DV, LANES), F32)],
        compiler_params=_params("arbitrary", "arbitrary"),
        name="gla",
    )(pg, a_w, a_b, norm_w)


def _ssd_kernel(ps_ref, cw_ref, cb_ref, dtb_ref, alog_ref, dsk_ref, nw_ref, e64_ref, e128_ref,
                o_ref, xbc_scr, st_scr):
    first = pl.program_id(1) == 0
    halo = SSD_HALO
    c = SSD_CHUNK

    @pl.when(first)
    def _():
        xbc_scr[0:halo, :] = jnp.zeros((halo, SSD_CONV_DIM), F32)
        st_scr[...] = jnp.zeros_like(st_scr)

    @pl.when(jnp.logical_not(first))
    def _():
        xbc_scr[0:halo, :] = xbc_scr[TB_SCAN:TB_SCAN + halo, :]

    xbc_scr[halo:halo + TB_SCAN, :] = ps_ref[:, S_X:S_X + SSD_CONV_DIM]

    row = lax.broadcasted_iota(jnp.int32, (c, c), 0)
    col = lax.broadcasted_iota(jnp.int32, (c, c), 1)
    causal = row >= col
    tril = jnp.where(causal, 1.0, 0.0).astype(BF16)
    head_lane = col < SSD_HEADS
    low_half = col < SSD_HEADDIM
    st_row = lax.broadcasted_iota(jnp.int32, (c, SSD_WIDTH), 0)
    st_col = lax.broadcasted_iota(jnp.int32, (c, SSD_WIDTH), 1)
    blockdiag = (st_row < SSD_STATE) == (st_col < SSD_WIDTH // SSD_GROUPS)
    a_neg = -jnp.exp(alog_ref[...])
    heads_per_group = SSD_HEADS // SSD_GROUPS

    for ci in range(TB_SCAN // c):
        rows = slice(ci * c, (ci + 1) * c)
        win = xbc_scr[ci * c:(ci + 1) * c + halo, :]
        conv = cw_ref[0:1, :] * win
        for k in range(1, SSD_CONV):
            conv = pltpu.roll(conv, 1, axis=0) + cw_ref[k:k + 1, :] * win
        act = _silu(conv[halo:, :] + cb_ref[...])
        dt = jnp.where(head_lane, _softplus(ps_ref[rows, S_DT:S_DT + LANES] + dtb_ref[...]), 0.0)
        cum = _dot_exact_lhs(tril, dt * a_neg)
        cum_t = cum.T
        cum64 = _dot_exact_rhs(cum, e64_ref[...])
        dt64 = _dot_exact_rhs(dt, e64_ref[...])
        cum_col = _dot_exact_rhs(cum, e128_ref[...])
        cl64 = cum64[c - 1:c, :]
        xs = act[:, 0:SSD_WIDTH]
        bm = act[:, SSD_WIDTH:SSD_WIDTH + SSD_BC].astype(BF16)
        cm = act[:, SSD_WIDTH + SSD_BC:SSD_CONV_DIM]
        xdt = xs * dt64
        xdt_bf = xdt.astype(BF16)
        y_parts = []
        for g in range(SSD_GROUPS):
            cm_g = jnp.where(low_half if g == 0 else jnp.logical_not(low_half), cm, 0.0).astype(BF16)
            scores = _dot_nt(cm_g, bm)
            for pp in range(heads_per_group // 2):
                p = g * (heads_per_group // 2) + pp
                ys = []
                for j in range(2):
                    h = 2 * p + j
                    seg = cum_col[:, h * LANES:(h + 1) * LANES] - cum_t[h:h + 1, :]
                    decay = jnp.exp(jnp.where(causal, seg, -jnp.inf))
                    ys.append(_dot((scores * decay).astype(BF16), xdt_bf[:, p * LANES:(p + 1) * LANES]))
                y_parts.append(jnp.where(low_half, ys[0], ys[1]))
        y = jnp.concatenate(y_parts, axis=1)
        st = st_scr[...]
        y = y + _dot(cm.astype(BF16), st.astype(BF16)) * jnp.exp(cum64)
        contrib = _dot_tn(bm, (xdt * jnp.exp(cl64 - cum64)).astype(BF16))
        st_scr[...] = st * jnp.exp(cl64) + jnp.where(blockdiag, contrib, 0.0)
        y = y + dsk_ref[...] * xs
        y = y * _silu(ps_ref[rows, S_Z:S_Z + SSD_WIDTH])
        ms = jnp.mean(y * y, axis=-1, keepdims=True)
        o_ref[rows, :] = (y * lax.rsqrt(ms + EPS) * nw_ref[...]).astype(BF16)


def _ssd(ps, conv_w, conv_b, dt_bias, a_log, d_skip, norm_w, e64, e128):
    nblk = SEQ // TB_SCAN
    const = lambda b, i: (0, 0)
    return pl.pallas_call(
        _ssd_kernel,
        out_shape=jax.ShapeDtypeStruct((TOKENS, SSD_WIDTH), BF16),
        grid=(BATCH, nblk),
        in_specs=[
            pl.BlockSpec((TB_SCAN, SSD_COLS), lambda b, i: (b * nblk + i, 0)),
            pl.BlockSpec((SSD_CONV, SSD_CONV_DIM), const),
            pl.BlockSpec((1, SSD_CONV_DIM), const),
            pl.BlockSpec((1, LANES), const),
            pl.BlockSpec((1, LANES), const),
            pl.BlockSpec((1, SSD_WIDTH), const),
            pl.BlockSpec((1, SSD_WIDTH), const),
            pl.BlockSpec((LANES, SSD_WIDTH), const),
            pl.BlockSpec((LANES, SSD_HEADS * LANES), const),
        ],
        out_specs=pl.BlockSpec((TB_SCAN, SSD_WIDTH), lambda b, i: (b * nblk + i, 0)),
        scratch_shapes=[pltpu.VMEM((TB_SCAN + SSD_HALO, SSD_CONV_DIM), F32),
                        pltpu.VMEM((SSD_BC, SSD_WIDTH), F32)],
        compiler_params=_params("arbitrary", "arbitrary"),
        name="ssd",
    )(ps, conv_w, conv_b, dt_bias, a_log, d_skip, norm_w, e64, e128)


def _outproj_kernel(x_ref, og_ref, oy_ref, wo_ref, mod_ref, nw_ref, *rest, with_router):
    if with_router:
        rw_ref, xo_ref, h_ref, gates_ref, sel1_ref, sel2_ref, cnt_ref = rest
    else:
        xo_ref, h_ref = rest
    mod = mod_ref[0, 0]
    half = TM_PROJ // 2
    counts = jnp.zeros((1, LANES), F32)
    for r0 in (0, half):
        rows = slice(r0, r0 + half)
        mix = (_dot(og_ref[rows, :], wo_ref[0, 0:GLA_WIDTH, :])
               + _dot(oy_ref[rows, :], wo_ref[0, GLA_WIDTH:, :]))
        x_new = x_ref[rows, :] + mod[2:3] * mix
        xo_ref[rows, :] = x_new
        h = _norm_mod(x_new, nw_ref[...], mod[4:5], mod[3:4])
        h_ref[rows, :] = h.astype(BF16)
        if with_router:
            logits = _dot3(h, rw_ref[...])
            lane = lax.broadcasted_iota(jnp.int32, logits.shape, 1)
            lg = jnp.where(lane < N_EXPERTS, logits, -jnp.inf)
            v1 = jnp.max(lg, axis=-1, keepdims=True)
            i1 = jnp.min(jnp.where(lg == v1, lane, LANES), axis=-1, keepdims=True)
            lg2 = jnp.where(lane == i1, -jnp.inf, lg)
            v2 = jnp.max(lg2, axis=-1, keepdims=True)
            i2 = jnp.min(jnp.where(lg2 == v2, lane, LANES), axis=-1, keepdims=True)
            e2 = jnp.exp(v2 - v1)
            p1 = 1.0 / (1.0 + e2)
            p2 = e2 / (1.0 + e2)
            sel1 = jnp.where(lane == i1, 1.0, 0.0)
            sel2 = jnp.where(lane == i2, 1.0, 0.0)
            gates_ref[rows, :] = sel1 * p1 + sel2 * p2
            sel1_ref[rows, :] = sel1
            sel2_ref[rows, :] = sel2
            counts = counts + jnp.sum(sel1 + sel2, axis=0, keepdims=True)
    if with_router:
        cnt_ref[0] = jnp.broadcast_to(counts.astype(jnp.int32), (8, LANES))


def _outproj(x, og, oy, w_out, layer, mod_l, norm_w, router_w):
    tiles_per_batch = SEQ // TM_PROJ
    with_router = router_w is not None
    tile = lambda i: (i, 0)
    const = lambda i: (0, 0)
    in_specs = [
        pl.BlockSpec((TM_PROJ, D_MODEL), tile),
        pl.BlockSpec((TM_PROJ, GLA_WIDTH), tile),
        pl.BlockSpec((TM_PROJ, SSD_WIDTH), tile),
        pl.BlockSpec((1, D_MODEL, D_MODEL), lambda i: (layer, 0, 0)),
        pl.BlockSpec((1, 1, 6, D_MODEL), lambda i: (i // tiles_per_batch, 0, 0, 0)),
        pl.BlockSpec((1, D_MODEL), const),
    ]
    out_shape = [jax.ShapeDtypeStruct((TOKENS, D_MODEL), F32),
                 jax.ShapeDtypeStruct((TOKENS, D_MODEL), BF16)]
    out_specs = [pl.BlockSpec((TM_PROJ, D_MODEL), tile), pl.BlockSpec((TM_PROJ, D_MODEL), tile)]
    args = [x, og, oy, w_out, mod_l, norm_w]
    if with_router:
        in_specs.append(pl.BlockSpec((D_MODEL, LANES), const))
        for _ in range(3):
            out_shape.append(jax.ShapeDtypeStruct((TOKENS, LANES), F32))
            out_specs.append(pl.BlockSpec((TM_PROJ, LANES), tile))
        out_shape.append(jax.ShapeDtypeStruct((TOKENS // TM_PROJ, 8, LANES), jnp.int32))
        out_specs.append(pl.BlockSpec((1, 8, LANES), lambda i: (i, 0, 0)))
        args.append(router_w)
    return pl.pallas_call(
        functools.partial(_outproj_kernel, with_router=with_router),
        out_shape=tuple(out_shape),
        grid=(TOKENS // TM_PROJ,),
        in_specs=in_specs,
        out_specs=tuple(out_specs),
        compiler_params=_params("arbitrary"),
        name="outproj_router" if with_router else "outproj",
    )(*args)


def _swiglu_chunk(h, wg, wu, wd):
    a = (_silu(_dot(h, wg.astype(BF16))) * _dot(h, wu.astype(BF16))).astype(BF16)
    return _dot(a, wd.astype(BF16))


def _ffn_kernel(x_ref, og_ref, oy_ref, wo_ref, mod_ref, nw_ref, wg_ref, wu_ref, wd_ref, o_ref, h_scr):
    mod = mod_ref[0, 0]

    @pl.when(pl.program_id(1) == 0)
    def _():
        half = TM_FFN // 2
        for r0 in (0, half):
            rows = slice(r0, r0 + half)
            mix = (_dot(og_ref[rows, :], wo_ref[0, 0:GLA_WIDTH, :])
                   + _dot(oy_ref[rows, :], wo_ref[0, GLA_WIDTH:, :]))
            x_new = x_ref[rows, :] + mod[2:3] * mix
            o_ref[rows, :] = x_new
            h_scr[rows, :] = _norm_mod(x_new, nw_ref[...], mod[4:5], mod[3:4]).astype(BF16)

    o_ref[...] += mod[5:6] * _swiglu_chunk(h_scr[...], wg_ref[0], wu_ref[0], wd_ref[0])


def _outproj_ffn(x, og, oy, w_out, layer, mod_l, norm_w, w_gate, w_up, w_down, layer_idx):
    tiles_per_batch = SEQ // TM_FFN
    tile = lambda i, j: (i, 0)
    return pl.pallas_call(
        _ffn_kernel,
        out_shape=jax.ShapeDtypeStruct((TOKENS, D_MODEL), F32),
        grid=(TOKENS // TM_FFN, N_FF_CHUNKS),
        in_specs=[pl.BlockSpec((TM_FFN, D_MODEL), tile),
                  pl.BlockSpec((TM_FFN, GLA_WIDTH), tile),
                  pl.BlockSpec((TM_FFN, SSD_WIDTH), tile),
                  pl.BlockSpec((1, D_MODEL, D_MODEL), lambda i, j: (layer, 0, 0)),
                  pl.BlockSpec((1, 1, 6, D_MODEL), lambda i, j: (i // tiles_per_batch, 0, 0, 0)),
                  pl.BlockSpec((1, D_MODEL), lambda i, j: (0, 0)),
                  pl.BlockSpec((1, D_MODEL, TF_FFN), lambda i, j: (layer_idx, 0, j)),
                  pl.BlockSpec((1, D_MODEL, TF_FFN), lambda i, j: (layer_idx, 0, j)),
                  pl.BlockSpec((1, TF_FFN, D_MODEL), lambda i, j: (layer_idx, j, 0))],
        out_specs=pl.BlockSpec((TM_FFN, D_MODEL), tile),
        scratch_shapes=[pltpu.VMEM((TM_FFN, D_MODEL), BF16)],
        compiler_params=_params("arbitrary", "arbitrary"),
        name="outproj_ffn",
    )(x, og, oy, w_out, mod_l, norm_w, w_gate, w_up, w_down)


def _moe_plan(cnt):
    seg_len = (cnt + SEG_ALIGN - 1) // SEG_ALIGN * SEG_ALIGN
    loc_off = jnp.cumsum(seg_len, axis=1) - seg_len
    n_rows = seg_len.sum(axis=0)
    region = (n_rows + MOE_TILE - 1) // MOE_TILE * MOE_TILE
    base = jnp.cumsum(region) - region
    seg_start = base[None, :] + jnp.cumsum(seg_len, axis=0) - seg_len
    tiles_e = region // MOE_TILE
    tile_end = jnp.cumsum(tiles_e)
    n_act = tile_end[-1]
    r = jnp.arange(MOE_MAX_TILES, dtype=jnp.int32)
    r_act = jnp.minimum(r, n_act - 1)
    tile_exp = jnp.sum(r_act[:, None] >= tile_end[None, :], axis=1).astype(jnp.int32)
    rows_left = n_rows[tile_exp] - (r_act - (tile_end - tiles_e)[tile_exp]) * MOE_TILE
    n_sub = jnp.clip((rows_left + MOE_SUB - 1) // MOE_SUB, 0, MOE_TILE // MOE_SUB)
    n_sub = jnp.where(r < n_act, n_sub, 0)
    i32 = lambda a: a.reshape(-1).astype(jnp.int32)
    return dict(seg=(i32(seg_start), i32(loc_off), i32(seg_len)),
                fill=(i32(base + n_rows), i32(region - n_rows), i32(n_act)),
                tiles=(tile_exp, i32(r_act), i32(n_sub)))


def _segment_dma(src, dst, src_off, dst_off, length, sem, wait, advance_src=True):
    off = jnp.int32(0)
    for size in SEG_PIECES:
        take = (length & size) != 0
        s0 = pl.multiple_of(src_off + off, SEG_ALIGN) if advance_src else src_off
        d0 = pl.multiple_of(dst_off + off, SEG_ALIGN)

        @pl.when(take)
        def _():
            cp = pltpu.make_async_copy(src.at[pl.ds(s0, size)], dst.at[pl.ds(d0, size)], sem)
            if wait:
                cp.wait()
            else:
                cp.start()

        off = off + jnp.where(take, size, 0)


def _local_rows(i, lo_ref, sel1, sel2):
    tm = sel1.shape[0]
    tr = lax.broadcasted_iota(jnp.int32, (tm, tm), 0)
    tc = lax.broadcasted_iota(jnp.int32, (tm, tm), 1)
    earlier = jnp.where(tr > tc, 1.0, 0.0).astype(BF16)
    lane = lax.broadcasted_iota(jnp.int32, (1, LANES), 1)
    loc = jnp.zeros((1, LANES), F32)
    for e in range(N_EXPERTS):
        loc = jnp.where(lane == e, lo_ref[i * N_EXPERTS + e].astype(F32), loc)
    pos = _dot(earlier, (sel1 + sel2).astype(BF16)) + loc
    return jnp.sum(sel1 * pos, axis=-1, keepdims=True), jnp.sum(sel2 * pos, axis=-1, keepdims=True)


def _moe_sort_kernel(ss_ref, lo_ref, ln_ref, fs_ref, fl_ref, na_ref, h_ref, sel1_ref, sel2_ref, xs_ref,
                     loc_scr, zero_scr, sem, fill_sem):
    i = pl.program_id(0)
    last = pl.num_programs(0) - 1
    slot = lax.rem(i, 2)
    pos1, pos2 = _local_rows(i, lo_ref, sel1_ref[...], sel2_ref[...])
    cid = lax.broadcasted_iota(jnp.int32, (TM_PROJ, MOE_LOC_ROWS), 1).astype(F32)
    pick = jnp.where(cid == pos1, 1.0, jnp.where(cid == pos2, 1.0, 0.0)).astype(BF16)
    loc_scr[slot] = _dot_tn(pick, h_ref[...]).astype(BF16)

    def copies(tile, buf, wait):
        for e in range(N_EXPERTS):
            k = tile * N_EXPERTS + e
            _segment_dma(loc_scr.at[buf], xs_ref, lo_ref[k], ss_ref[k], ln_ref[k], sem.at[buf], wait)

    copies(i, slot, False)

    @pl.when(i > 0)
    def _():
        copies(i - 1, 1 - slot, True)

    @pl.when(i == last)
    def _():
        copies(i, slot, True)
        zero_scr[...] = jnp.zeros_like(zero_scr)
        n_act = na_ref[0]

        def fill_tiles(wait):
            def body(r, carry):
                for part in range(MOE_TILE // FILL_ROWS):
                    d0 = pl.multiple_of(r * MOE_TILE + part * FILL_ROWS, FILL_ROWS)
                    cp = pltpu.make_async_copy(zero_scr, xs_ref.at[pl.ds(d0, FILL_ROWS)], fill_sem)
                    if wait:
                        cp.wait()
                    else:
                        cp.start()
                return carry
            lax.fori_loop(n_act, MOE_MAX_TILES, body, 0)

        for wait in (False, True):
            for e in range(N_EXPERTS):
                _segment_dma(zero_scr, xs_ref, 0, fs_ref[e], fl_ref[e], fill_sem, wait, advance_src=False)
            fill_tiles(wait)


def _moe_sort(plan, h, sel1, sel2):
    tile = lambda i, *_: (i, 0)
    return pl.pallas_call(
        _moe_sort_kernel,
        out_shape=jax.ShapeDtypeStruct((MOE_MAX_TILES * MOE_TILE, D_MODEL), BF16),
        grid_spec=pltpu.PrefetchScalarGridSpec(
            num_scalar_prefetch=6,
            grid=(TOKENS // TM_PROJ,),
            in_specs=[pl.BlockSpec((TM_PROJ, D_MODEL), tile),
                      pl.BlockSpec((TM_PROJ, LANES), tile),
                      pl.BlockSpec((TM_PROJ, LANES), tile)],
            out_specs=pl.BlockSpec(memory_space=pl.ANY),
            scratch_shapes=[pltpu.VMEM((2, MOE_LOC_ROWS, D_MODEL), BF16),
                            pltpu.VMEM((FILL_ROWS, D_MODEL), BF16),
                            pltpu.SemaphoreType.DMA((2,)),
                            pltpu.SemaphoreType.DMA],
        ),
        compiler_params=_params("arbitrary"),
        name="moe_sort",
    )(*plan["seg"], *plan["fill"], h, sel1, sel2)


def _moe_ffn_kernel(te_ref, ra_ref, ns_ref, xs_ref, wg_ref, wu_ref, wd_ref, o_ref, acc_ref):
    del te_ref, ra_ref
    r, j = pl.program_id(0), pl.program_id(1)
    n_sub = ns_ref[r]
    n_parts = MOE_TILE // MOE_SUB
    for used in range(1, n_parts + 1):
        rows = slice(0, used * MOE_SUB)

        @pl.when(n_sub == used)
        def _():
            @pl.when(j == 0)
            def _():
                acc_ref[rows, :] = jnp.zeros((used * MOE_SUB, D_MODEL), F32)

            acc_ref[rows, :] += _swiglu_chunk(xs_ref[rows, :], wg_ref[0, 0], wu_ref[0, 0], wd_ref[0, 0])

            @pl.when(j == N_FF_MOE - 1)
            def _():
                o_ref[rows, :] = acc_ref[rows, :].astype(BF16)

    for s in range(n_parts):
        @pl.when(jnp.logical_and(s >= n_sub, j == N_FF_MOE - 1))
        def _():
            o_ref[s * MOE_SUB:(s + 1) * MOE_SUB, :] = jnp.zeros((MOE_SUB, D_MODEL), BF16)


def _moe_ffn(plan, xs, w_gate, w_up, w_down, layer_idx):
    last = N_FF_MOE - 1
    rows = lambda r, j, te, ra, ns: (ra[r], 0)
    chunk = lambda r, j, ns: jnp.where(ns[r] > 0, j, last)
    return pl.pallas_call(
        _moe_ffn_kernel,
        out_shape=jax.ShapeDtypeStruct(xs.shape, BF16),
        grid_spec=pltpu.PrefetchScalarGridSpec(
            num_scalar_prefetch=3,
            grid=(MOE_MAX_TILES, N_FF_MOE),
            in_specs=[pl.BlockSpec((MOE_TILE, D_MODEL), rows),
                      pl.BlockSpec((1, 1, D_MODEL, TF_MOE),
                                   lambda r, j, te, ra, ns: (layer_idx, te[r], 0, chunk(r, j, ns))),
                      pl.BlockSpec((1, 1, D_MODEL, TF_MOE),
                                   lambda r, j, te, ra, ns: (layer_idx, te[r], 0, chunk(r, j, ns))),
                      pl.BlockSpec((1, 1, TF_MOE, D_MODEL),
                                   lambda r, j, te, ra, ns: (layer_idx, te[r], chunk(r, j, ns), 0))],
            out_specs=pl.BlockSpec((MOE_TILE, D_MODEL), lambda r, j, *_: (r, 0)),
            scratch_shapes=[pltpu.VMEM((MOE_TILE, D_MODEL), F32)],
        ),
        compiler_params=_params("arbitrary", "arbitrary"),
        name="moe_ffn",
    )(*plan["tiles"], xs, w_gate, w_up, w_down)


def _moe_combine_kernel(ss_ref, lo_ref, ln_ref, x_ref, mod_ref, gates_ref, sel1_ref, sel2_ref, *rest,
                        final_norm):
    if final_norm:
        fw_ref, ys_ref, o_ref, loc_scr, sem = rest
    else:
        ys_ref, o_ref, loc_scr, sem = rest
    i = pl.program_id(0)
    tm = TM_PROJ
    slot = lax.rem(i, 2)

    def fetch(tile, buf, wait):
        for e in range(N_EXPERTS):
            k = tile * N_EXPERTS + e
            _segment_dma(ys_ref, loc_scr.at[buf], ss_ref[k], lo_ref[k], ln_ref[k], sem.at[buf], wait)

    @pl.when(i == 0)
    def _():
        loc_scr[...] = jnp.zeros_like(loc_scr)
        fetch(0, 0, False)

    @pl.when(i + 1 < pl.num_programs(0))
    def _():
        fetch(i + 1, 1 - slot, False)

    sel1, sel2, gates = sel1_ref[...], sel2_ref[...], gates_ref[...]
    pos1, pos2 = _local_rows(i, lo_ref, sel1, sel2)
    p1 = jnp.sum(sel1 * gates, axis=-1, keepdims=True)
    p2 = jnp.sum(sel2 * gates, axis=-1, keepdims=True)
    cid = lax.broadcasted_iota(jnp.int32, (tm, MOE_LOC_ROWS), 1).astype(F32)
    pick1 = jnp.where(cid == pos1, 1.0, 0.0).astype(BF16)
    pick2 = jnp.where(cid == pos2, 1.0, 0.0).astype(BF16)
    fetch(i, slot, True)
    y = loc_scr[slot]
    ff = p1 * _dot(pick1, y) + p2 * _dot(pick2, y)
    out = x_ref[...] + mod_ref[0, 0, 5:6, :] * ff
    if final_norm:
        ms = jnp.mean(out * out, axis=-1, keepdims=True)
        out = out * lax.rsqrt(ms + EPS) * fw_ref[...]
    o_ref[...] = out


def _moe_combine(plan, x, mod_l, gates, sel1, sel2, ys, final_w=None):
    tiles_per_batch = SEQ // TM_PROJ
    tile = lambda i, *_: (i, 0)
    final_norm = final_w is not None
    in_specs = [pl.BlockSpec((TM_PROJ, D_MODEL), tile),
                pl.BlockSpec((1, 1, 6, D_MODEL), lambda i, *_: (i // tiles_per_batch, 0, 0, 0)),
                pl.BlockSpec((TM_PROJ, LANES), tile),
                pl.BlockSpec((TM_PROJ, LANES), tile),
                pl.BlockSpec((TM_PROJ, LANES), tile)]
    args = [x, mod_l, gates, sel1, sel2]
    if final_norm:
        in_specs.append(pl.BlockSpec((1, D_MODEL), lambda i, *_: (0, 0)))
        args.append(final_w)
    return pl.pallas_call(
        functools.partial(_moe_combine_kernel, final_norm=final_norm),
        out_shape=jax.ShapeDtypeStruct((TOKENS, D_MODEL), F32),
        grid_spec=pltpu.PrefetchScalarGridSpec(
            num_scalar_prefetch=3,
            grid=(TOKENS // TM_PROJ,),
            in_specs=in_specs + [pl.BlockSpec(memory_space=pl.ANY)],
            out_specs=pl.BlockSpec((TM_PROJ, D_MODEL), tile),
            scratch_shapes=[pltpu.VMEM((2, MOE_LOC_ROWS, D_MODEL), BF16), pltpu.SemaphoreType.DMA((2,))],
        ),
        compiler_params=_params("arbitrary"),
        name="moe_combine_norm" if final_norm else "moe_combine",
    )(*plan["seg"], *args, ys)


def _final_norm_kernel(x_ref, w_ref, o_ref):
    x = x_ref[...]
    ms = jnp.mean(x * x, axis=-1, keepdims=True)
    o_ref[...] = x * lax.rsqrt(ms + EPS) * w_ref[...]


def _final_norm(x, w):
    return pl.pallas_call(
        _final_norm_kernel,
        out_shape=jax.ShapeDtypeStruct((TOKENS, D_MODEL), F32),
        grid=(TOKENS // TM_FFN,),
        in_specs=[pl.BlockSpec((TM_FFN, D_MODEL), lambda i: (i, 0)),
                  pl.BlockSpec((1, D_MODEL), lambda i: (0, 0))],
        out_specs=pl.BlockSpec((TM_FFN, D_MODEL), lambda i: (i, 0)),
        compiler_params=_params("arbitrary"),
        name="final_norm",
    )(x, w)


def _pad_cols(a, width):
    return jnp.pad(a, [(0, 0)] * (a.ndim - 1) + [(0, width - a.shape[-1])])


def _split_w_in(w_in):
    ssd_start = G_A + GLA_LOWRANK
    return w_in[:, :, :GLA_COLS].astype(BF16), _pad_cols(w_in[:, :, ssd_start:], SSD_COLS).astype(BF16)


def kernel(x, c, ada_w, ada_b, norm1_w, w_in, gla_a_w, gla_a_b, gla_norm_w, conv_w, conv_b, dt_bias, a_log,
           d_skip, ssd_norm_w, w_out, norm2_w, ffn_w_gate, ffn_w_up, ffn_w_down, router_w, moe_w_gate,
           moe_w_up, moe_w_down, final_norm_w):
    xt = x.reshape(TOKENS, D_MODEL)
    c_pad = jnp.pad(c, ((0, 8 - BATCH), (0, 0)))
    mod = _adaln(c_pad, ada_w, ada_b)[:, :BATCH].reshape(DEPTH, BATCH, 6, D_MODEL)

    w_gla, w_ssd = _split_w_in(w_in)
    w_out_bf = w_out.astype(BF16)
    a_w_p = jnp.pad(gla_a_w, ((0, 0), (0, LANES - GLA_LOWRANK), (0, 0)))
    dtb_p = _pad_cols(dt_bias, LANES)
    alog_p = _pad_cols(a_log, LANES)
    dsk_p = jnp.repeat(d_skip, SSD_HEADDIM, axis=-1)
    rw_p = _pad_cols(router_w, LANES)
    lane_head = jnp.arange(LANES)[:, None]
    e64 = (lane_head == jnp.arange(SSD_WIDTH)[None, :] // SSD_HEADDIM).astype(BF16)
    e128 = (lane_head == jnp.arange(SSD_HEADS * LANES)[None, :] // LANES).astype(BF16)

    for l in range(DEPTH):
        mod_l = mod[l].reshape(BATCH, 1, 6, D_MODEL)
        pg, ps = _inproj(xt, mod_l, norm1_w[l][None], w_gla, w_ssd, l)
        og = _gla(pg, a_w_p[l], gla_a_b[l][None], gla_norm_w[l][None])
        oy = _ssd(ps, conv_w[l], conv_b[l][None], dtb_p[l][None], alog_p[l][None], dsk_p[l][None],
                  ssd_norm_w[l][None], e64, e128)
        i = l // 2
        if l % 2 == 0:
            xt = _outproj_ffn(xt, og, oy, w_out_bf, l, mod_l, norm2_w[l][None], ffn_w_gate, ffn_w_up, ffn_w_down, i)
        else:
            xt, h2, gates, sel1, sel2, cnt = _outproj(xt, og, oy, w_out_bf, l, mod_l, norm2_w[l][None], rw_p[i])
            plan = _moe_plan(cnt[:, 0, :N_EXPERTS])
            xs = _moe_sort(plan, h2, sel1, sel2)
            ys = _moe_ffn(plan, xs, moe_w_gate, moe_w_up, moe_w_down, i)
            last = l == DEPTH - 1
            xt = _moe_combine(plan, xt, mod_l, gates, sel1, sel2, ys, final_norm_w[None] if last else None)
    if DEPTH % 2:
        xt = _final_norm(xt, final_norm_w[None])
    return xt.reshape(BATCH, SEQ, D_MODEL)
```
